```python
import math
import jax
import jax.numpy as jnp
from jax import lax
import numpy as np


D_MODEL = 1024
BATCH = 2
SEQ = 16384
DEPTH = 4

CTX_LEN = 256
GRID_W = 64
N_MIXERS = 4
Q_BLOCK = 128
WINDOW = 128
ROPE_THETA = 10000.0
NORM_EPS = 1e-6
WIDTH = D_MODEL

A_HEADS = 16
A_Q_LORA = 256
A_KV_LORA = 128
A_NOPE = 64
A_ROPE = 32
A_V = 64
A_IN = A_Q_LORA + A_KV_LORA + A_ROPE + WIDTH

B_HEADS = 8
B_HEAD = 64
B_IN = 3 * (2 * B_HEADS * B_HEAD) + WIDTH

C_HEADS = 8
C_KV_HEADS = 2
C_HEAD = 128
C_IN = (C_HEADS + 2 * C_KV_HEADS) * C_HEAD + WIDTH

D_HEADS = 16
D_KV_HEADS = 2
D_HEAD = 64
D_IN = (D_HEADS + 2 * D_KV_HEADS) * D_HEAD + WIDTH

DEEPNORM_ALPHA = (2 * DEPTH) ** 0.25
DEEPNORM_BETA = (8 * DEPTH) ** -0.25

kernel_name = 'hybrid_interleaved_mla_diff_gqa_swa_dit'


def rms_norm(x, g):
    xf = x.astype(jnp.float32)
    y = xf * lax.rsqrt(jnp.mean(xf * xf, axis=-1, keepdims=True) + NORM_EPS)
    return (y * g.astype(jnp.float32)).astype(x.dtype)


def layer_norm(x, g, b):
    xf = x.astype(jnp.float32)
    xc = xf - jnp.mean(xf, axis=-1, keepdims=True)
    var = jnp.mean(xc * xc, axis=-1, keepdims=True)
    y = xc * lax.rsqrt(var + NORM_EPS) * g.astype(jnp.float32) + b.astype(jnp.float32)
    return y.astype(x.dtype)


def axial_rope_tables(rows, rot_dim):
    row = jnp.repeat(jnp.arange(rows, dtype=jnp.float32), GRID_W)
    col = jnp.tile(jnp.arange(GRID_W, dtype=jnp.float32), rows)
    n_freq = rot_dim // 4
    inv_freq = ROPE_THETA ** (-jnp.arange(n_freq, dtype=jnp.float32) / n_freq)
    ang = jnp.concatenate([row[:, None] * inv_freq, col[:, None] * inv_freq], axis=-1)
    return jnp.cos(ang)[:, None, :], jnp.sin(ang)[:, None, :]


def apply_rope(x, cos, sin):
    half = x.shape[-1] // 2
    x1 = x[..., :half].astype(jnp.float32)
    x2 = x[..., half:].astype(jnp.float32)
    return jnp.concatenate([x1 * cos - x2 * sin, x2 * cos + x1 * sin], axis=-1).astype(x.dtype)


def adaln(cvec, w, b):
    m = jax.nn.silu(cvec) @ w + b
    return jnp.split(m, 3, axis=-1)


def gated(o, gate):
    return o.reshape(gate.shape) * jax.nn.silu(gate)


def dense_attention(q, k, v, scale):
    bsz, s_len, hk, g, d = q.shape
    dv = v.shape[-1]
    nb = s_len // Q_BLOCK
    q_blocks = jnp.moveaxis(q.reshape(bsz, nb, Q_BLOCK, hk, g, d), 1, 0)

    def one_block(qb):
        s = jnp.einsum('bqhgd,bkhd->bhgqk', qb, k).astype(jnp.float32) * scale
        p = jax.nn.softmax(s, axis=-1).astype(v.dtype)
        return jnp.einsum('bhgqk,bkhv->bqhgv', p, v)

    o = lax.map(one_block, q_blocks)
    return jnp.moveaxis(o, 0, 1).reshape(bsz, s_len, hk, g, dv)


def prefix_attention(q, k, v, q_c, k_c, v_c, scale, need_ctx):
    o = dense_attention(q, jnp.concatenate([k, k_c], axis=1), jnp.concatenate([v, v_c], axis=1), scale)
    o_c = dense_attention(q_c, k_c, v_c, scale) if need_ctx else None
    return o, o_c


def window_sink_attention(q, k, v, k_c, v_c, sinks, scale):
    bsz, s_len, hk, g, d = q.shape
    dv = v.shape[-1]
    nb = s_len // Q_BLOCK
    kw = 3 * Q_BLOCK
    pad = ((0, 0), (Q_BLOCK, Q_BLOCK), (0, 0), (0, 0))
    k_pad = jnp.pad(k, pad)
    v_pad = jnp.pad(v, pad)
    offs_q = jnp.arange(Q_BLOCK)
    offs_k = jnp.arange(kw) - Q_BLOCK
    in_band = jnp.abs(offs_k[None, :] - offs_q[:, None]) <= WINDOW
    sink_logit = sinks.reshape(hk, g).astype(jnp.float32)[None, :, :, None, None]

    def one_block(n):
        start = n * Q_BLOCK
        qb = lax.dynamic_slice_in_dim(q, start, Q_BLOCK, axis=1)
        kb = lax.dynamic_slice_in_dim(k_pad, start, kw, axis=1)
        vb = lax.dynamic_slice_in_dim(v_pad, start, kw, axis=1)
        kpos = start + offs_k
        valid = in_band & ((kpos >= 0) & (kpos < s_len))[None, :]
        s_loc = jnp.einsum('bqhgd,bkhd->bhgqk', qb, kb).astype(jnp.float32) * scale
        s_loc = jnp.where(valid, s_loc, -jnp.inf)
        s_ctx = jnp.einsum('bqhgd,bkhd->bhgqk', qb, k_c).astype(jnp.float32) * scale
        s_sink = jnp.broadcast_to(sink_logit, s_loc.shape[:-1] + (1,))
        p = jax.nn.softmax(jnp.concatenate([s_loc, s_ctx, s_sink], axis=-1), axis=-1).astype(v.dtype)
        o_loc = jnp.einsum('bhgqk,bkhv->bqhgv', p[..., :kw], vb)
        o_ctx = jnp.einsum('bhgqk,bkhv->bqhgv', p[..., kw:kw + k_c.shape[1]], v_c)
        return o_loc + o_ctx

    o = lax.map(one_block, jnp.arange(nb))
    return jnp.moveaxis(o, 0, 1).reshape(bsz, s_len, hk, g, dv)


def context_sink_attention(q_c, k_c, v_c, sinks, scale):
    hk, g = q_c.shape[2], q_c.shape[3]
    s = jnp.einsum('bqhgd,bkhd->bhgqk', q_c, k_c).astype(jnp.float32) * scale
    s_sink = jnp.broadcast_to(sinks.reshape(hk, g).astype(jnp.float32)[None, :, :, None, None], s.shape[:-1] + (1,))
    p = jax.nn.softmax(jnp.concatenate([s, s_sink], axis=-1), axis=-1)[..., :-1].astype(v_c.dtype)
    return jnp.einsum('bhgqk,bkhv->bqhgv', p, v_c)


def mla_mixer(h, h_c, rows, need_ctx, w_in, g_qa, w_qb, g_kva, w_kvb):
    cos, sin = axial_rope_tables(rows, A_ROPE)
    splits = [A_Q_LORA, A_Q_LORA + A_KV_LORA, A_Q_LORA + A_KV_LORA + A_ROPE]

    def project(t, rope):
        bsz, n = t.shape[:2]
        q_lat, kv_lat, k_pe, gate = jnp.split(t @ w_in, splits, axis=-1)
        q = (rms_norm(q_lat, g_qa) @ w_qb).reshape(bsz, n, A_HEADS, A_NOPE + A_ROPE)
        kv = (rms_norm(kv_lat, g_kva) @ w_kvb).reshape(bsz, n, A_HEADS, A_NOPE + A_V)
        q_nope, q_pe = q[..., :A_NOPE], q[..., A_NOPE:]
        k_nope, v = kv[..., :A_NOPE], kv[..., A_NOPE:]
        k_pe = k_pe[:, :, None, :]
        if rope:
            q_pe = apply_rope(q_pe, cos, sin)
            k_pe = apply_rope(k_pe, cos, sin)
        q = jnp.concatenate([q_nope, q_pe], axis=-1)[:, :, :, None, :]
        k = jnp.concatenate([k_nope, jnp.broadcast_to(k_pe, (bsz, n, A_HEADS, A_ROPE))], axis=-1)
        return q, k, v, gate

    q, k, v, gate = project(h, True)
    q_c, k_c, v_c, gate_c = project(h_c, False)
    o, o_c = prefix_attention(q, k, v, q_c, k_c, v_c, (A_NOPE + A_ROPE) ** -0.5, need_ctx)
    return gated(o, gate), (gated(o_c, gate_c) if need_ctx else None)


def diff_mixer(h, h_c, rows, need_ctx, layer_idx, w_in, lam, g_sub):
    cos, sin = axial_rope_tables(rows, B_HEAD)
    lambda_init = 0.8 - 0.6 * math.exp(-0.3 * layer_idx)
    lam = lam.astype(jnp.float32)
    lam_full = jnp.exp(jnp.sum(lam[0] * lam[1])) - jnp.exp(jnp.sum(lam[2] * lam[3])) + lambda_init
    scale = B_HEAD ** -0.5

    def project(t, rope):
        bsz, n = t.shape[:2]
        q, k, v, gate = jnp.split(t @ w_in, 4, axis=-1)
        q = q.reshape(bsz, n, 2 * B_HEADS, B_HEAD)
        k = k.reshape(bsz, n, 2 * B_HEADS, B_HEAD)
        if rope:
            q = apply_rope(q, cos, sin)
            k = apply_rope(k, cos, sin)
        q = q.reshape(bsz, n, B_HEADS, 2, B_HEAD)
        k = k.reshape(bsz, n, B_HEADS, 2, B_HEAD)
        v = v.reshape(bsz, n, B_HEADS, 2 * B_HEAD)
        return q, k, v, gate

    def combine(o1, o2, gate):
        o = o1 - lam_full.astype(o1.dtype) * o2
        o = rms_norm(o, g_sub) * (1.0 - lambda_init)
        return gated(o, gate)

    q, k, v, gate = project(h, True)
    q_c, k_c, v_c, gate_c = project(h_c, False)
    o1, o1_c = prefix_attention(q[:, :, :, 0:1], k[:, :, :, 0], v, q_c[:, :, :, 0:1], k_c[:, :, :, 0], v_c, scale, need_ctx)
    o2, o2_c = prefix_attention(q[:, :, :, 1:2], k[:, :, :, 1], v, q_c[:, :, :, 1:2], k_c[:, :, :, 1], v_c, scale, need_ctx)
    return combine(o1, o2, gate), (combine(o1_c, o2_c, gate_c) if need_ctx else None)


def qknorm_gqa_mixer(h, h_c, rows, need_ctx, w_in, g_q, g_k):
    cos, sin = axial_rope_tables(rows, C_HEAD)
    splits = [C_HEADS * C_HEAD, (C_HEADS + C_KV_HEADS) * C_HEAD, (C_HEADS + 2 * C_KV_HEADS) * C_HEAD]

    def project(t, rope):
        bsz, n = t.shape[:2]
        q, k, v, gate = jnp.split(t @ w_in, splits, axis=-1)
        q = rms_norm(q.reshape(bsz, n, C_HEADS, C_HEAD), g_q)
        k = rms_norm(k.reshape(bsz, n, C_KV_HEADS, C_HEAD), g_k)
        v = v.reshape(bsz, n, C_KV_HEADS, C_HEAD)
        if rope:
            q = apply_rope(q, cos, sin)
            k = apply_rope(k, cos, sin)
        q = q.reshape(bsz, n, C_KV_HEADS, C_HEADS // C_KV_HEADS, C_HEAD)
        return q, k, v, gate

    q, k, v, gate = project(h, True)
    q_c, k_c, v_c, gate_c = project(h_c, False)
    o, o_c = prefix_attention(q, k, v, q_c, k_c, v_c, C_HEAD ** -0.5, need_ctx)
    return gated(o, gate), (gated(o_c, gate_c) if need_ctx else None)


def swa_sink_mixer(h, h_c, rows, need_ctx, w_in, sinks):
    cos, sin = axial_rope_tables(rows, D_HEAD)
    splits = [D_HEADS * D_HEAD, (D_HEADS + D_KV_HEADS) * D_HEAD, (D_HEADS + 2 * D_KV_HEADS) * D_HEAD]
    scale = D_HEAD ** -0.5

    def project(t, rope):
        bsz, n = t.shape[:2]
        q, k, v, gate = jnp.split(t @ w_in, splits, axis=-1)
        q = q.reshape(bsz, n, D_HEADS, D_HEAD)
        k = k.reshape(bsz, n, D_KV_HEADS, D_HEAD)
        v = v.reshape(bsz, n, D_KV_HEADS, D_HEAD)
        if rope:
            q = apply_rope(q, cos, sin)
            k = apply_rope(k, cos, sin)
        q = q.reshape(bsz, n, D_KV_HEADS, D_HEADS // D_KV_HEADS, D_HEAD)
        return q, k, v, gate

    q, k, v, gate = project(h, True)
    q_c, k_c, v_c, gate_c = project(h_c, False)
    o = window_sink_attention(q, k, v, k_c, v_c, sinks, scale)
    o_c = gated(context_sink_attention(q_c, k_c, v_c, sinks, scale), gate_c) if need_ctx else None
    return gated(o, gate), o_c


def setup_inputs(seed: int = 0) -> dict:
    key = jax.random.key(seed)
    ks = iter(jax.random.split(key, 32))

    def nrm(shape, std):
        return jax.random.normal(next(ks), shape, jnp.float32) * std

    def gain(shape):
        return 1.0 + nrm(shape, 0.02)

    n_a = len(range(0, DEPTH, N_MIXERS))
    n_b = len(range(1, DEPTH, N_MIXERS))
    n_c = len(range(2, DEPTH, N_MIXERS))
    n_d = len(range(3, DEPTH, N_MIXERS))
    return {
        'x': nrm((BATCH, SEQ, D_MODEL), 1.0),
        'c': nrm((BATCH, D_MODEL), 1.0),
        'ctx': nrm((BATCH, CTX_LEN, D_MODEL), 1.0),
        'c_ctx': nrm((D_MODEL,), 1.0),
        'ada_w': nrm((DEPTH, D_MODEL, 3 * D_MODEL), 0.02),
        'ada_b': nrm((DEPTH, 3 * D_MODEL), 0.02),
        'out_w': nrm((DEPTH, WIDTH, D_MODEL), DEEPNORM_BETA * WIDTH ** -0.5),
        'ln_g': gain((DEPTH, D_MODEL)),
        'ln_b': nrm((DEPTH, D_MODEL), 0.02),
        'mla_w_in': nrm((n_a, D_MODEL, A_IN), D_MODEL ** -0.5),
        'mla_g_qa': gain((n_a, A_Q_LORA)),
        'mla_w_qb': nrm((n_a, A_Q_LORA, A_HEADS * (A_NOPE + A_ROPE)), A_Q_LORA ** -0.5),
        'mla_g_kva': gain((n_a, A_KV_LORA)),
        'mla_w_kvb': nrm((n_a, A_KV_LORA, A_HEADS * (A_NOPE + A_V)), A_KV_LORA ** -0.5),
        'diff_w_in': nrm((n_b, D_MODEL, B_IN), D_MODEL ** -0.5),
        'diff_lambda': nrm((n_b, 4, B_HEAD), 0.1),
        'diff_g_sub': gain((n_b, 2 * B_HEAD)),
        'gqa_w_in': nrm((n_c, D_MODEL, C_IN), D_MODEL ** -0.5),
        'gqa_g_q': gain((n_c, C_HEAD)),
        'gqa_g_k': gain((n_c, C_HEAD)),
        'swa_w_in': nrm((n_d, D_MODEL, D_IN), D_MODEL ** -0.5),
        'swa_sink': nrm((n_d, D_HEADS), 0.5),
    }


def reference(x, c, ctx, c_ctx, ada_w, ada_b, out_w, ln_g, ln_b,
              mla_w_in, mla_g_qa, mla_w_qb, mla_g_kva, mla_w_kvb,
              diff_w_in, diff_lambda, diff_g_sub,
              gqa_w_in, gqa_g_q, gqa_g_k,
              swa_w_in, swa_sink):
    ROWS = x.shape[1] // GRID_W
    for i in range(DEPTH):
        kind, j = i % N_MIXERS, i // N_MIXERS
        need_ctx = i < DEPTH - 1
        shift, scale, gate = adaln(c, ada_w[i], ada_b[i])
        shift_c, scale_c, gate_c = adaln(c_ctx, ada_w[i], ada_b[i])
        h = x * (1.0 + scale[:, None, :]) + shift[:, None, :]
        h_c = ctx * (1.0 + scale_c) + shift_c
        if kind == 0:
            o, o_c = mla_mixer(h, h_c, ROWS, need_ctx, mla_w_in[j], mla_g_qa[j], mla_w_qb[j], mla_g_kva[j], mla_w_kvb[j])
        elif kind == 1:
            o, o_c = diff_mixer(h, h_c, ROWS, need_ctx, i, diff_w_in[j], diff_lambda[j], diff_g_sub[j])
        elif kind == 2:
            o, o_c = qknorm_gqa_mixer(h, h_c, ROWS, need_ctx, gqa_w_in[j], gqa_g_q[j], gqa_g_k[j])
        else:
            o, o_c = swa_sink_mixer(h, h_c, ROWS, need_ctx, swa_w_in[j], swa_sink[j])
        x = layer_norm(DEEPNORM_ALPHA * x + gate[:, None, :] * (o @ out_w[i]), ln_g[i], ln_b[i])
        if need_ctx:
            ctx = layer_norm(DEEPNORM_ALPHA * ctx + gate_c * (o_c @ out_w[i]), ln_g[i], ln_b[i])
    return x
```

```python
import functools
import math

import jax
import jax.numpy as jnp
from jax import lax
from jax.experimental import pallas as pl
from jax.experimental.pallas import tpu as pltpu

D_MODEL = 1024
GRID_W = 64
WINDOW = 128
ROPE_THETA = 10000.0
NORM_EPS = 1e-6
A_HEADS, A_Q_LORA, A_KV_LORA, A_NOPE, A_ROPE, A_V = 16, 256, 128, 64, 32, 64
B_HEADS, B_HEAD = 8, 64
C_HEADS, C_KV_HEADS, C_HEAD = 8, 2, 128
D_HEADS, D_KV_HEADS, D_HEAD = 16, 2, 64

LANE = 128
HALF = LANE // 2
LOG2E = math.log2(math.e)
NEG_BIG = -1e30
VMEM_LIMIT = 56 * 1024 * 1024
BF16 = jnp.bfloat16
F32 = jnp.float32


def _cparams(n_grid):
    return pltpu.CompilerParams(dimension_semantics=("arbitrary",) * n_grid, vmem_limit_bytes=VMEM_LIMIT)


def _silu(x):
    return x / (1.0 + jnp.exp(-x))


def _dot(a, b):
    return jnp.dot(a, b, preferred_element_type=F32)


def _dot_nt(a, b):
    return lax.dot_general(a, b, (((1,), (1,)), ((), ())), preferred_element_type=F32)


def _adaln_kernel(c_ref, w_ref, b_ref, o_ref):
    sc = _silu(c_ref[...])
    o_ref[0] = jnp.dot(sc, w_ref[0], preferred_element_type=F32, precision=lax.Precision.HIGHEST) + b_ref[0]


def _adaln_call(cvec, ada_w, ada_b):
    depth, d, n = ada_w.shape
    rows = cvec.shape[0]
    tn = 1024
    return pl.pallas_call(
        _adaln_kernel,
        grid=(depth, n // tn),
        in_specs=[
            pl.BlockSpec((rows, d), lambda i, j: (0, 0)),
            pl.BlockSpec((1, d, tn), lambda i, j: (i, 0, j)),
            pl.BlockSpec((1, 1, tn), lambda i, j: (i, 0, j)),
        ],
        out_specs=pl.BlockSpec((1, rows, tn), lambda i, j: (i, 0, j)),
        out_shape=jax.ShapeDtypeStruct((depth, rows, n), F32),
        compiler_params=_cparams(2),
        name="adaln",
    )(cvec, ada_w, ada_b.reshape(depth, 1, n))


def _rope_tables(t_len, rot_dim, layout):
    rows = t_len // GRID_W
    row = jnp.repeat(jnp.arange(rows, dtype=F32), GRID_W)
    col = jnp.tile(jnp.arange(GRID_W, dtype=F32), rows)
    n_freq = rot_dim // 4
    inv_freq = ROPE_THETA ** (-jnp.arange(n_freq, dtype=F32) / n_freq)
    ang = jnp.concatenate([row[:, None] * inv_freq, col[:, None] * inv_freq], axis=-1)
    cos, sin = jnp.cos(ang), jnp.sin(ang)
    zero = jnp.zeros_like(sin)
    if layout == "mla":
        one = jnp.ones((t_len, A_NOPE), F32)
        pad = jnp.zeros((t_len, LANE - A_NOPE - A_ROPE), F32)
        c = jnp.concatenate([one, cos, cos, pad], axis=-1)
        s_lo = jnp.concatenate([0 * one, -sin, zero, pad], axis=-1)
        s_hi = jnp.concatenate([0 * one, zero, sin, pad], axis=-1)
    else:
        reps = LANE // rot_dim
        c = jnp.tile(jnp.concatenate([cos, cos], axis=-1), (1, reps))
        s_lo = jnp.tile(jnp.concatenate([-sin, zero], axis=-1), (1, reps))
        s_hi = jnp.tile(jnp.concatenate([zero, sin], axis=-1), (1, reps))
    return c, s_lo, s_hi


def _rope_slab(x, c, s_lo, s_hi, half):
    if half == HALF:
        return x * c + pltpu.roll(x, HALF, 1) * (s_lo + s_hi)
    return x * c + pltpu.roll(x, LANE - half, 1) * s_lo + pltpu.roll(x, half, 1) * s_hi


def _modulate(x_ref, mod_ref):
    d = x_ref.shape[-1]
    shift = mod_ref[0, :, 0:d]
    scale = mod_ref[0, :, d:2 * d]
    return (x_ref[0] * (1.0 + scale) + shift).astype(BF16)


def _rms(x, g):
    return x * lax.rsqrt(jnp.mean(x * x, axis=-1, keepdims=True) + NORM_EPS) * g


PROJ_CHUNK = 512


def _proj_kernel(*refs, segs, rope_half, has_rope, has_norm):
    it = iter(refs)
    x_ref, mod_ref = next(it), next(it)
    if has_rope:
        c = next(it)[...]
        s_lo = next(it)[...]
        s_hi = next(it)[...]
    w_ref = next(it)
    gn_ref = next(it) if has_norm else None
    out_refs = list(it)
    hb = _modulate(x_ref, mod_ref)
    for (oi, col0, width, kind, q_scale, norm_row) in segs:
        for cc in range(0, width, PROJ_CHUNK):
            cw = min(PROJ_CHUNK, width - cc)
            t = _dot(hb, w_ref[:, col0 + cc:col0 + cc + cw])
            if kind == "silu":
                y = _silu(t)
            elif kind == "plain":
                y = t
            else:
                slabs = []
                for s in range(cw // LANE):
                    xs = t[:, s * LANE:(s + 1) * LANE]
                    if norm_row is not None:
                        xs = _rms(xs, gn_ref[norm_row:norm_row + 1, :])
                    if has_rope:
                        xs = _rope_slab(xs, c, s_lo, s_hi, rope_half)
                    if q_scale != 1.0:
                        xs = xs * q_scale
                    slabs.append(xs)
                y = jnp.concatenate(slabs, axis=1) if len(slabs) > 1 else slabs[0]
            out_refs[oi][0, :, cc:cc + cw] = y.astype(BF16)


def _proj_call(x, mod, tabs, w, gains, segs, out_widths, rope_half, tm, name):
    bsz, t_len, d = x.shape
    has_rope = tabs is not None
    has_norm = gains is not None
    args = [x, mod]
    in_specs = [
        pl.BlockSpec((1, tm, d), lambda b, i: (b, i, 0)),
        pl.BlockSpec((1, 1, mod.shape[-1]), lambda b, i: (b, 0, 0)),
    ]
    if has_rope:
        for tab in tabs:
            args.append(tab)
            in_specs.append(pl.BlockSpec((tm, LANE), lambda b, i: (i, 0)))
    args.append(w)
    in_specs.append(pl.BlockSpec(w.shape, lambda b, i: (0, 0)))
    if has_norm:
        args.append(gains)
        in_specs.append(pl.BlockSpec(gains.shape, lambda b, i: (0, 0)))
    return pl.pallas_call(
        functools.partial(_proj_kernel, segs=segs, rope_half=rope_half, has_rope=has_rope, has_norm=has_norm),
        grid=(bsz, t_len // tm),
        in_specs=in_specs,
        out_specs=[pl.BlockSpec((1, tm, ow), lambda b, i: (b, i, 0)) for ow in out_widths],
        out_shape=[jax.ShapeDtypeStruct((bsz, t_len, ow), BF16) for ow in out_widths],
        compiler_params=_cparams(2),
        name=name,
    )(*args)


def _mla_proj_kernel(*refs, has_rope, q_scale):
    it = iter(refs)
    x_ref, mod_ref = next(it), next(it)
    if has_rope:
        c = next(it)[...]
        s_lo = next(it)[...]
        s_hi = next(it)[...]
    w_in_ref, gqa_ref, w_qb_ref, gkva_ref, w_kvb_ref = next(it), next(it), next(it), next(it), next(it)
    q_ref, k_ref, v_ref, g_ref = next(it), next(it), next(it), next(it)
    half = A_ROPE // 2
    n_lat = A_Q_LORA + A_KV_LORA + LANE
    hb = _modulate(x_ref, mod_ref)
    t = _dot(hb, w_in_ref[:, 0:n_lat])
    qn = _rms(t[:, 0:A_Q_LORA], gqa_ref[...]).astype(BF16)
    kn = _rms(t[:, A_Q_LORA:A_Q_LORA + A_KV_LORA], gkva_ref[...]).astype(BF16)
    kpe = t[:, A_Q_LORA + A_KV_LORA:n_lat]
    if has_rope:
        kpe = _rope_slab(kpe, c, s_lo, s_hi, half)
    n_qk = A_HEADS * LANE
    for cc in range(0, n_qk, PROJ_CHUNK):
        tq = _dot(qn, w_qb_ref[:, cc:cc + PROJ_CHUNK])
        tk = _dot(kn, w_kvb_ref[:, cc:cc + PROJ_CHUNK])
        qs, ks = [], []
        for s in range(PROJ_CHUNK // LANE):
            xq = tq[:, s * LANE:(s + 1) * LANE]
            if has_rope:
                xq = _rope_slab(xq, c, s_lo, s_hi, half)
            qs.append(xq * q_scale)
            ks.append(tk[:, s * LANE:(s + 1) * LANE] + kpe)
        q_ref[0, :, cc:cc + PROJ_CHUNK] = jnp.concatenate(qs, axis=1).astype(BF16)
        k_ref[0, :, cc:cc + PROJ_CHUNK] = jnp.concatenate(ks, axis=1).astype(BF16)
    n_v = A_HEADS * A_V
    for cc in range(0, n_v, PROJ_CHUNK):
        v_ref[0, :, cc:cc + PROJ_CHUNK] = _dot(kn, w_kvb_ref[:, n_qk + cc:n_qk + cc + PROJ_CHUNK]).astype(BF16)
    for cc in range(0, D_MODEL, PROJ_CHUNK):
        g_ref[0, :, cc:cc + PROJ_CHUNK] = _silu(_dot(hb, w_in_ref[:, n_lat + cc:n_lat + cc + PROJ_CHUNK])).astype(BF16)


def _mla_proj_call(x, mod, tabs, w_in, g_qa, w_qb, g_kva, w_kvb, q_scale, tm, name):
    bsz, t_len, d = x.shape
    has_rope = tabs is not None
    args = [x, mod]
    in_specs = [
        pl.BlockSpec((1, tm, d), lambda b, i: (b, i, 0)),
        pl.BlockSpec((1, 1, mod.shape[-1]), lambda b, i: (b, 0, 0)),
    ]
    if has_rope:
        for tab in tabs:
            args.append(tab)
            in_specs.append(pl.BlockSpec((tm, LANE), lambda b, i: (i, 0)))
    for a in (w_in, g_qa, w_qb, g_kva, w_kvb):
        args.append(a)
        in_specs.append(pl.BlockSpec(a.shape, lambda b, i: (0, 0)))
    out_widths = (A_HEADS * LANE, A_HEADS * LANE, A_HEADS * A_V, D_MODEL)
    return pl.pallas_call(
        functools.partial(_mla_proj_kernel, has_rope=has_rope, q_scale=q_scale),
        grid=(bsz, t_len // tm),
        in_specs=in_specs,
        out_specs=[pl.BlockSpec((1, tm, ow), lambda b, i: (b, i, 0)) for ow in out_widths],
        out_shape=[jax.ShapeDtypeStruct((bsz, t_len, ow), BF16) for ow in out_widths],
        compiler_params=_cparams(2),
        name=name,
    )(*args)


def _lane_lo(shape):
    return lax.broadcasted_iota(jnp.int32, shape, len(shape) - 1) < HALF


def _dense_attn_kernel(*refs, mode, n_q, n_k, rows, tq, tk, n_chunks, lambda_init):
    it = iter(refs)
    q_refs = [next(it) for _ in range(n_q)]
    kc_refs = [next(it) for _ in range(n_k)]
    vc_ref = next(it)
    if n_chunks:
        k_refs = [next(it) for _ in range(n_k)]
        v_ref = next(it)
    if mode == "diff":
        lam_ref, gsub_ref = next(it), next(it)
    o_ref = next(it)
    qs_scr, m_scr, l_scr, acc_scr = next(it), next(it), next(it), next(it)

    n_rows = len(rows)
    for r, (qi, mask, _) in enumerate(rows):
        q = q_refs[qi][0]
        if mask is not None:
            lo = _lane_lo(q.shape)
            q = jnp.where(lo if mask == "lo" else jnp.logical_not(lo), q, jnp.zeros_like(q))
        qs_scr[r * tq:(r + 1) * tq, :] = q
    spans = []
    for ks in range(n_k):
        idx = [r for r, row in enumerate(rows) if row[2] == ks]
        spans.append((idx[0] * tq, (idx[-1] + 1) * tq))

    def process(k_list, v, first):
        width = k_list[0].shape[0]
        p_parts, alphas = [], []
        for ks, (r0, r1) in enumerate(spans):
            s = _dot_nt(qs_scr[r0:r1, :], k_list[ks])
            cols = [s[:, j * LANE:(j + 1) * LANE] for j in range(width // LANE)]
            mp = cols[0]
            for cj in cols[1:]:
                mp = jnp.maximum(mp, cj)
            smax = jnp.max(mp, axis=1, keepdims=True)
            if first:
                m_new = jnp.broadcast_to(smax, (r1 - r0, LANE))
            else:
                m_prev = m_scr[r0:r1, :]
                m_new = jnp.maximum(m_prev, smax)
            lsum = None
            ps = []
            for cj in cols:
                pj = jnp.exp2(cj - m_new)
                lsum = pj if lsum is None else lsum + pj
                ps.append(pj.astype(BF16))
            p_parts.append(jnp.concatenate(ps, axis=1) if len(ps) > 1 else ps[0])
            if first:
                l_scr[r0:r1, :] = lsum
            else:
                alpha = jnp.exp2(m_prev - m_new)
                l_scr[r0:r1, :] = alpha * l_scr[r0:r1, :] + lsum
                alphas.append(alpha)
            m_scr[r0:r1, :] = m_new
        p = jnp.concatenate(p_parts, axis=0) if len(p_parts) > 1 else p_parts[0]
        pv = _dot(p, v)
        if first:
            acc_scr[...] = pv
        else:
            alpha = jnp.concatenate(alphas, axis=0) if len(alphas) > 1 else alphas[0]
            acc_scr[...] = alpha * acc_scr[...] + pv

    process([r[0] for r in kc_refs], vc_ref[0], True)

    if n_chunks:
        def step(ci, carry):
            off = pl.multiple_of(ci * tk, tk)
            process([r[0, pl.ds(off, tk), :] for r in k_refs], v_ref[0, pl.ds(off, tk), :], False)
            return carry
        lax.fori_loop(0, n_chunks, step, 0)

    l = jnp.sum(l_scr[...], axis=1, keepdims=True)
    o = acc_scr[...] * (1.0 / l)
    if mode == "pair":
        out = jnp.where(_lane_lo((tq, LANE)), o[0:tq], o[tq:2 * tq])
    elif mode == "diff":
        lam = lam_ref[...]
        a = jnp.sum(lam[0:1, :] * lam[1:2, :], axis=1, keepdims=True)
        b = jnp.sum(lam[2:3, :] * lam[3:4, :], axis=1, keepdims=True)
        lam_full = jnp.exp(a) - jnp.exp(b) + lambda_init
        dlt = o[0:tq] - lam_full * o[tq:2 * tq]
        out = _rms(dlt, gsub_ref[...]) * (1.0 - lambda_init)
    else:
        out = jnp.concatenate([o[r * tq:(r + 1) * tq] for r in range(n_rows)], axis=1)
    o_ref[0] = out.astype(o_ref.dtype)


def _dense_attn_call(mode, q, kc, vc, k, v, extras, n_groups, q_map, k_map, v_map, rows, out_width, tq, tk,
                     lambda_init, name):
    bsz, t_q, _ = q.shape
    t_c = kc.shape[1]
    n_q = 1 + max(r[0] for r in rows)
    n_k = 1 + max(r[2] for r in rows)
    n_rows = len(rows)
    has_latent = k is not None
    n_chunks = (k.shape[1] // tk) if has_latent else 0
    args, in_specs = [], []
    for j in range(n_q):
        args.append(q)
        in_specs.append(pl.BlockSpec((1, tq, LANE), lambda b, g, i, j=j: (b, i, q_map(g, j))))
    for j in range(n_k):
        args.append(kc)
        in_specs.append(pl.BlockSpec((1, t_c, LANE), lambda b, g, i, j=j: (b, 0, k_map(g, j))))
    args.append(vc)
    in_specs.append(pl.BlockSpec((1, t_c, LANE), lambda b, g, i: (b, 0, v_map(g))))
    if has_latent:
        t_k = k.shape[1]
        for j in range(n_k):
            args.append(k)
            in_specs.append(pl.BlockSpec((1, t_k, LANE), lambda b, g, i, j=j: (b, 0, k_map(g, j))))
        args.append(v)
        in_specs.append(pl.BlockSpec((1, t_k, LANE), lambda b, g, i: (b, 0, v_map(g))))
    for e in extras:
        args.append(e)
        in_specs.append(pl.BlockSpec(e.shape, lambda b, g, i: (0, 0)))
    ow = out_width // n_groups
    return pl.pallas_call(
        functools.partial(_dense_attn_kernel, mode=mode, n_q=n_q, n_k=n_k, rows=rows, tq=tq, tk=tk,
                          n_chunks=n_chunks, lambda_init=lambda_init),
        grid=(bsz, n_groups, t_q // tq),
        in_specs=in_specs,
        out_specs=pl.BlockSpec((1, tq, ow), lambda b, g, i: (b, i, g)),
        out_shape=jax.ShapeDtypeStruct((bsz, t_q, out_width), BF16),
        scratch_shapes=[
            pltpu.VMEM((n_rows * tq, LANE), BF16),
            pltpu.VMEM((n_rows * tq, LANE), F32),
            pltpu.VMEM((n_rows * tq, LANE), F32),
            pltpu.VMEM((n_rows * tq, LANE), F32),
        ],
        compiler_params=_cparams(3),
        name=name,
    )(*args)


def _swa_kernel(*refs, latent, tq, n_blocks):
    it = iter(refs)
    sink_ref = next(it)
    q_ref = next(it)
    kc_ref, vc_ref = next(it), next(it)
    if latent:
        kp_ref, k0_ref, kn_ref = next(it), next(it), next(it)
        vp_ref, v0_ref, vn_ref = next(it), next(it), next(it)
    o_ref = next(it)
    i = pl.program_id(1)
    per_g = D_HEADS // D_KV_HEADS
    n_pairs = per_g // 2
    lo = _lane_lo((tq, LANE))
    if latent:
        r = lax.broadcasted_iota(jnp.int32, (tq, tq), 0)
        cidx = lax.broadcasted_iota(jnp.int32, (tq, tq), 1)
        ok_prev = jnp.logical_and(cidx >= r, i > 0)
        ok_next = jnp.logical_and(cidx <= r, i < n_blocks - 1)
        ok_prev = jnp.concatenate([ok_prev] * n_pairs, axis=0)
        ok_next = jnp.concatenate([ok_next] * n_pairs, axis=0)
    for g in range(D_KV_HEADS):
        halves = []
        for e in range(2):
            slab = g ^ e
            sl = slice(slab * LANE, (slab + 1) * LANE)
            qrows, sinks = [], []
            for pi in range(n_pairs):
                ps = g * n_pairs + pi
                qv = q_ref[0, :, ps * LANE:(ps + 1) * LANE]
                qrows.append(jnp.where(lo if e == 0 else jnp.logical_not(lo), qv, jnp.zeros_like(qv)))
                sinks.append(jnp.full((tq, 1), sink_ref[g * per_g + 2 * pi + e], F32))
            qst = jnp.concatenate(qrows, axis=0)
            sink = jnp.concatenate(sinks, axis=0)
            s_list = [_dot_nt(qst, kc_ref[0, :, sl])]
            v_list = [vc_ref[0, :, sl]]
            if latent:
                s_list.append(jnp.where(ok_prev, _dot_nt(qst, kp_ref[0, :, sl]), NEG_BIG))
                s_list.append(_dot_nt(qst, k0_ref[0, :, sl]))
                s_list.append(jnp.where(ok_next, _dot_nt(qst, kn_ref[0, :, sl]), NEG_BIG))
                v_list += [vp_ref[0, :, sl], v0_ref[0, :, sl], vn_ref[0, :, sl]]
            s = jnp.concatenate(s_list, axis=1)
            m = jnp.maximum(jnp.max(s, axis=1, keepdims=True), sink)
            p = jnp.exp2(s - m)
            l = jnp.sum(p, axis=1, keepdims=True) + jnp.exp2(sink - m)
            vv = jnp.concatenate(v_list, axis=0)
            halves.append(_dot(p.astype(BF16), vv) * (1.0 / l))
        for pi in range(n_pairs):
            ps = g * n_pairs + pi
            out = jnp.where(lo, halves[0][pi * tq:(pi + 1) * tq], halves[1][pi * tq:(pi + 1) * tq])
            o_ref[0, :, ps * LANE:(ps + 1) * LANE] = out.astype(o_ref.dtype)


def _swa_call(q, kc, vc, k, v, sinks_log2, tq, name):
    bsz, t_q, width = q.shape
    t_c = kc.shape[1]
    latent = k is not None
    n_blocks = t_q // tq
    args = [sinks_log2, q, kc, vc]
    in_specs = [
        pl.BlockSpec(memory_space=pltpu.SMEM),
        pl.BlockSpec((1, tq, width), lambda b, i: (b, i, 0)),
        pl.BlockSpec((1, t_c, 2 * LANE), lambda b, i: (b, 0, 0)),
        pl.BlockSpec((1, t_c, 2 * LANE), lambda b, i: (b, 0, 0)),
    ]
    if latent:
        assert tq == WINDOW
        prev_map = lambda b, i: (b, jnp.maximum(i - 1, 0), 0)
        cur_map = lambda b, i: (b, i, 0)
        next_map = lambda b, i: (b, jnp.minimum(i + 1, n_blocks - 1), 0)
        for arr in (k, v):
            for mp in (prev_map, cur_map, next_map):
                args.append(arr)
                in_specs.append(pl.BlockSpec((1, tq, 2 * LANE), mp))
    return pl.pallas_call(
        functools.partial(_swa_kernel, latent=latent, tq=tq, n_blocks=n_blocks),
        grid=(bsz, n_blocks),
        in_specs=in_specs,
        out_specs=pl.BlockSpec((1, tq, width), lambda b, i: (b, i, 0)),
        out_shape=jax.ShapeDtypeStruct((bsz, t_q, width), BF16),
        compiler_params=_cparams(2),
        name=name,
    )(*args)


def _out_kernel(x_ref, o_ref, g_ref, mod_ref, w_ref, lng_ref, lnb_ref, y_ref, *, alpha):
    d = x_ref.shape[-1]
    u = (o_ref[0].astype(F32) * g_ref[0].astype(F32)).astype(BF16)
    br = _dot(u, w_ref[...])
    gate = mod_ref[0, :, 2 * d:3 * d]
    z = alpha * x_ref[0] + gate * br
    zc = z - jnp.mean(z, axis=-1, keepdims=True)
    var = jnp.mean(zc * zc, axis=-1, keepdims=True)
    y_ref[0] = zc * lax.rsqrt(var + NORM_EPS) * lng_ref[...] + lnb_ref[...]


def _out_call(x, o, g, mod, w, ln_g, ln_b, alpha, tm, name):
    bsz, t_len, d = x.shape
    width = o.shape[-1]
    return pl.pallas_call(
        functools.partial(_out_kernel, alpha=alpha),
        grid=(bsz, t_len // tm),
        in_specs=[
            pl.BlockSpec((1, tm, d), lambda b, i: (b, i, 0)),
            pl.BlockSpec((1, tm, width), lambda b, i: (b, i, 0)),
            pl.BlockSpec((1, tm, width), lambda b, i: (b, i, 0)),
            pl.BlockSpec((1, 1, mod.shape[-1]), lambda b, i: (b, 0, 0)),
            pl.BlockSpec(w.shape, lambda b, i: (0, 0)),
            pl.BlockSpec((1, d), lambda b, i: (0, 0)),
            pl.BlockSpec((1, d), lambda b, i: (0, 0)),
        ],
        out_specs=pl.BlockSpec((1, tm, d), lambda b, i: (b, i, 0)),
        out_shape=jax.ShapeDtypeStruct((bsz, t_len, d), F32),
        compiler_params=_cparams(2),
        name=name,
    )(x, o, g, mod, w, ln_g.reshape(1, d), ln_b.reshape(1, d))


def _tiles(t_len):
    tm = min(512, t_len)
    tq = min(512, t_len)
    tk = min(512, t_len)
    return tm, tq, tk


def _mla_layer(x, ctx, mod, mod_c, need_ctx, w_in, g_qa, w_qb, g_kva, w_kvb):
    t_len, t_c = x.shape[1], ctx.shape[1]
    tm, tq, tk = _tiles(t_len)
    d = w_in.shape[0]
    o0, o1, o2 = A_Q_LORA, A_Q_LORA + A_KV_LORA, A_Q_LORA + A_KV_LORA + A_ROPE
    kpe_pad = jnp.concatenate([jnp.zeros((d, A_NOPE), F32), w_in[:, o1:o2],
                               jnp.zeros((d, LANE - A_NOPE - A_ROPE), F32)], axis=1)
    w_in_r = jnp.concatenate([w_in[:, :o1], kpe_pad, w_in[:, o2:]], axis=1).astype(BF16)
    w_qb_r = jnp.pad(w_qb.reshape(A_Q_LORA, A_HEADS, A_NOPE + A_ROPE),
                     ((0, 0), (0, 0), (0, LANE - A_NOPE - A_ROPE))).reshape(A_Q_LORA, A_HEADS * LANE).astype(BF16)
    kvb = w_kvb.reshape(A_KV_LORA, A_HEADS, A_NOPE + A_V)
    k_pad = jnp.pad(kvb[:, :, :A_NOPE], ((0, 0), (0, 0), (0, LANE - A_NOPE))).reshape(A_KV_LORA, A_HEADS * LANE)
    w_kvb_r = jnp.concatenate([k_pad, kvb[:, :, A_NOPE:].reshape(A_KV_LORA, A_HEADS * A_V)], axis=1).astype(BF16)
    q_scale = (A_NOPE + A_ROPE) ** -0.5 * LOG2E
    gq, gk = g_qa.reshape(1, -1), g_kva.reshape(1, -1)
    tabs = _rope_tables(t_len, A_ROPE, "mla")
    q, k, v, g = _mla_proj_call(x, mod, tabs, w_in_r, gq, w_qb_r, gk, w_kvb_r, q_scale, tm, "mla_proj")
    q_c, k_c, v_c, g_c = _mla_proj_call(ctx, mod_c, None, w_in_r, gq, w_qb_r, gk, w_kvb_r, q_scale, t_c, "mla_proj_ctx")
    rows = ((0, None, 0), (1, None, 1))
    maps = dict(n_groups=A_HEADS // 2, q_map=lambda gi, j: 2 * gi + j, k_map=lambda gi, j: 2 * gi + j,
                v_map=lambda gi: gi, rows=rows, out_width=A_HEADS * A_V, lambda_init=0.0)
    o = _dense_attn_call("pair", q, k_c, v_c, k, v, (), tq=tq, tk=tk, name="mla_attn", **maps)
    o_c = _dense_attn_call("pair", q_c, k_c, v_c, None, None, (), tq=t_c, tk=tk, name="mla_attn_ctx", **maps) \
        if need_ctx else None
    return o, g, o_c, g_c


def _diff_layer(x, ctx, mod, mod_c, need_ctx, layer_idx, w_in, lam, g_sub):
    t_len, t_c = x.shape[1], ctx.shape[1]
    tm, tq, tk = _tiles(t_len)
    tq = min(tq, 256) if t_len >= 512 else tq
    n = 2 * B_HEADS * B_HEAD
    q_scale = B_HEAD ** -0.5 * LOG2E
    segs = ((0, 0, n, "rope", q_scale, None), (1, n, n, "rope", 1.0, None),
            (2, 2 * n, n, "plain", 1.0, None), (3, 3 * n, D_MODEL, "silu", 1.0, None))
    widths = (n, n, n, D_MODEL)
    wb = w_in.astype(BF16)
    tabs = _rope_tables(t_len, B_HEAD, "unit")
    q, k, v, g = _proj_call(x, mod, tabs, wb, None, segs, widths, B_HEAD // 2, tm, "diff_proj")
    q_c, k_c, v_c, g_c = _proj_call(ctx, mod_c, None, wb, None, segs, widths, B_HEAD // 2, t_c, "diff_proj_ctx")
    lambda_init = 0.8 - 0.6 * math.exp(-0.3 * layer_idx)
    rows = ((0, "lo", 0), (0, "hi", 0))
    extras = (lam.astype(F32), g_sub.reshape(1, -1).astype(F32))
    maps = dict(n_groups=B_HEADS, q_map=lambda gi, j: gi, k_map=lambda gi, j: gi, v_map=lambda gi: gi,
                rows=rows, out_width=n, lambda_init=lambda_init)
    o = _dense_attn_call("diff", q, k_c, v_c, k, v, extras, tq=tq, tk=tk, name="diff_attn", **maps)
    o_c = _dense_attn_call("diff", q_c, k_c, v_c, None, None, extras, tq=t_c, tk=tk, name="diff_attn_ctx", **maps) \
        if need_ctx else None
    return o, g, o_c, g_c


def _gqa_layer(x, ctx, mod, mod_c, need_ctx, w_in, g_q, g_k):
    t_len, t_c = x.shape[1], ctx.shape[1]
    tm, tq, tk = _tiles(t_len)
    tq = min(tq, 256) if t_len >= 512 else tq
    nq, nkv = C_HEADS * C_HEAD, C_KV_HEADS * C_HEAD
    q_scale = C_HEAD ** -0.5 * LOG2E
    segs = ((0, 0, nq, "rope", q_scale, 0), (1, nq, nkv, "rope", 1.0, 1),
            (2, nq + nkv, nkv, "plain", 1.0, None), (3, nq + 2 * nkv, D_MODEL, "silu", 1.0, None))
    widths = (nq, nkv, nkv, D_MODEL)
    wb = w_in.astype(BF16)
    gains = jnp.stack([g_q, g_k]).astype(F32)
    tabs = _rope_tables(t_len, C_HEAD, "unit")
    q, k, v, g = _proj_call(x, mod, tabs, wb, gains, segs, widths, C_HEAD // 2, tm, "gqa_proj")
    q_c, k_c, v_c, g_c = _proj_call(ctx, mod_c, None, wb, gains, segs, widths, C_HEAD // 2, t_c, "gqa_proj_ctx")
    per = C_HEADS // C_KV_HEADS
    rows = tuple((j, None, 0) for j in range(per))
    maps = dict(n_groups=C_KV_HEADS, q_map=lambda gi, j: per * gi + j, k_map=lambda gi, j: gi, v_map=lambda gi: gi,
                rows=rows, out_width=nq, lambda_init=0.0)
    o = _dense_attn_call("stack", q, k_c, v_c, k, v, (), tq=tq, tk=tk, name="gqa_attn", **maps)
    o_c = _dense_attn_call("stack", q_c, k_c, v_c, None, None, (), tq=t_c, tk=tk, name="gqa_attn_ctx", **maps) \
        if need_ctx else None
    return o, g, o_c, g_c


def _swa_layer(x, ctx, mod, mod_c, need_ctx, w_in, sinks):
    t_len, t_c = x.shape[1], ctx.shape[1]
    tm, _, _ = _tiles(t_len)
    nq, nkv = D_HEADS * D_HEAD, D_KV_HEADS * D_HEAD
    wk = w_in[:, nq:nq + nkv]
    wv = w_in[:, nq + nkv:nq + 2 * nkv]
    swap = lambda w: jnp.concatenate([w, w[:, D_HEAD:], w[:, :D_HEAD]], axis=1)
    wb = jnp.concatenate([w_in[:, :nq], swap(wk), swap(wv), w_in[:, nq + 2 * nkv:]], axis=1).astype(BF16)
    q_scale = D_HEAD ** -0.5 * LOG2E
    segs = ((0, 0, nq, "rope", q_scale, None), (1, nq, 2 * nkv, "rope", 1.0, None),
            (2, nq + 2 * nkv, 2 * nkv, "plain", 1.0, None), (3, nq + 4 * nkv, D_MODEL, "silu", 1.0, None))
    widths = (nq, 2 * nkv, 2 * nkv, D_MODEL)
    tabs = _rope_tables(t_len, D_HEAD, "unit")
    q, k, v, g = _proj_call(x, mod, tabs, wb, None, segs, widths, D_HEAD // 2, tm, "swa_proj")
    q_c, k_c, v_c, g_c = _proj_call(ctx, mod_c, None, wb, None, segs, widths, D_HEAD // 2, t_c, "swa_proj_ctx")
    sinks_log2 = sinks.astype(F32) * LOG2E
    o = _swa_call(q, k_c, v_c, k, v, sinks_log2, WINDOW, "swa_attn")
    o_c = _swa_call(q_c, k_c, v_c, None, None, sinks_log2, t_c, "swa_attn_ctx") if need_ctx else None
    return o, g, o_c, g_c


def kernel(x, c, ctx, c_ctx, ada_w, ada_b, out_w, ln_g, ln_b, mla_w_in, mla_g_qa, mla_w_qb, mla_g_kva, mla_w_kvb,
           diff_w_in, diff_lambda, diff_g_sub, gqa_w_in, gqa_g_q, gqa_g_k, swa_w_in, swa_sink):
    depth = ada_w.shape[0]
    bsz, t_len, d = x.shape
    t_c = ctx.shape[1]
    alpha = (2 * depth) ** 0.25
    n_rows = 8
    assert bsz + 1 <= n_rows
    cvec = jnp.concatenate([c, c_ctx[None, :], jnp.zeros((n_rows - bsz - 1, d), F32)], axis=0)
    mods = _adaln_call(cvec, ada_w, ada_b)
    tm = _tiles(t_len)[0]
    for i in range(depth):
        kind, j = i % 4, i // 4
        need_ctx = i < depth - 1
        mod = mods[i, 0:bsz][:, None, :]
        mod_c = jnp.broadcast_to(mods[i, bsz][None, None, :], (bsz, 1, 3 * d))
        if kind == 0:
            o, g, o_c, g_c = _mla_layer(x, ctx, mod, mod_c, need_ctx, mla_w_in[j], mla_g_qa[j], mla_w_qb[j],
                                        mla_g_kva[j], mla_w_kvb[j])
        elif kind == 1:
            o, g, o_c, g_c = _diff_layer(x, ctx, mod, mod_c, need_ctx, i, diff_w_in[j], diff_lambda[j], diff_g_sub[j])
        elif kind == 2:
            o, g, o_c, g_c = _gqa_layer(x, ctx, mod, mod_c, need_ctx, gqa_w_in[j], gqa_g_q[j], gqa_g_k[j])
        else:
            o, g, o_c, g_c = _swa_layer(x, ctx, mod, mod_c, need_ctx, swa_w_in[j], swa_sink[j])
        wo = out_w[i].astype(BF16)
        x = _out_call(x, o, g, mod, wo, ln_g[i], ln_b[i], alpha, tm, "out_proj")
        if need_ctx:
            ctx = _out_call(ctx, o_c, g_c, mod_c, wo, ln_g[i], ln_b[i], alpha, t_c, "out_proj_ctx")
    return x
```

```python
import functools
import math

import jax
import jax.numpy as jnp
from jax import lax
from jax.experimental import pallas as pl
from jax.experimental.pallas import tpu as pltpu

D_MODEL = 1024
GRID_W = 64
WINDOW = 128
ROPE_THETA = 10000.0
NORM_EPS = 1e-6
A_HEADS, A_Q_LORA, A_KV_LORA, A_NOPE, A_ROPE, A_V = 16, 256, 128, 64, 32, 64
B_HEADS, B_HEAD = 8, 64
C_HEADS, C_KV_HEADS, C_HEAD = 8, 2, 128
D_HEADS, D_KV_HEADS, D_HEAD = 16, 2, 64

LANE = 128
HALF = LANE // 2
LOG2E = math.log2(math.e)
NEG_BIG = -1e30
VMEM_LIMIT = 56 * 1024 * 1024
BF16 = jnp.bfloat16
F32 = jnp.float32


def _cparams(n_grid):
    return pltpu.CompilerParams(dimension_semantics=("arbitrary",) * n_grid, vmem_limit_bytes=VMEM_LIMIT)


def _silu(x):
    return x / (1.0 + jnp.exp(-x))


def _dot(a, b):
    return jnp.dot(a, b, preferred_element_type=F32)


def _dot_nt(a, b):
    return lax.dot_general(a, b, (((1,), (1,)), ((), ())), preferred_element_type=F32)


def _adaln_kernel(c_ref, w_ref, b_ref, o_ref):
    sc = _silu(c_ref[...])
    o_ref[0] = jnp.dot(sc, w_ref[0], preferred_element_type=F32, precision=lax.Precision.HIGHEST) + b_ref[0]


def _adaln_call(cvec, ada_w, ada_b):
    depth, d, n = ada_w.shape
    rows = cvec.shape[0]
    tn = 1024
    return pl.pallas_call(
        _adaln_kernel,
        grid=(depth, n // tn),
        in_specs=[
            pl.BlockSpec((rows, d), lambda i, j: (0, 0)),
            pl.BlockSpec((1, d, tn), lambda i, j: (i, 0, j)),
            pl.BlockSpec((1, 1, tn), lambda i, j: (i, 0, j)),
        ],
        out_specs=pl.BlockSpec((1, rows, tn), lambda i, j: (i, 0, j)),
        out_shape=jax.ShapeDtypeStruct((depth, rows, n), F32),
        compiler_params=_cparams(2),
        name="adaln",
    )(cvec, ada_w, ada_b.reshape(depth, 1, n))


def _rope_tables(t_len, rot_dim, layout):
    rows = t_len // GRID_W
    row = jnp.repeat(jnp.arange(rows, dtype=F32), GRID_W)
    col = jnp.tile(jnp.arange(GRID_W, dtype=F32), rows)
    n_freq = rot_dim // 4
    inv_freq = ROPE_THETA ** (-jnp.arange(n_freq, dtype=F32) / n_freq)
    ang = jnp.concatenate([row[:, None] * inv_freq, col[:, None] * inv_freq], axis=-1)
    cos, sin = jnp.cos(ang), jnp.sin(ang)
    zero = jnp.zeros_like(sin)
    if layout == "mla":
        one = jnp.ones((t_len, A_NOPE), F32)
        pad = jnp.zeros((t_len, LANE - A_NOPE - A_ROPE), F32)
        c = jnp.concatenate([one, cos, cos, pad], axis=-1)
        s_lo = jnp.concatenate([0 * one, -sin, zero, pad], axis=-1)
        s_hi = jnp.concatenate([0 * one, zero, sin, pad], axis=-1)
    else:
        reps = LANE // rot_dim
        c = jnp.tile(jnp.concatenate([cos, cos], axis=-1), (1, reps))
        s_lo = jnp.tile(jnp.concatenate([-sin, zero], axis=-1), (1, reps))
        s_hi = jnp.tile(jnp.concatenate([zero, sin], axis=-1), (1, reps))
    return c, s_lo, s_hi


def _rope_slab(x, c, s_lo, s_hi, half):
    if half == HALF:
        return x * c + pltpu.roll(x, HALF, 1) * (s_lo + s_hi)
    return x * c + pltpu.roll(x, LANE - half, 1) * s_lo + pltpu.roll(x, half, 1) * s_hi


def _modulate(x_ref, mod_ref):
    d = x_ref.shape[-1]
    shift = mod_ref[0, :, 0:d]
    scale = mod_ref[0, :, d:2 * d]
    return (x_ref[0] * (1.0 + scale) + shift).astype(BF16)


def _rms(x, g):
    return x * lax.rsqrt(jnp.mean(x * x, axis=-1, keepdims=True) + NORM_EPS) * g


PROJ_CHUNK = 512


def _proj_kernel(*refs, segs, rope_half, has_rope, has_norm):
    it = iter(refs)
    x_ref, mod_ref = next(it), next(it)
    if has_rope:
        c = next(it)[...]
        s_lo = next(it)[...]
        s_hi = next(it)[...]
    w_ref = next(it)
    gn_ref = next(it) if has_norm else None
    out_refs = list(it)
    hb = _modulate(x_ref, mod_ref)
    for (oi, col0, width, kind, q_scale, norm_row) in segs:
        for cc in range(0, width, PROJ_CHUNK):
            cw = min(PROJ_CHUNK, width - cc)
            t = _dot(hb, w_ref[:, col0 + cc:col0 + cc + cw])
            if kind == "silu":
                y = _silu(t)
            elif kind == "plain":
                y = t
            else:
                slabs = []
                for s in range(cw // LANE):
                    xs = t[:, s * LANE:(s + 1) * LANE]
                    if norm_row is not None:
                        xs = _rms(xs, gn_ref[norm_row:norm_row + 1, :])
                    if has_rope:
                        xs = _rope_slab(xs, c, s_lo, s_hi, rope_half)
                    if q_scale != 1.0:
                        xs = xs * q_scale
                    slabs.append(xs)
                y = jnp.concatenate(slabs, axis=1) if len(slabs) > 1 else slabs[0]
            out_refs[oi][0, :, cc:cc + cw] = y.astype(BF16)


def _proj_call(x, mod, tabs, w, gains, segs, out_widths, rope_half, tm, name):
    bsz, t_len, d = x.shape
    has_rope = tabs is not None
    has_norm = gains is not None
    args = [x, mod]
    in_specs = [
        pl.BlockSpec((1, tm, d), lambda b, i: (b, i, 0)),
        pl.BlockSpec((1, 1, mod.shape[-1]), lambda b, i: (b, 0, 0)),
    ]
    if has_rope:
        for tab in tabs:
            args.append(tab)
            in_specs.append(pl.BlockSpec((tm, LANE), lambda b, i: (i, 0)))
    args.append(w)
    in_specs.append(pl.BlockSpec(w.shape, lambda b, i: (0, 0)))
    if has_norm:
        args.append(gains)
        in_specs.append(pl.BlockSpec(gains.shape, lambda b, i: (0, 0)))
    return pl.pallas_call(
        functools.partial(_proj_kernel, segs=segs, rope_half=rope_half, has_rope=has_rope, has_norm=has_norm),
        grid=(bsz, t_len // tm),
        in_specs=in_specs,
        out_specs=[pl.BlockSpec((1, tm, ow), lambda b, i: (b, i, 0)) for ow in out_widths],
        out_shape=[jax.ShapeDtypeStruct((bsz, t_len, ow), BF16) for ow in out_widths],
        compiler_params=_cparams(2),
        name=name,
    )(*args)


def _mla_proj_kernel(*refs, has_rope, q_scale):
    it = iter(refs)
    x_ref, mod_ref = next(it), next(it)
    if has_rope:
        c = next(it)[...]
        s_lo = next(it)[...]
        s_hi = next(it)[...]
    w_in_ref, gqa_ref, w_qb_ref, gkva_ref, w_kvb_ref = next(it), next(it), next(it), next(it), next(it)
    q_ref, k_ref, v_ref, g_ref = next(it), next(it), next(it), next(it)
    half = A_ROPE // 2
    n_lat = A_Q_LORA + A_KV_LORA + LANE
    hb = _modulate(x_ref, mod_ref)
    t = _dot(hb, w_in_ref[:, 0:n_lat])
    qn = _rms(t[:, 0:A_Q_LORA], gqa_ref[...]).astype(BF16)
    kn = _rms(t[:, A_Q_LORA:A_Q_LORA + A_KV_LORA], gkva_ref[...]).astype(BF16)
    kpe = t[:, A_Q_LORA + A_KV_LORA:n_lat]
    if has_rope:
        kpe = _rope_slab(kpe, c, s_lo, s_hi, half)
    n_qk = A_HEADS * LANE
    for cc in range(0, n_qk, PROJ_CHUNK):
        tq = _dot(qn, w_qb_ref[:, cc:cc + PROJ_CHUNK])
        tk = _dot(kn, w_kvb_ref[:, cc:cc + PROJ_CHUNK])
        qs, ks = [], []
        for s in range(PROJ_CHUNK // LANE):
            xq = tq[:, s * LANE:(s + 1) * LANE]
            if has_rope:
                xq = _rope_slab(xq, c, s_lo, s_hi, half)
            qs.append(xq * q_scale)
            ks.append(tk[:, s * LANE:(s + 1) * LANE] + kpe)
        q_ref[0, :, cc:cc + PROJ_CHUNK] = jnp.concatenate(qs, axis=1).astype(BF16)
        k_ref[0, :, cc:cc + PROJ_CHUNK] = jnp.concatenate(ks, axis=1).astype(BF16)
    n_v = A_HEADS * A_V
    for cc in range(0, n_v, PROJ_CHUNK):
        v_ref[0, :, cc:cc + PROJ_CHUNK] = _dot(kn, w_kvb_ref[:, n_qk + cc:n_qk + cc + PROJ_CHUNK]).astype(BF16)
    for cc in range(0, D_MODEL, PROJ_CHUNK):
        g_ref[0, :, cc:cc + PROJ_CHUNK] = _silu(_dot(hb, w_in_ref[:, n_lat + cc:n_lat + cc + PROJ_CHUNK])).astype(BF16)


def _mla_proj_call(x, mod, tabs, w_in, g_qa, w_qb, g_kva, w_kvb, q_scale, tm, name):
    bsz, t_len, d = x.shape
    has_rope = tabs is not None
    args = [x, mod]
    in_specs = [
        pl.BlockSpec((1, tm, d), lambda b, i: (b, i, 0)),
        pl.BlockSpec((1, 1, mod.shape[-1]), lambda b, i: (b, 0, 0)),
    ]
    if has_rope:
        for tab in tabs:
            args.append(tab)
            in_specs.append(pl.BlockSpec((tm, LANE), lambda b, i: (i, 0)))
    for a in (w_in, g_qa, w_qb, g_kva, w_kvb):
        args.append(a)
        in_specs.append(pl.BlockSpec(a.shape, lambda b, i: (0, 0)))
    out_widths = (A_HEADS * LANE, A_HEADS * LANE, A_HEADS * A_V, D_MODEL)
    return pl.pallas_call(
        functools.partial(_mla_proj_kernel, has_rope=has_rope, q_scale=q_scale),
        grid=(bsz, t_len // tm),
        in_specs=in_specs,
        out_specs=[pl.BlockSpec((1, tm, ow), lambda b, i: (b, i, 0)) for ow in out_widths],
        out_shape=[jax.ShapeDtypeStruct((bsz, t_len, ow), BF16) for ow in out_widths],
        compiler_params=_cparams(2),
        name=name,
    )(*args)


def _lane_lo(shape):
    return lax.broadcasted_iota(jnp.int32, shape, len(shape) - 1) < HALF


def _dense_attn_kernel(*refs, mode, n_q, n_k, rows, tq, tk, n_chunks, lambda_init):
    it = iter(refs)
    q_refs = [next(it) for _ in range(n_q)]
    kc_refs = [next(it) for _ in range(n_k)]
    vc_ref = next(it)
    if n_chunks:
        k_refs = [next(it) for _ in range(n_k)]
        v_ref = next(it)
    if mode == "diff":
        lam_ref, gsub_ref = next(it), next(it)
    o_ref = next(it)
    qs_scr, m_scr, l_scr, acc_scr = next(it), next(it), next(it), next(it)

    n_rows = len(rows)
    for r, (qi, mask, _) in enumerate(rows):
        q = q_refs[qi][0]
        if mask is not None:
            lo = _lane_lo(q.shape)
            q = jnp.where(lo if mask == "lo" else jnp.logical_not(lo), q, jnp.zeros_like(q))
        qs_scr[r * tq:(r + 1) * tq, :] = q
    spans = []
    for ks in range(n_k):
        idx = [r for r, row in enumerate(rows) if row[2] == ks]
        spans.append((idx[0] * tq, (idx[-1] + 1) * tq))

    def process(k_list, v, first):
        width = k_list[0].shape[0]
        p_parts, alphas = [], []
        for ks, (r0, r1) in enumerate(spans):
            s = _dot_nt(qs_scr[r0:r1, :], k_list[ks])
            cols = [s[:, j * LANE:(j + 1) * LANE] for j in range(width // LANE)]
            mp = cols[0]
            for cj in cols[1:]:
                mp = jnp.maximum(mp, cj)
            smax = jnp.max(mp, axis=1, keepdims=True)
            if first:
                m_new = jnp.broadcast_to(smax, (r1 - r0, LANE))
            else:
                m_prev = m_scr[r0:r1, :]
                m_new = jnp.maximum(m_prev, smax)
            lsum = None
            ps = []
            for cj in cols:
                pj = jnp.exp2(cj - m_new)
                lsum = pj if lsum is None else lsum + pj
                ps.append(pj.astype(BF16))
            p_parts.append(jnp.concatenate(ps, axis=1) if len(ps) > 1 else ps[0])
            if first:
                l_scr[r0:r1, :] = lsum
            else:
                alpha = jnp.exp2(m_prev - m_new)
                l_scr[r0:r1, :] = alpha * l_scr[r0:r1, :] + lsum
                alphas.append(alpha)
            m_scr[r0:r1, :] = m_new
        p = jnp.concatenate(p_parts, axis=0) if len(p_parts) > 1 else p_parts[0]
        pv = _dot(p, v)
        if first:
            acc_scr[...] = pv
        else:
            alpha = jnp.concatenate(alphas, axis=0) if len(alphas) > 1 else alphas[0]
            acc_scr[...] = alpha * acc_scr[...] + pv

    process([r[0] for r in kc_refs], vc_ref[0], True)

    if n_chunks:
        def step(ci, carry):
            off = pl.multiple_of(ci * tk, tk)
            process([r[0, pl.ds(off, tk), :] for r in k_refs], v_ref[0, pl.ds(off, tk), :], False)
            return carry
        lax.fori_loop(0, n_chunks, step, 0)

    l = jnp.sum(l_scr[...], axis=1, keepdims=True)
    o = acc_scr[...] * (1.0 / l)
    if mode == "pair":
        out = jnp.where(_lane_lo((tq, LANE)), o[0:tq], o[tq:2 * tq])
    elif mode == "diff":
        lam = lam_ref[...]
        a = jnp.sum(lam[0:1, :] * lam[1:2, :], axis=1, keepdims=True)
        b = jnp.sum(lam[2:3, :] * lam[3:4, :], axis=1, keepdims=True)
        lam_full = jnp.exp(a) - jnp.exp(b) + lambda_init
        dlt = o[0:tq] - lam_full * o[tq:2 * tq]
        out = _rms(dlt, gsub_ref[...]) * (1.0 - lambda_init)
    else:
        out = jnp.concatenate([o[r * tq:(r + 1) * tq] for r in range(n_rows)], axis=1)
    o_ref[0] = out.astype(o_ref.dtype)


def _dense_attn_call(mode, q, kc, vc, k, v, extras, n_groups, q_map, k_map, v_map, rows, out_width, tq, tk,
                     lambda_init, name):
    bsz, t_q, _ = q.shape
    t_c = kc.shape[1]
    n_q = 1 + max(r[0] for r in rows)
    n_k = 1 + max(r[2] for r in rows)
    n_rows = len(rows)
    has_latent = k is not None
    n_chunks = (k.shape[1] // tk) if has_latent else 0
    args, in_specs = [], []
    for j in range(n_q):
        args.append(q)
        in_specs.append(pl.BlockSpec((1, tq, LANE), lambda b, g, i, j=j: (b, i, q_map(g, j))))
    for j in range(n_k):
        args.append(kc)
        in_specs.append(pl.BlockSpec((1, t_c, LANE), lambda b, g, i, j=j: (b, 0, k_map(g, j))))
    args.append(vc)
    in_specs.append(pl.BlockSpec((1, t_c, LANE), lambda b, g, i: (b, 0, v_map(g))))
    if has_latent:
        t_k = k.shape[1]
        for j in range(n_k):
            args.append(k)
            in_specs.append(pl.BlockSpec((1, t_k, LANE), lambda b, g, i, j=j: (b, 0, k_map(g, j))))
        args.append(v)
        in_specs.append(pl.BlockSpec((1, t_k, LANE), lambda b, g, i: (b, 0, v_map(g))))
    for e in extras:
        args.append(e)
        in_specs.append(pl.BlockSpec(e.shape, lambda b, g, i: (0, 0)))
    ow = out_width // n_groups
    return pl.pallas_call(
        functools.partial(_dense_attn_kernel, mode=mode, n_q=n_q, n_k=n_k, rows=rows, tq=tq, tk=tk,
                          n_chunks=n_chunks, lambda_init=lambda_init),
        grid=(bsz, n_groups, t_q // tq),
        in_specs=in_specs,
        out_specs=pl.BlockSpec((1, tq, ow), lambda b, g, i: (b, i, g)),
        out_shape=jax.ShapeDtypeStruct((bsz, t_q, out_width), BF16),
        scratch_shapes=[
            pltpu.VMEM((n_rows * tq, LANE), BF16),
            pltpu.VMEM((n_rows * tq, LANE), F32),
            pltpu.VMEM((n_rows * tq, LANE), F32),
            pltpu.VMEM((n_rows * tq, LANE), F32),
        ],
        compiler_params=_cparams(3),
        name=name,
    )(*args)


def _swa_kernel(*refs, latent, tq, n_blocks):
    it = iter(refs)
    sink_ref = next(it)
    q_ref = next(it)
    kc_ref, vc_ref = next(it), next(it)
    if latent:
        kp_ref, k0_ref, kn_ref = next(it), next(it), next(it)
        vp_ref, v0_ref, vn_ref = next(it), next(it), next(it)
    o_ref = next(it)
    i = pl.program_id(1)
    per_g = D_HEADS // D_KV_HEADS
    n_pairs = per_g // 2
    lo = _lane_lo((tq, LANE))
    if latent:
        r = lax.broadcasted_iota(jnp.int32, (tq, tq), 0)
        cidx = lax.broadcasted_iota(jnp.int32, (tq, tq), 1)
        ok_prev = jnp.logical_and(cidx >= r, i > 0)
        ok_next = jnp.logical_and(cidx <= r, i < n_blocks - 1)
        ok_prev = jnp.concatenate([ok_prev] * n_pairs, axis=0)
        ok_next = jnp.concatenate([ok_next] * n_pairs, axis=0)
    for g in range(D_KV_HEADS):
        halves = []
        for e in range(2):
            slab = g ^ e
            sl = slice(slab * LANE, (slab + 1) * LANE)
            qrows, sinks = [], []
            for pi in range(n_pairs):
                ps = g * n_pairs + pi
                qv = q_ref[0, :, ps * LANE:(ps + 1) * LANE]
                qrows.append(jnp.where(lo if e == 0 else jnp.logical_not(lo), qv, jnp.zeros_like(qv)))
                sinks.append(jnp.full((tq, 1), sink_ref[g * per_g + 2 * pi + e], F32))
            qst = jnp.concatenate(qrows, axis=0)
            sink = jnp.concatenate(sinks, axis=0)
            s_list = [_dot_nt(qst, kc_ref[0, :, sl])]
            v_list = [vc_ref[0, :, sl]]
            if latent:
                s_list.append(jnp.where(ok_prev, _dot_nt(qst, kp_ref[0, :, sl]), NEG_BIG))
                s_list.append(_dot_nt(qst, k0_ref[0, :, sl]))
                s_list.append(jnp.where(ok_next, _dot_nt(qst, kn_ref[0, :, sl]), NEG_BIG))
                v_list += [vp_ref[0, :, sl], v0_ref[0, :, sl], vn_ref[0, :, sl]]
            s = jnp.concatenate(s_list, axis=1)
            m = jnp.maximum(jnp.max(s, axis=1, keepdims=True), sink)
            p = jnp.exp2(s - m)
            l = jnp.sum(p, axis=1, keepdims=True) + jnp.exp2(sink - m)
            vv = jnp.concatenate(v_list, axis=0)
            halves.append(_dot(p.astype(BF16), vv) * (1.0 / l))
        for pi in range(n_pairs):
            ps = g * n_pairs + pi
            out = jnp.where(lo, halves[0][pi * tq:(pi + 1) * tq], halves[1][pi * tq:(pi + 1) * tq])
            o_ref[0, :, ps * LANE:(ps + 1) * LANE] = out.astype(o_ref.dtype)


def _swa_call(q, kc, vc, k, v, sinks_log2, tq, name):
    bsz, t_q, width = q.shape
    t_c = kc.shape[1]
    latent = k is not None
    n_blocks = t_q // tq
    args = [sinks_log2, q, kc, vc]
    in_specs = [
        pl.BlockSpec(memory_space=pltpu.SMEM),
        pl.BlockSpec((1, tq, width), lambda b, i: (b, i, 0)),
        pl.BlockSpec((1, t_c, 2 * LANE), lambda b, i: (b, 0, 0)),
        pl.BlockSpec((1, t_c, 2 * LANE), lambda b, i: (b, 0, 0)),
    ]
    if latent:
        assert tq == WINDOW
        prev_map = lambda b, i: (b, jnp.maximum(i - 1, 0), 0)
        cur_map = lambda b, i: (b, i, 0)
        next_map = lambda b, i: (b, jnp.minimum(i + 1, n_blocks - 1), 0)
        for arr in (k, v):
            for mp in (prev_map, cur_map, next_map):
                args.append(arr)
                in_specs.append(pl.BlockSpec((1, tq, 2 * LANE), mp))
    return pl.pallas_call(
        functools.partial(_swa_kernel, latent=latent, tq=tq, n_blocks=n_blocks),
        grid=(bsz, n_blocks),
        in_specs=in_specs,
        out_specs=pl.BlockSpec((1, tq, width), lambda b, i: (b, i, 0)),
        out_shape=jax.ShapeDtypeStruct((bsz, t_q, width), BF16),
        compiler_params=_cparams(2),
        name=name,
    )(*args)


def _out_kernel(x_ref, o_ref, g_ref, mod_ref, w_ref, lng_ref, lnb_ref, y_ref, *, alpha):
    d = x_ref.shape[-1]
    u = (o_ref[0].astype(F32) * g_ref[0].astype(F32)).astype(BF16)
    br = _dot(u, w_ref[...])
    gate = mod_ref[0, :, 2 * d:3 * d]
    z = alpha * x_ref[0] + gate * br
    zc = z - jnp.mean(z, axis=-1, keepdims=True)
    var = jnp.mean(zc * zc, axis=-1, keepdims=True)
    y_ref[0] = zc * lax.rsqrt(var + NORM_EPS) * lng_ref[...] + lnb_ref[...]


def _out_call(x, o, g, mod, w, ln_g, ln_b, alpha, tm, name):
    bsz, t_len, d = x.shape
    width = o.shape[-1]
    return pl.pallas_call(
        functools.partial(_out_kernel, alpha=alpha),
        grid=(bsz, t_len // tm),
        in_specs=[
            pl.BlockSpec((1, tm, d), lambda b, i: (b, i, 0)),
            pl.BlockSpec((1, tm, width), lambda b, i: (b, i, 0)),
            pl.BlockSpec((1, tm, width), lambda b, i: (b, i, 0)),
            pl.BlockSpec((1, 1, mod.shape[-1]), lambda b, i: (b, 0, 0)),
            pl.BlockSpec(w.shape, lambda b, i: (0, 0)),
            pl.BlockSpec((1, d), lambda b, i: (0, 0)),
            pl.BlockSpec((1, d), lambda b, i: (0, 0)),
        ],
        out_specs=pl.BlockSpec((1, tm, d), lambda b, i: (b, i, 0)),
        out_shape=jax.ShapeDtypeStruct((bsz, t_len, d), F32),
        compiler_params=_cparams(2),
        name=name,
    )(x, o, g, mod, w, ln_g.reshape(1, d), ln_b.reshape(1, d))


ATTN_ROWS = 1024
ATTN_TK = 1024


def _tiles(t_len, n_row_groups=1):
    tm = min(512, t_len)
    tq = min(ATTN_ROWS // n_row_groups, t_len)
    tk = min(ATTN_TK, t_len)
    return tm, tq, tk


def _mla_layer(x, ctx, mod, mod_c, need_ctx, w_in, g_qa, w_qb, g_kva, w_kvb):
    t_len, t_c = x.shape[1], ctx.shape[1]
    tm, tq, tk = _tiles(t_len, 2)
    d = w_in.shape[0]
    o0, o1, o2 = A_Q_LORA, A_Q_LORA + A_KV_LORA, A_Q_LORA + A_KV_LORA + A_ROPE
    kpe_pad = jnp.concatenate([jnp.zeros((d, A_NOPE), F32), w_in[:, o1:o2],
                               jnp.zeros((d, LANE - A_NOPE - A_ROPE), F32)], axis=1)
    w_in_r = jnp.concatenate([w_in[:, :o1], kpe_pad, w_in[:, o2:]], axis=1).astype(BF16)
    w_qb_r = jnp.pad(w_qb.reshape(A_Q_LORA, A_HEADS, A_NOPE + A_ROPE),
                     ((0, 0), (0, 0), (0, LANE - A_NOPE - A_ROPE))).reshape(A_Q_LORA, A_HEADS * LANE).astype(BF16)
    kvb = w_kvb.reshape(A_KV_LORA, A_HEADS, A_NOPE + A_V)
    k_pad = jnp.pad(kvb[:, :, :A_NOPE], ((0, 0), (0, 0), (0, LANE - A_NOPE))).reshape(A_KV_LORA, A_HEADS * LANE)
    w_kvb_r = jnp.concatenate([k_pad, kvb[:, :, A_NOPE:].reshape(A_KV_LORA, A_HEADS * A_V)], axis=1).astype(BF16)
    q_scale = (A_NOPE + A_ROPE) ** -0.5 * LOG2E
    gq, gk = g_qa.reshape(1, -1), g_kva.reshape(1, -1)
    tabs = _rope_tables(t_len, A_ROPE, "mla")
    q, k, v, g = _mla_proj_call(x, mod, tabs, w_in_r, gq, w_qb_r, gk, w_kvb_r, q_scale, tm, "mla_proj")
    q_c, k_c, v_c, g_c = _mla_proj_call(ctx, mod_c, None, w_in_r, gq, w_qb_r, gk, w_kvb_r, q_scale, t_c, "mla_proj_ctx")
    rows = ((0, None, 0), (1, None, 1))
    maps = dict(n_groups=A_HEADS // 2, q_map=lambda gi, j: 2 * gi + j, k_map=lambda gi, j: 2 * gi + j,
                v_map=lambda gi: gi, rows=rows, out_width=A_HEADS * A_V, lambda_init=0.0)
    o = _dense_attn_call("pair", q, k_c, v_c, k, v, (), tq=tq, tk=tk, name="mla_attn", **maps)
    o_c = _dense_attn_call("pair", q_c, k_c, v_c, None, None, (), tq=t_c, tk=tk, name="mla_attn_ctx", **maps) \
        if need_ctx else None
    return o, g, o_c, g_c


def _diff_layer(x, ctx, mod, mod_c, need_ctx, layer_idx, w_in, lam, g_sub):
    t_len, t_c = x.shape[1], ctx.shape[1]
    tm, tq, tk = _tiles(t_len, 2)
    n = 2 * B_HEADS * B_HEAD
    q_scale = B_HEAD ** -0.5 * LOG2E
    segs = ((0, 0, n, "rope", q_scale, None), (1, n, n, "rope", 1.0, None),
            (2, 2 * n, n, "plain", 1.0, None), (3, 3 * n, D_MODEL, "silu", 1.0, None))
    widths = (n, n, n, D_MODEL)
    wb = w_in.astype(BF16)
    tabs = _rope_tables(t_len, B_HEAD, "unit")
    q, k, v, g = _proj_call(x, mod, tabs, wb, None, segs, widths, B_HEAD // 2, tm, "diff_proj")
    q_c, k_c, v_c, g_c = _proj_call(ctx, mod_c, None, wb, None, segs, widths, B_HEAD // 2, t_c, "diff_proj_ctx")
    lambda_init = 0.8 - 0.6 * math.exp(-0.3 * layer_idx)
    rows = ((0, "lo", 0), (0, "hi", 0))
    extras = (lam.astype(F32), g_sub.reshape(1, -1).astype(F32))
    maps = dict(n_groups=B_HEADS, q_map=lambda gi, j: gi, k_map=lambda gi, j: gi, v_map=lambda gi: gi,
                rows=rows, out_width=n, lambda_init=lambda_init)
    o = _dense_attn_call("diff", q, k_c, v_c, k, v, extras, tq=tq, tk=tk, name="diff_attn", **maps)
    o_c = _dense_attn_call("diff", q_c, k_c, v_c, None, None, extras, tq=t_c, tk=tk, name="diff_attn_ctx", **maps) \
        if need_ctx else None
    return o, g, o_c, g_c


def _gqa_layer(x, ctx, mod, mod_c, need_ctx, w_in, g_q, g_k):
    t_len, t_c = x.shape[1], ctx.shape[1]
    tm, tq, tk = _tiles(t_len, C_HEADS // C_KV_HEADS)
    nq, nkv = C_HEADS * C_HEAD, C_KV_HEADS * C_HEAD
    q_scale = C_HEAD ** -0.5 * LOG2E
    segs = ((0, 0, nq, "rope", q_scale, 0), (1, nq, nkv, "rope", 1.0, 1),
            (2, nq + nkv, nkv, "plain", 1.0, None), (3, nq + 2 * nkv, D_MODEL, "silu", 1.0, None))
    widths = (nq, nkv, nkv, D_MODEL)
    wb = w_in.astype(BF16)
    gains = jnp.stack([g_q, g_k]).astype(F32)
    tabs = _rope_tables(t_len, C_HEAD, "unit")
    q, k, v, g = _proj_call(x, mod, tabs, wb, gains, segs, widths, C_HEAD // 2, tm, "gqa_proj")
    q_c, k_c, v_c, g_c = _proj_call(ctx, mod_c, None, wb, gains, segs, widths, C_HEAD // 2, t_c, "gqa_proj_ctx")
    per = C_HEADS // C_KV_HEADS
    rows = tuple((j, None, 0) for j in range(per))
    maps = dict(n_groups=C_KV_HEADS, q_map=lambda gi, j: per * gi + j, k_map=lambda gi, j: gi, v_map=lambda gi: gi,
                rows=rows, out_width=nq, lambda_init=0.0)
    o = _dense_attn_call("stack", q, k_c, v_c, k, v, (), tq=tq, tk=tk, name="gqa_attn", **maps)
    o_c = _dense_attn_call("stack", q_c, k_c, v_c, None, None, (), tq=t_c, tk=tk, name="gqa_attn_ctx", **maps) \
        if need_ctx else None
    return o, g, o_c, g_c


def _swa_layer(x, ctx, mod, mod_c, need_ctx, w_in, sinks):
    t_len, t_c = x.shape[1], ctx.shape[1]
    tm, _, _ = _tiles(t_len)
    nq, nkv = D_HEADS * D_HEAD, D_KV_HEADS * D_HEAD
    wk = w_in[:, nq:nq + nkv]
    wv = w_in[:, nq + nkv:nq + 2 * nkv]
    swap = lambda w: jnp.concatenate([w, w[:, D_HEAD:], w[:, :D_HEAD]], axis=1)
    wb = jnp.concatenate([w_in[:, :nq], swap(wk), swap(wv), w_in[:, nq + 2 * nkv:]], axis=1).astype(BF16)
    q_scale = D_HEAD ** -0.5 * LOG2E
    segs = ((0, 0, nq, "rope", q_scale, None), (1, nq, 2 * nkv, "rope", 1.0, None),
            (2, nq + 2 * nkv, 2 * nkv, "plain", 1.0, None), (3, nq + 4 * nkv, D_MODEL, "silu", 1.0, None))
    widths = (nq, 2 * nkv, 2 * nkv, D_MODEL)
    tabs = _rope_tables(t_len, D_HEAD, "unit")
    q, k, v, g = _proj_call(x, mod, tabs, wb, None, segs, widths, D_HEAD // 2, tm, "swa_proj")
    q_c, k_c, v_c, g_c = _proj_call(ctx, mod_c, None, wb, None, segs, widths, D_HEAD // 2, t_c, "swa_proj_ctx")
    sinks_log2 = sinks.astype(F32) * LOG2E
    o = _swa_call(q, k_c, v_c, k, v, sinks_log2, WINDOW, "swa_attn")
    o_c = _swa_call(q_c, k_c, v_c, None, None, sinks_log2, t_c, "swa_attn_ctx") if need_ctx else None
    return o, g, o_c, g_c


def kernel(x, c, ctx, c_ctx, ada_w, ada_b, out_w, ln_g, ln_b, mla_w_in, mla_g_qa, mla_w_qb, mla_g_kva, mla_w_kvb,
           diff_w_in, diff_lambda, diff_g_sub, gqa_w_in, gqa_g_q, gqa_g_k, swa_w_in, swa_sink):
    depth = ada_w.shape[0]
    bsz, t_len, d = x.shape
    t_c = ctx.shape[1]
    alpha = (2 * depth) ** 0.25
    n_rows = 8
    assert bsz + 1 <= n_rows
    cvec = jnp.concatenate([c, c_ctx[None, :], jnp.zeros((n_rows - bsz - 1, d), F32)], axis=0)
    mods = _adaln_call(cvec, ada_w, ada_b)
    tm = _tiles(t_len)[0]
    for i in range(depth):
        kind, j = i % 4, i // 4
        need_ctx = i < depth - 1
        mod = mods[i, 0:bsz][:, None, :]
        mod_c = jnp.broadcast_to(mods[i, bsz][None, None, :], (bsz, 1, 3 * d))
        if kind == 0:
            o, g, o_c, g_c = _mla_layer(x, ctx, mod, mod_c, need_ctx, mla_w_in[j], mla_g_qa[j], mla_w_qb[j],
                                        mla_g_kva[j], mla_w_kvb[j])
        elif kind == 1:
            o, g, o_c, g_c = _diff_layer(x, ctx, mod, mod_c, need_ctx, i, diff_w_in[j], diff_lambda[j], diff_g_sub[j])
        elif kind == 2:
            o, g, o_c, g_c = _gqa_layer(x, ctx, mod, mod_c, need_ctx, gqa_w_in[j], gqa_g_q[j], gqa_g_k[j])
        else:
            o, g, o_c, g_c = _swa_layer(x, ctx, mod, mod_c, need_ctx, swa_w_in[j], swa_sink[j])
        wo = out_w[i].astype(BF16)
        x = _out_call(x, o, g, mod, wo, ln_g[i], ln_b[i], alpha, tm, "out_proj")
        if need_ctx:
            ctx = _out_call(ctx, o_c, g_c, mod_c, wo, ln_g[i], ln_b[i], alpha, t_c, "out_proj_ctx")
    return x
```

```python
import functools
import math

import jax
import jax.numpy as jnp
from jax import lax
from jax.experimental import pallas as pl
from jax.experimental.pallas import tpu as pltpu

D_MODEL = 1024
GRID_W = 64
WINDOW = 128
ROPE_THETA = 10000.0
NORM_EPS = 1e-6
A_HEADS, A_Q_LORA, A_KV_LORA, A_NOPE, A_ROPE, A_V = 16, 256, 128, 64, 32, 64
B_HEADS, B_HEAD = 8, 64
C_HEADS, C_KV_HEADS, C_HEAD = 8, 2, 128
D_HEADS, D_KV_HEADS, D_HEAD = 16, 2, 64

LANE = 128
HALF = LANE // 2
LOG2E = math.log2(math.e)
NEG_BIG = -1e30
VMEM_LIMIT = 56 * 1024 * 1024
BF16 = jnp.bfloat16
F32 = jnp.float32


def _cparams(n_grid):
    return pltpu.CompilerParams(dimension_semantics=("arbitrary",) * n_grid, vmem_limit_bytes=VMEM_LIMIT)


def _silu(x):
    return x / (1.0 + jnp.exp(-x))


def _dot(a, b):
    return jnp.dot(a, b, preferred_element_type=F32)


def _dot_nt(a, b):
    return lax.dot_general(a, b, (((1,), (1,)), ((), ())), preferred_element_type=F32)


def _adaln_kernel(c_ref, w_ref, b_ref, o_ref):
    sc = _silu(c_ref[...])
    o_ref[0] = jnp.dot(sc, w_ref[0], preferred_element_type=F32, precision=lax.Precision.HIGHEST) + b_ref[0]


def _adaln_call(cvec, ada_w, ada_b):
    depth, d, n = ada_w.shape
    rows = cvec.shape[0]
    tn = 1024
    return pl.pallas_call(
        _adaln_kernel,
        grid=(depth, n // tn),
        in_specs=[
            pl.BlockSpec((rows, d), lambda i, j: (0, 0)),
            pl.BlockSpec((1, d, tn), lambda i, j: (i, 0, j)),
            pl.BlockSpec((1, 1, tn), lambda i, j: (i, 0, j)),
        ],
        out_specs=pl.BlockSpec((1, rows, tn), lambda i, j: (i, 0, j)),
        out_shape=jax.ShapeDtypeStruct((depth, rows, n), F32),
        compiler_params=_cparams(2),
        name="adaln",
    )(cvec, ada_w, ada_b.reshape(depth, 1, n))


def _rope_tables(t_len, rot_dim, layout):
    rows = t_len // GRID_W
    row = jnp.repeat(jnp.arange(rows, dtype=F32), GRID_W)
    col = jnp.tile(jnp.arange(GRID_W, dtype=F32), rows)
    n_freq = rot_dim // 4
    inv_freq = ROPE_THETA ** (-jnp.arange(n_freq, dtype=F32) / n_freq)
    ang = jnp.concatenate([row[:, None] * inv_freq, col[:, None] * inv_freq], axis=-1)
    cos, sin = jnp.cos(ang), jnp.sin(ang)
    zero = jnp.zeros_like(sin)
    if layout == "mla":
        one = jnp.ones((t_len, A_NOPE), F32)
        pad = jnp.zeros((t_len, LANE - A_NOPE - A_ROPE), F32)
        c = jnp.concatenate([one, cos, cos, pad], axis=-1)
        s_lo = jnp.concatenate([0 * one, -sin, zero, pad], axis=-1)
        s_hi = jnp.concatenate([0 * one, zero, sin, pad], axis=-1)
    else:
        reps = LANE // rot_dim
        c = jnp.tile(jnp.concatenate([cos, cos], axis=-1), (1, reps))
        s_lo = jnp.tile(jnp.concatenate([-sin, zero], axis=-1), (1, reps))
        s_hi = jnp.tile(jnp.concatenate([zero, sin], axis=-1), (1, reps))
    return c, s_lo, s_hi


def _rope_slab(x, c, s_lo, s_hi, half):
    if half == HALF:
        return x * c + pltpu.roll(x, HALF, 1) * (s_lo + s_hi)
    return x * c + pltpu.roll(x, LANE - half, 1) * s_lo + pltpu.roll(x, half, 1) * s_hi


def _modulate(x_ref, mod_ref):
    d = x_ref.shape[-1]
    shift = mod_ref[0, :, 0:d]
    scale = mod_ref[0, :, d:2 * d]
    return (x_ref[0] * (1.0 + scale) + shift).astype(BF16)


def _rms(x, g):
    return x * lax.rsqrt(jnp.mean(x * x, axis=-1, keepdims=True) + NORM_EPS) * g


PROJ_CHUNK = 512


def _proj_kernel(*refs, segs, rope_half, has_rope, has_norm):
    it = iter(refs)
    x_ref, mod_ref = next(it), next(it)
    if has_rope:
        c = next(it)[...]
        s_lo = next(it)[...]
        s_hi = next(it)[...]
    w_ref = next(it)
    gn_ref = next(it) if has_norm else None
    out_refs = list(it)
    hb = _modulate(x_ref, mod_ref)
    for (oi, col0, width, kind, q_scale, norm_row) in segs:
        for cc in range(0, width, PROJ_CHUNK):
            cw = min(PROJ_CHUNK, width - cc)
            t = _dot(hb, w_ref[:, col0 + cc:col0 + cc + cw])
            if kind == "silu":
                y = _silu(t)
            elif kind == "plain":
                y = t
            else:
                slabs = []
                for s in range(cw // LANE):
                    xs = t[:, s * LANE:(s + 1) * LANE]
                    if norm_row is not None:
                        xs = _rms(xs, gn_ref[norm_row:norm_row + 1, :])
                    if has_rope:
                        xs = _rope_slab(xs, c, s_lo, s_hi, rope_half)
                    if q_scale != 1.0:
                        xs = xs * q_scale
                    slabs.append(xs)
                y = jnp.concatenate(slabs, axis=1) if len(slabs) > 1 else slabs[0]
            out_refs[oi][0, :, cc:cc + cw] = y.astype(BF16)


def _proj_call(x, mod, tabs, w, gains, segs, out_widths, rope_half, tm, name):
    bsz, t_len, d = x.shape
    has_rope = tabs is not None
    has_norm = gains is not None
    args = [x, mod]
    in_specs = [
        pl.BlockSpec((1, tm, d), lambda b, i: (b, i, 0)),
        pl.BlockSpec((1, 1, mod.shape[-1]), lambda b, i: (b, 0, 0)),
    ]
    if has_rope:
        for tab in tabs:
            args.append(tab)
            in_specs.append(pl.BlockSpec((tm, LANE), lambda b, i: (i, 0)))
    args.append(w)
    in_specs.append(pl.BlockSpec(w.shape, lambda b, i: (0, 0)))
    if has_norm:
        args.append(gains)
        in_specs.append(pl.BlockSpec(gains.shape, lambda b, i: (0, 0)))
    return pl.pallas_call(
        functools.partial(_proj_kernel, segs=segs, rope_half=rope_half, has_rope=has_rope, has_norm=has_norm),
        grid=(bsz, t_len // tm),
        in_specs=in_specs,
        out_specs=[pl.BlockSpec((1, tm, ow), lambda b, i: (b, i, 0)) for ow in out_widths],
        out_shape=[jax.ShapeDtypeStruct((bsz, t_len, ow), BF16) for ow in out_widths],
        compiler_params=_cparams(2),
        name=name,
    )(*args)


def _mla_proj_kernel(*refs, has_rope, q_scale):
    it = iter(refs)
    x_ref, mod_ref = next(it), next(it)
    if has_rope:
        c = next(it)[...]
        s_lo = next(it)[...]
        s_hi = next(it)[...]
    w_in_ref, gqa_ref, w_qb_ref, gkva_ref, w_kvb_ref = next(it), next(it), next(it), next(it), next(it)
    q_ref, k_ref, v_ref, g_ref = next(it), next(it), next(it), next(it)
    half = A_ROPE // 2
    n_lat = A_Q_LORA + A_KV_LORA + LANE
    hb = _modulate(x_ref, mod_ref)
    t = _dot(hb, w_in_ref[:, 0:n_lat])
    qn = _rms(t[:, 0:A_Q_LORA], gqa_ref[...]).astype(BF16)
    kn = _rms(t[:, A_Q_LORA:A_Q_LORA + A_KV_LORA], gkva_ref[...]).astype(BF16)
    kpe = t[:, A_Q_LORA + A_KV_LORA:n_lat]
    if has_rope:
        kpe = _rope_slab(kpe, c, s_lo, s_hi, half)
    n_qk = A_HEADS * LANE
    for cc in range(0, n_qk, PROJ_CHUNK):
        tq = _dot(qn, w_qb_ref[:, cc:cc + PROJ_CHUNK])
        tk = _dot(kn, w_kvb_ref[:, cc:cc + PROJ_CHUNK])
        qs, ks = [], []
        for s in range(PROJ_CHUNK // LANE):
            xq = tq[:, s * LANE:(s + 1) * LANE]
            if has_rope:
                xq = _rope_slab(xq, c, s_lo, s_hi, half)
            qs.append(xq * q_scale)
            ks.append(tk[:, s * LANE:(s + 1) * LANE] + kpe)
        q_ref[0, :, cc:cc + PROJ_CHUNK] = jnp.concatenate(qs, axis=1).astype(BF16)
        k_ref[0, :, cc:cc + PROJ_CHUNK] = jnp.concatenate(ks, axis=1).astype(BF16)
    n_v = A_HEADS * A_V
    for cc in range(0, n_v, PROJ_CHUNK):
        v_ref[0, :, cc:cc + PROJ_CHUNK] = _dot(kn, w_kvb_ref[:, n_qk + cc:n_qk + cc + PROJ_CHUNK]).astype(BF16)
    for cc in range(0, D_MODEL, PROJ_CHUNK):
        g_ref[0, :, cc:cc + PROJ_CHUNK] = _silu(_dot(hb, w_in_ref[:, n_lat + cc:n_lat + cc + PROJ_CHUNK])).astype(BF16)


def _mla_proj_call(x, mod, tabs, w_in, g_qa, w_qb, g_kva, w_kvb, q_scale, tm, name):
    bsz, t_len, d = x.shape
    has_rope = tabs is not None
    args = [x, mod]
    in_specs = [
        pl.BlockSpec((1, tm, d), lambda b, i: (b, i, 0)),
        pl.BlockSpec((1, 1, mod.shape[-1]), lambda b, i: (b, 0, 0)),
    ]
    if has_rope:
        for tab in tabs:
            args.append(tab)
            in_specs.append(pl.BlockSpec((tm, LANE), lambda b, i: (i, 0)))
    for a in (w_in, g_qa, w_qb, g_kva, w_kvb):
        args.append(a)
        in_specs.append(pl.BlockSpec(a.shape, lambda b, i: (0, 0)))
    out_widths = (A_HEADS * LANE, A_HEADS * LANE, A_HEADS * A_V, D_MODEL)
    return pl.pallas_call(
        functools.partial(_mla_proj_kernel, has_rope=has_rope, q_scale=q_scale),
        grid=(bsz, t_len // tm),
        in_specs=in_specs,
        out_specs=[pl.BlockSpec((1, tm, ow), lambda b, i: (b, i, 0)) for ow in out_widths],
        out_shape=[jax.ShapeDtypeStruct((bsz, t_len, ow), BF16) for ow in out_widths],
        compiler_params=_cparams(2),
        name=name,
    )(*args)


def _lane_lo(shape):
    return lax.broadcasted_iota(jnp.int32, shape, len(shape) - 1) < HALF


def _dense_attn_kernel(*refs, mode, n_q, n_k, rows, tq, tk, n_chunks, n_parts, lambda_init):
    it = iter(refs)
    q_refs = [next(it) for _ in range(n_q)]
    kc_refs = [next(it) for _ in range(n_k)]
    vc_ref = next(it)
    if n_chunks:
        k_refs = [next(it) for _ in range(n_k)]
        v_ref = next(it)
    if mode == "diff":
        lam_ref, gsub_ref = next(it), next(it)
    o_ref = next(it)
    qs_scr, m_scr, acc_scr = next(it), next(it), next(it)
    buf_a, buf_b = next(it), next(it)

    n_rows = len(rows)
    for r, (qi, mask, _) in enumerate(rows):
        q = q_refs[qi][0]
        if mask is not None:
            lo = _lane_lo(q.shape)
            q = jnp.where(lo if mask == "lo" else jnp.logical_not(lo), q, jnp.zeros_like(q))
        qs_scr[r * tq:(r + 1) * tq, :] = q
    spans = []
    for ks in range(n_k):
        idx = [r for r, row in enumerate(rows) if row[2] == ks]
        spans.append((idx[0] * tq, (idx[-1] + 1) * tq))
    blocks_per_span = max(1, n_parts // n_k)
    row_blocks = []
    for (r0, r1) in spans:
        nb = (r1 - r0) // blocks_per_span
        row_blocks += [(r0 + i * nb, r0 + (i + 1) * nb) for i in range(blocks_per_span)]

    def qk_part(k_load, s_ref, c0, c1):
        for ks, (r0, r1) in enumerate(spans):
            s_ref[r0:r1, c0:c1] = _dot_nt(qs_scr[r0:r1, :], k_load(ks, c0, c1))

    def sm_part(v_ext, s_ref, a, b, first):
        n_cols = v_ext.shape[0] // LANE
        mp = s_ref[a:b, 0:LANE]
        for j in range(1, n_cols):
            mp = jnp.maximum(mp, s_ref[a:b, j * LANE:(j + 1) * LANE])
        smax = jnp.max(mp, axis=1, keepdims=True)
        if first:
            m_new = jnp.broadcast_to(smax, (b - a, LANE))
        else:
            m_prev = m_scr[a:b, :]
            m_new = jnp.maximum(m_prev, smax)
        ps = [jnp.exp2(s_ref[a:b, j * LANE:(j + 1) * LANE] - m_new).astype(BF16) for j in range(n_cols)]
        p = jnp.concatenate(ps, axis=1) if n_cols > 1 else ps[0]
        pv = _dot(p, v_ext)
        if first:
            acc_scr[a:b, :] = pv
        else:
            alpha = jnp.exp2(m_prev - m_new)
            acc_scr[a:b, :] = jnp.concatenate([alpha, alpha], axis=1) * acc_scr[a:b, :] + pv
        m_scr[a:b, :] = m_new

    def ext(v):
        return jnp.concatenate([v, jnp.ones(v.shape, BF16)], axis=1)

    def k_loader(off):
        return lambda ks, c0, c1: k_refs[ks][0, pl.ds(off + c0, c1 - c0), :]

    def v_chunk(off):
        return ext(v_ref[0, pl.ds(off, tk), :])

    def pipelined(v_ext, s_cur, k_next, s_next):
        kw = tk // n_parts
        for pi, (a, b) in enumerate(row_blocks):
            sm_part(v_ext, s_cur, a, b, False)
            if k_next is not None:
                qk_part(k_next, s_next, pi * kw, (pi + 1) * kw)

    t_c = vc_ref.shape[1]
    qk_part(lambda ks, c0, c1: kc_refs[ks][0, c0:c1, :], buf_a, 0, t_c)
    if n_chunks:
        qk_part(k_loader(0), buf_b, 0, tk)
    vc_ext = ext(vc_ref[0])
    for (a, b) in row_blocks:
        sm_part(vc_ext, buf_a, a, b, True)
    if n_chunks:
        n_pairs = (n_chunks - 1) // 2

        def step(pi, carry):
            off0 = pl.multiple_of(2 * pi * tk, tk)
            off1 = pl.multiple_of(off0 + tk, tk)
            off2 = pl.multiple_of(off0 + 2 * tk, tk)
            pipelined(v_chunk(off0), buf_b, k_loader(off1), buf_a)
            pipelined(v_chunk(off1), buf_a, k_loader(off2), buf_b)
            return carry
        if n_pairs:
            lax.fori_loop(0, n_pairs, step, 0)
        done = 2 * n_pairs
        if n_chunks - done == 2:
            pipelined(v_chunk(done * tk), buf_b, k_loader((done + 1) * tk), buf_a)
            pipelined(v_chunk((done + 1) * tk), buf_a, None, None)
        else:
            pipelined(v_chunk(done * tk), buf_b, None, None)

    acc = acc_scr[...]
    o = acc[:, 0:LANE] / acc[:, LANE:2 * LANE]
    if mode == "pair":
        out = jnp.where(_lane_lo((tq, LANE)), o[0:tq], o[tq:2 * tq])
    elif mode == "diff":
        lam = lam_ref[...]
        a = jnp.sum(lam[0:1, :] * lam[1:2, :], axis=1, keepdims=True)
        b = jnp.sum(lam[2:3, :] * lam[3:4, :], axis=1, keepdims=True)
        lam_full = jnp.exp(a) - jnp.exp(b) + lambda_init
        dlt = o[0:tq] - lam_full * o[tq:2 * tq]
        out = _rms(dlt, gsub_ref[...]) * (1.0 - lambda_init)
    else:
        out = jnp.concatenate([o[r * tq:(r + 1) * tq] for r in range(n_rows)], axis=1)
    o_ref[0] = out.astype(o_ref.dtype)


def _dense_attn_call(mode, q, kc, vc, k, v, extras, n_groups, q_map, k_map, v_map, rows, out_width, tq, tk,
                     lambda_init, name):
    bsz, t_q, _ = q.shape
    t_c = kc.shape[1]
    n_q = 1 + max(r[0] for r in rows)
    n_k = 1 + max(r[2] for r in rows)
    n_rows = len(rows)
    has_latent = k is not None
    n_chunks = (k.shape[1] // tk) if has_latent else 0
    args, in_specs = [], []
    for j in range(n_q):
        args.append(q)
        in_specs.append(pl.BlockSpec((1, tq, LANE), lambda b, g, i, j=j: (b, i, q_map(g, j))))
    for j in range(n_k):
        args.append(kc)
        in_specs.append(pl.BlockSpec((1, t_c, LANE), lambda b, g, i, j=j: (b, 0, k_map(g, j))))
    args.append(vc)
    in_specs.append(pl.BlockSpec((1, t_c, LANE), lambda b, g, i: (b, 0, v_map(g))))
    if has_latent:
        t_k = k.shape[1]
        for j in range(n_k):
            args.append(k)
            in_specs.append(pl.BlockSpec((1, t_k, LANE), lambda b, g, i, j=j: (b, 0, k_map(g, j))))
        args.append(v)
        in_specs.append(pl.BlockSpec((1, t_k, LANE), lambda b, g, i: (b, 0, v_map(g))))
    for e in extras:
        args.append(e)
        in_specs.append(pl.BlockSpec(e.shape, lambda b, g, i: (0, 0)))
    ow = out_width // n_groups
    s_width = max(tk, t_c) if has_latent else t_c
    return pl.pallas_call(
        functools.partial(_dense_attn_kernel, mode=mode, n_q=n_q, n_k=n_k, rows=rows, tq=tq, tk=tk,
                          n_chunks=n_chunks, n_parts=ATTN_PARTS, lambda_init=lambda_init),
        grid=(bsz, n_groups, t_q // tq),
        in_specs=in_specs,
        out_specs=pl.BlockSpec((1, tq, ow), lambda b, g, i: (b, i, g)),
        out_shape=jax.ShapeDtypeStruct((bsz, t_q, out_width), BF16),
        scratch_shapes=[
            pltpu.VMEM((n_rows * tq, LANE), BF16),
            pltpu.VMEM((n_rows * tq, LANE), F32),
            pltpu.VMEM((n_rows * tq, 2 * LANE), F32),
            pltpu.VMEM((n_rows * tq, s_width), F32),
            pltpu.VMEM((n_rows * tq, s_width), F32),
        ],
        compiler_params=_cparams(3),
        name=name,
    )(*args)


def _swa_kernel(*refs, latent, tq, n_blocks):
    it = iter(refs)
    sink_ref = next(it)
    q_ref = next(it)
    kc_ref, vc_ref = next(it), next(it)
    if latent:
        kp_ref, k0_ref, kn_ref = next(it), next(it), next(it)
        vp_ref, v0_ref, vn_ref = next(it), next(it), next(it)
    o_ref = next(it)
    i = pl.program_id(1)
    per_g = D_HEADS // D_KV_HEADS
    n_pairs = per_g // 2
    lo = _lane_lo((tq, LANE))
    if latent:
        r = lax.broadcasted_iota(jnp.int32, (tq, tq), 0)
        cidx = lax.broadcasted_iota(jnp.int32, (tq, tq), 1)
        ok_prev = jnp.logical_and(cidx >= r, i > 0)
        ok_next = jnp.logical_and(cidx <= r, i < n_blocks - 1)
        ok_prev = jnp.concatenate([ok_prev] * n_pairs, axis=0)
        ok_next = jnp.concatenate([ok_next] * n_pairs, axis=0)
    for g in range(D_KV_HEADS):
        halves = []
        for e in range(2):
            slab = g ^ e
            sl = slice(slab * LANE, (slab + 1) * LANE)
            qrows, sinks = [], []
            for pi in range(n_pairs):
                ps = g * n_pairs + pi
                qv = q_ref[0, :, ps * LANE:(ps + 1) * LANE]
                qrows.append(jnp.where(lo if e == 0 else jnp.logical_not(lo), qv, jnp.zeros_like(qv)))
                sinks.append(jnp.full((tq, 1), sink_ref[g * per_g + 2 * pi + e], F32))
            qst = jnp.concatenate(qrows, axis=0)
            sink = jnp.concatenate(sinks, axis=0)
            s_list = [_dot_nt(qst, kc_ref[0, :, sl])]
            v_list = [vc_ref[0, :, sl]]
            if latent:
                s_list.append(jnp.where(ok_prev, _dot_nt(qst, kp_ref[0, :, sl]), NEG_BIG))
                s_list.append(_dot_nt(qst, k0_ref[0, :, sl]))
                s_list.append(jnp.where(ok_next, _dot_nt(qst, kn_ref[0, :, sl]), NEG_BIG))
                v_list += [vp_ref[0, :, sl], v0_ref[0, :, sl], vn_ref[0, :, sl]]
            s = jnp.concatenate(s_list, axis=1)
            m = jnp.maximum(jnp.max(s, axis=1, keepdims=True), sink)
            p = jnp.exp2(s - m)
            l = jnp.sum(p, axis=1, keepdims=True) + jnp.exp2(sink - m)
            vv = jnp.concatenate(v_list, axis=0)
            halves.append(_dot(p.astype(BF16), vv) * (1.0 / l))
        for pi in range(n_pairs):
            ps = g * n_pairs + pi
            out = jnp.where(lo, halves[0][pi * tq:(pi + 1) * tq], halves[1][pi * tq:(pi + 1) * tq])
            o_ref[0, :, ps * LANE:(ps + 1) * LANE] = out.astype(o_ref.dtype)


def _swa_call(q, kc, vc, k, v, sinks_log2, tq, name):
    bsz, t_q, width = q.shape
    t_c = kc.shape[1]
    latent = k is not None
    n_blocks = t_q // tq
    args = [sinks_log2, q, kc, vc]
    in_specs = [
        pl.BlockSpec(memory_space=pltpu.SMEM),
        pl.BlockSpec((1, tq, width), lambda b, i: (b, i, 0)),
        pl.BlockSpec((1, t_c, 2 * LANE), lambda b, i: (b, 0, 0)),
        pl.BlockSpec((1, t_c, 2 * LANE), lambda b, i: (b, 0, 0)),
    ]
    if latent:
        assert tq == WINDOW
        prev_map = lambda b, i: (b, jnp.maximum(i - 1, 0), 0)
        cur_map = lambda b, i: (b, i, 0)
        next_map = lambda b, i: (b, jnp.minimum(i + 1, n_blocks - 1), 0)
        for arr in (k, v):
            for mp in (prev_map, cur_map, next_map):
                args.append(arr)
                in_specs.append(pl.BlockSpec((1, tq, 2 * LANE), mp))
    return pl.pallas_call(
        functools.partial(_swa_kernel, latent=latent, tq=tq, n_blocks=n_blocks),
        grid=(bsz, n_blocks),
        in_specs=in_specs,
        out_specs=pl.BlockSpec((1, tq, width), lambda b, i: (b, i, 0)),
        out_shape=jax.ShapeDtypeStruct((bsz, t_q, width), BF16),
        compiler_params=_cparams(2),
        name=name,
    )(*args)


def _out_kernel(x_ref, o_ref, g_ref, mod_ref, w_ref, lng_ref, lnb_ref, y_ref, *, alpha):
    d = x_ref.shape[-1]
    u = (o_ref[0].astype(F32) * g_ref[0].astype(F32)).astype(BF16)
    br = _dot(u, w_ref[...])
    gate = mod_ref[0, :, 2 * d:3 * d]
    z = alpha * x_ref[0] + gate * br
    zc = z - jnp.mean(z, axis=-1, keepdims=True)
    var = jnp.mean(zc * zc, axis=-1, keepdims=True)
    y_ref[0] = zc * lax.rsqrt(var + NORM_EPS) * lng_ref[...] + lnb_ref[...]


def _out_call(x, o, g, mod, w, ln_g, ln_b, alpha, tm, name):
    bsz, t_len, d = x.shape
    width = o.shape[-1]
    return pl.pallas_call(
        functools.partial(_out_kernel, alpha=alpha),
        grid=(bsz, t_len // tm),
        in_specs=[
            pl.BlockSpec((1, tm, d), lambda b, i: (b, i, 0)),
            pl.BlockSpec((1, tm, width), lambda b, i: (b, i, 0)),
            pl.BlockSpec((1, tm, width), lambda b, i: (b, i, 0)),
            pl.BlockSpec((1, 1, mod.shape[-1]), lambda b, i: (b, 0, 0)),
            pl.BlockSpec(w.shape, lambda b, i: (0, 0)),
            pl.BlockSpec((1, d), lambda b, i: (0, 0)),
            pl.BlockSpec((1, d), lambda b, i: (0, 0)),
        ],
        out_specs=pl.BlockSpec((1, tm, d), lambda b, i: (b, i, 0)),
        out_shape=jax.ShapeDtypeStruct((bsz, t_len, d), F32),
        compiler_params=_cparams(2),
        name=name,
    )(x, o, g, mod, w, ln_g.reshape(1, d), ln_b.reshape(1, d))


ATTN_ROWS = 1024
ATTN_TK = 1024
ATTN_PARTS = 4


def _tiles(t_len, n_row_groups=1):
    tm = min(512, t_len)
    tq = min(ATTN_ROWS // n_row_groups, t_len)
    tk = min(ATTN_TK, t_len)
    return tm, tq, tk


def _mla_layer(x, ctx, mod, mod_c, need_ctx, w_in, g_qa, w_qb, g_kva, w_kvb):
    t_len, t_c = x.shape[1], ctx.shape[1]
    tm, tq, tk = _tiles(t_len, 2)
    d = w_in.shape[0]
    o0, o1, o2 = A_Q_LORA, A_Q_LORA + A_KV_LORA, A_Q_LORA + A_KV_LORA + A_ROPE
    kpe_pad = jnp.concatenate([jnp.zeros((d, A_NOPE), F32), w_in[:, o1:o2],
                               jnp.zeros((d, LANE - A_NOPE - A_ROPE), F32)], axis=1)
    w_in_r = jnp.concatenate([w_in[:, :o1], kpe_pad, w_in[:, o2:]], axis=1).astype(BF16)
    w_qb_r = jnp.pad(w_qb.reshape(A_Q_LORA, A_HEADS, A_NOPE + A_ROPE),
                     ((0, 0), (0, 0), (0, LANE - A_NOPE - A_ROPE))).reshape(A_Q_LORA, A_HEADS * LANE).astype(BF16)
    kvb = w_kvb.reshape(A_KV_LORA, A_HEADS, A_NOPE + A_V)
    k_pad = jnp.pad(kvb[:, :, :A_NOPE], ((0, 0), (0, 0), (0, LANE - A_NOPE))).reshape(A_KV_LORA, A_HEADS * LANE)
    w_kvb_r = jnp.concatenate([k_pad, kvb[:, :, A_NOPE:].reshape(A_KV_LORA, A_HEADS * A_V)], axis=1).astype(BF16)
    q_scale = (A_NOPE + A_ROPE) ** -0.5 * LOG2E
    gq, gk = g_qa.reshape(1, -1), g_kva.reshape(1, -1)
    tabs = _rope_tables(t_len, A_ROPE, "mla")
    q, k, v, g = _mla_proj_call(x, mod, tabs, w_in_r, gq, w_qb_r, gk, w_kvb_r, q_scale, tm, "mla_proj")
    q_c, k_c, v_c, g_c = _mla_proj_call(ctx, mod_c, None, w_in_r, gq, w_qb_r, gk, w_kvb_r, q_scale, t_c, "mla_proj_ctx")
    rows = ((0, None, 0), (1, None, 1))
    maps = dict(n_groups=A_HEADS // 2, q_map=lambda gi, j: 2 * gi + j, k_map=lambda gi, j: 2 * gi + j,
                v_map=lambda gi: gi, rows=rows, out_width=A_HEADS * A_V, lambda_init=0.0)
    o = _dense_attn_call("pair", q, k_c, v_c, k, v, (), tq=tq, tk=tk, name="mla_attn", **maps)
    o_c = _dense_attn_call("pair", q_c, k_c, v_c, None, None, (), tq=t_c, tk=tk, name="mla_attn_ctx", **maps) \
        if need_ctx else None
    return o, g, o_c, g_c


def _diff_layer(x, ctx, mod, mod_c, need_ctx, layer_idx, w_in, lam, g_sub):
    t_len, t_c = x.shape[1], ctx.shape[1]
    tm, tq, tk = _tiles(t_len, 2)
    n = 2 * B_HEADS * B_HEAD
    q_scale = B_HEAD ** -0.5 * LOG2E
    segs = ((0, 0, n, "rope", q_scale, None), (1, n, n, "rope", 1.0, None),
            (2, 2 * n, n, "plain", 1.0, None), (3, 3 * n, D_MODEL, "silu", 1.0, None))
    widths = (n, n, n, D_MODEL)
    wb = w_in.astype(BF16)
    tabs = _rope_tables(t_len, B_HEAD, "unit")
    q, k, v, g = _proj_call(x, mod, tabs, wb, None, segs, widths, B_HEAD // 2, tm, "diff_proj")
    q_c, k_c, v_c, g_c = _proj_call(ctx, mod_c, None, wb, None, segs, widths, B_HEAD // 2, t_c, "diff_proj_ctx")
    lambda_init = 0.8 - 0.6 * math.exp(-0.3 * layer_idx)
    rows = ((0, "lo", 0), (0, "hi", 0))
    extras = (lam.astype(F32), g_sub.reshape(1, -1).astype(F32))
    maps = dict(n_groups=B_HEADS, q_map=lambda gi, j: gi, k_map=lambda gi, j: gi, v_map=lambda gi: gi,
                rows=rows, out_width=n, lambda_init=lambda_init)
    o = _dense_attn_call("diff", q, k_c, v_c, k, v, extras, tq=tq, tk=tk, name="diff_attn", **maps)
    o_c = _dense_attn_call("diff", q_c, k_c, v_c, None, None, extras, tq=t_c, tk=tk, name="diff_attn_ctx", **maps) \
        if need_ctx else None
    return o, g, o_c, g_c


def _gqa_layer(x, ctx, mod, mod_c, need_ctx, w_in, g_q, g_k):
    t_len, t_c = x.shape[1], ctx.shape[1]
    tm, tq, tk = _tiles(t_len, C_HEADS // C_KV_HEADS)
    nq, nkv = C_HEADS * C_HEAD, C_KV_HEADS * C_HEAD
    q_scale = C_HEAD ** -0.5 * LOG2E
    segs = ((0, 0, nq, "rope", q_scale, 0), (1, nq, nkv, "rope", 1.0, 1),
            (2, nq + nkv, nkv, "plain", 1.0, None), (3, nq + 2 * nkv, D_MODEL, "silu", 1.0, None))
    widths = (nq, nkv, nkv, D_MODEL)
    wb = w_in.astype(BF16)
    gains = jnp.stack([g_q, g_k]).astype(F32)
    tabs = _rope_tables(t_len, C_HEAD, "unit")
    q, k, v, g = _proj_call(x, mod, tabs, wb, gains, segs, widths, C_HEAD // 2, tm, "gqa_proj")
    q_c, k_c, v_c, g_c = _proj_call(ctx, mod_c, None, wb, gains, segs, widths, C_HEAD // 2, t_c, "gqa_proj_ctx")
    per = C_HEADS // C_KV_HEADS
    rows = tuple((j, None, 0) for j in range(per))
    maps = dict(n_groups=C_KV_HEADS, q_map=lambda gi, j: per * gi + j, k_map=lambda gi, j: gi, v_map=lambda gi: gi,
                rows=rows, out_width=nq, lambda_init=0.0)
    o = _dense_attn_call("stack", q, k_c, v_c, k, v, (), tq=tq, tk=tk, name="gqa_attn", **maps)
    o_c = _dense_attn_call("stack", q_c, k_c, v_c, None, None, (), tq=t_c, tk=tk, name="gqa_attn_ctx", **maps) \
        if need_ctx else None
    return o, g, o_c, g_c


def _swa_layer(x, ctx, mod, mod_c, need_ctx, w_in, sinks):
    t_len, t_c = x.shape[1], ctx.shape[1]
    tm, _, _ = _tiles(t_len)
    nq, nkv = D_HEADS * D_HEAD, D_KV_HEADS * D_HEAD
    wk = w_in[:, nq:nq + nkv]
    wv = w_in[:, nq + nkv:nq + 2 * nkv]
    swap = lambda w: jnp.concatenate([w, w[:, D_HEAD:], w[:, :D_HEAD]], axis=1)
    wb = jnp.concatenate([w_in[:, :nq], swap(wk), swap(wv), w_in[:, nq + 2 * nkv:]], axis=1).astype(BF16)
    q_scale = D_HEAD ** -0.5 * LOG2E
    segs = ((0, 0, nq, "rope", q_scale, None), (1, nq, 2 * nkv, "rope", 1.0, None),
            (2, nq + 2 * nkv, 2 * nkv, "plain", 1.0, None), (3, nq + 4 * nkv, D_MODEL, "silu", 1.0, None))
    widths = (nq, 2 * nkv, 2 * nkv, D_MODEL)
    tabs = _rope_tables(t_len, D_HEAD, "unit")
    q, k, v, g = _proj_call(x, mod, tabs, wb, None, segs, widths, D_HEAD // 2, tm, "swa_proj")
    q_c, k_c, v_c, g_c = _proj_call(ctx, mod_c, None, wb, None, segs, widths, D_HEAD // 2, t_c, "swa_proj_ctx")
    sinks_log2 = sinks.astype(F32) * LOG2E
    o = _swa_call(q, k_c, v_c, k, v, sinks_log2, WINDOW, "swa_attn")
    o_c = _swa_call(q_c, k_c, v_c, None, None, sinks_log2, t_c, "swa_attn_ctx") if need_ctx else None
    return o, g, o_c, g_c


def kernel(x, c, ctx, c_ctx, ada_w, ada_b, out_w, ln_g, ln_b, mla_w_in, mla_g_qa, mla_w_qb, mla_g_kva, mla_w_kvb,
           diff_w_in, diff_lambda, diff_g_sub, gqa_w_in, gqa_g_q, gqa_g_k, swa_w_in, swa_sink):
    depth = ada_w.shape[0]
    bsz, t_len, d = x.shape
    t_c = ctx.shape[1]
    alpha = (2 * depth) ** 0.25
    n_rows = 8
    assert bsz + 1 <= n_rows
    cvec = jnp.concatenate([c, c_ctx[None, :], jnp.zeros((n_rows - bsz - 1, d), F32)], axis=0)
    mods = _adaln_call(cvec, ada_w, ada_b)
    tm = _tiles(t_len)[0]
    for i in range(depth):
        kind, j = i % 4, i // 4
        need_ctx = i < depth - 1
        mod = mods[i, 0:bsz][:, None, :]
        mod_c = jnp.broadcast_to(mods[i, bsz][None, None, :], (bsz, 1, 3 * d))
        if kind == 0:
            o, g, o_c, g_c = _mla_layer(x, ctx, mod, mod_c, need_ctx, mla_w_in[j], mla_g_qa[j], mla_w_qb[j],
                                        mla_g_kva[j], mla_w_kvb[j])
        elif kind == 1:
            o, g, o_c, g_c = _diff_layer(x, ctx, mod, mod_c, need_ctx, i, diff_w_in[j], diff_lambda[j], diff_g_sub[j])
        elif kind == 2:
            o, g, o_c, g_c = _gqa_layer(x, ctx, mod, mod_c, need_ctx, gqa_w_in[j], gqa_g_q[j], gqa_g_k[j])
        else:
            o, g, o_c, g_c = _swa_layer(x, ctx, mod, mod_c, need_ctx, swa_w_in[j], swa_sink[j])
        wo = out_w[i].astype(BF16)
        x = _out_call(x, o, g, mod, wo, ln_g[i], ln_b[i], alpha, tm, "out_proj")
        if need_ctx:
            ctx = _out_call(ctx, o_c, g_c, mod_c, wo, ln_g[i], ln_b[i], alpha, t_c, "out_proj_ctx")
    return x
```

```python
import functools
import math

import jax
import jax.numpy as jnp
from jax import lax
from jax.experimental import pallas as pl
from jax.experimental.pallas import tpu as pltpu

D_MODEL = 1024
GRID_W = 64
WINDOW = 128
ROPE_THETA = 10000.0
NORM_EPS = 1e-6
A_HEADS, A_Q_LORA, A_KV_LORA, A_NOPE, A_ROPE, A_V = 16, 256, 128, 64, 32, 64
B_HEADS, B_HEAD = 8, 64
C_HEADS, C_KV_HEADS, C_HEAD = 8, 2, 128
D_HEADS, D_KV_HEADS, D_HEAD = 16, 2, 64

LANE = 128
HALF = LANE // 2
LOG2E = math.log2(math.e)
NEG_BIG = -1e30
VMEM_LIMIT = 56 * 1024 * 1024
BF16 = jnp.bfloat16
F32 = jnp.float32


def _cparams(n_grid):
    return pltpu.CompilerParams(dimension_semantics=("arbitrary",) * n_grid, vmem_limit_bytes=VMEM_LIMIT)


def _silu(x):
    return x / (1.0 + jnp.exp(-x))


def _dot(a, b):
    return jnp.dot(a, b, preferred_element_type=F32)


def _dot_nt(a, b):
    return lax.dot_general(a, b, (((1,), (1,)), ((), ())), preferred_element_type=F32)


def _adaln_kernel(c_ref, w_ref, b_ref, o_ref):
    sc = _silu(c_ref[...])
    o_ref[0] = jnp.dot(sc, w_ref[0], preferred_element_type=F32, precision=lax.Precision.HIGHEST) + b_ref[0]


def _adaln_call(cvec, ada_w, ada_b):
    depth, d, n = ada_w.shape
    rows = cvec.shape[0]
    tn = 1024
    return pl.pallas_call(
        _adaln_kernel,
        grid=(depth, n // tn),
        in_specs=[
            pl.BlockSpec((rows, d), lambda i, j: (0, 0)),
            pl.BlockSpec((1, d, tn), lambda i, j: (i, 0, j)),
            pl.BlockSpec((1, 1, tn), lambda i, j: (i, 0, j)),
        ],
        out_specs=pl.BlockSpec((1, rows, tn), lambda i, j: (i, 0, j)),
        out_shape=jax.ShapeDtypeStruct((depth, rows, n), F32),
        compiler_params=_cparams(2),
        name="adaln",
    )(cvec, ada_w, ada_b.reshape(depth, 1, n))


def _rope_tables(t_len, rot_dim, layout):
    rows = t_len // GRID_W
    row = jnp.repeat(jnp.arange(rows, dtype=F32), GRID_W)
    col = jnp.tile(jnp.arange(GRID_W, dtype=F32), rows)
    n_freq = rot_dim // 4
    inv_freq = ROPE_THETA ** (-jnp.arange(n_freq, dtype=F32) / n_freq)
    ang = jnp.concatenate([row[:, None] * inv_freq, col[:, None] * inv_freq], axis=-1)
    cos, sin = jnp.cos(ang), jnp.sin(ang)
    zero = jnp.zeros_like(sin)
    if layout == "mla":
        one = jnp.ones((t_len, A_NOPE), F32)
        pad = jnp.zeros((t_len, LANE - A_NOPE - A_ROPE), F32)
        c = jnp.concatenate([one, cos, cos, pad], axis=-1)
        s_lo = jnp.concatenate([0 * one, -sin, zero, pad], axis=-1)
        s_hi = jnp.concatenate([0 * one, zero, sin, pad], axis=-1)
    else:
        reps = LANE // rot_dim
        c = jnp.tile(jnp.concatenate([cos, cos], axis=-1), (1, reps))
        s_lo = jnp.tile(jnp.concatenate([-sin, zero], axis=-1), (1, reps))
        s_hi = jnp.tile(jnp.concatenate([zero, sin], axis=-1), (1, reps))
    return c, s_lo, s_hi


def _rope_slab(x, c, s_lo, s_hi, half):
    if half == HALF:
        return x * c + pltpu.roll(x, HALF, 1) * (s_lo + s_hi)
    return x * c + pltpu.roll(x, LANE - half, 1) * s_lo + pltpu.roll(x, half, 1) * s_hi


def _modulate(x_ref, mod_ref):
    d = x_ref.shape[-1]
    shift = mod_ref[0, :, 0:d]
    scale = mod_ref[0, :, d:2 * d]
    return (x_ref[0] * (1.0 + scale) + shift).astype(BF16)


def _rms(x, g):
    return x * lax.rsqrt(jnp.mean(x * x, axis=-1, keepdims=True) + NORM_EPS) * g


PROJ_CHUNK = 512


def _proj_kernel(*refs, segs, rope_half, has_rope, has_norm):
    it = iter(refs)
    x_ref, mod_ref = next(it), next(it)
    if has_rope:
        c = next(it)[...]
        s_lo = next(it)[...]
        s_hi = next(it)[...]
    w_ref = next(it)
    gn_ref = next(it) if has_norm else None
    out_refs = list(it)
    hb = _modulate(x_ref, mod_ref)
    for (oi, col0, width, kind, q_scale, norm_row) in segs:
        for cc in range(0, width, PROJ_CHUNK):
            cw = min(PROJ_CHUNK, width - cc)
            t = _dot(hb, w_ref[:, col0 + cc:col0 + cc + cw])
            if kind == "silu":
                y = _silu(t)
            elif kind == "plain":
                y = t
            else:
                slabs = []
                for s in range(cw // LANE):
                    xs = t[:, s * LANE:(s + 1) * LANE]
                    if norm_row is not None:
                        xs = _rms(xs, gn_ref[norm_row:norm_row + 1, :])
                    if has_rope:
                        xs = _rope_slab(xs, c, s_lo, s_hi, rope_half)
                    if q_scale != 1.0:
                        xs = xs * q_scale
                    slabs.append(xs)
                y = jnp.concatenate(slabs, axis=1) if len(slabs) > 1 else slabs[0]
            out_refs[oi][0, :, cc:cc + cw] = y.astype(BF16)


def _proj_call(x, mod, tabs, w, gains, segs, out_widths, rope_half, tm, name):
    bsz, t_len, d = x.shape
    has_rope = tabs is not None
    has_norm = gains is not None
    args = [x, mod]
    in_specs = [
        pl.BlockSpec((1, tm, d), lambda b, i: (b, i, 0)),
        pl.BlockSpec((1, 1, mod.shape[-1]), lambda b, i: (b, 0, 0)),
    ]
    if has_rope:
        for tab in tabs:
            args.append(tab)
            in_specs.append(pl.BlockSpec((tm, LANE), lambda b, i: (i, 0)))
    args.append(w)
    in_specs.append(pl.BlockSpec(w.shape, lambda b, i: (0, 0)))
    if has_norm:
        args.append(gains)
        in_specs.append(pl.BlockSpec(gains.shape, lambda b, i: (0, 0)))
    return pl.pallas_call(
        functools.partial(_proj_kernel, segs=segs, rope_half=rope_half, has_rope=has_rope, has_norm=has_norm),
        grid=(bsz, t_len // tm),
        in_specs=in_specs,
        out_specs=[pl.BlockSpec((1, tm, ow), lambda b, i: (b, i, 0)) for ow in out_widths],
        out_shape=[jax.ShapeDtypeStruct((bsz, t_len, ow), BF16) for ow in out_widths],
        compiler_params=_cparams(2),
        name=name,
    )(*args)


def _mla_proj_kernel(*refs, has_rope, q_scale):
    it = iter(refs)
    x_ref, mod_ref = next(it), next(it)
    if has_rope:
        c = next(it)[...]
        s_lo = next(it)[...]
        s_hi = next(it)[...]
    w_in_ref, gqa_ref, w_qb_ref, gkva_ref, w_kvb_ref = next(it), next(it), next(it), next(it), next(it)
    q_ref, k_ref, v_ref, g_ref = next(it), next(it), next(it), next(it)
    half = A_ROPE // 2
    n_lat = A_Q_LORA + A_KV_LORA + LANE
    hb = _modulate(x_ref, mod_ref)
    t = _dot(hb, w_in_ref[:, 0:n_lat])
    qn = _rms(t[:, 0:A_Q_LORA], gqa_ref[...]).astype(BF16)
    kn = _rms(t[:, A_Q_LORA:A_Q_LORA + A_KV_LORA], gkva_ref[...]).astype(BF16)
    kpe = t[:, A_Q_LORA + A_KV_LORA:n_lat]
    if has_rope:
        kpe = _rope_slab(kpe, c, s_lo, s_hi, half)
    n_qk = A_HEADS * LANE
    for cc in range(0, n_qk, PROJ_CHUNK):
        tq = _dot(qn, w_qb_ref[:, cc:cc + PROJ_CHUNK])
        tk = _dot(kn, w_kvb_ref[:, cc:cc + PROJ_CHUNK])
        qs, ks = [], []
        for s in range(PROJ_CHUNK // LANE):
            xq = tq[:, s * LANE:(s + 1) * LANE]
            if has_rope:
                xq = _rope_slab(xq, c, s_lo, s_hi, half)
            qs.append(xq * q_scale)
            ks.append(tk[:, s * LANE:(s + 1) * LANE] + kpe)
        q_ref[0, :, cc:cc + PROJ_CHUNK] = jnp.concatenate(qs, axis=1).astype(BF16)
        k_ref[0, :, cc:cc + PROJ_CHUNK] = jnp.concatenate(ks, axis=1).astype(BF16)
    n_v = A_HEADS * A_V
    for cc in range(0, n_v, PROJ_CHUNK):
        v_ref[0, :, cc:cc + PROJ_CHUNK] = _dot(kn, w_kvb_ref[:, n_qk + cc:n_qk + cc + PROJ_CHUNK]).astype(BF16)
    for cc in range(0, D_MODEL, PROJ_CHUNK):
        g_ref[0, :, cc:cc + PROJ_CHUNK] = _silu(_dot(hb, w_in_ref[:, n_lat + cc:n_lat + cc + PROJ_CHUNK])).astype(BF16)


def _mla_proj_call(x, mod, tabs, w_in, g_qa, w_qb, g_kva, w_kvb, q_scale, tm, name):
    bsz, t_len, d = x.shape
    has_rope = tabs is not None
    args = [x, mod]
    in_specs = [
        pl.BlockSpec((1, tm, d), lambda b, i: (b, i, 0)),
        pl.BlockSpec((1, 1, mod.shape[-1]), lambda b, i: (b, 0, 0)),
    ]
    if has_rope:
        for tab in tabs:
            args.append(tab)
            in_specs.append(pl.BlockSpec((tm, LANE), lambda b, i: (i, 0)))
    for a in (w_in, g_qa, w_qb, g_kva, w_kvb):
        args.append(a)
        in_specs.append(pl.BlockSpec(a.shape, lambda b, i: (0, 0)))
    out_widths = (A_HEADS * LANE, A_HEADS * LANE, A_HEADS * A_V, D_MODEL)
    return pl.pallas_call(
        functools.partial(_mla_proj_kernel, has_rope=has_rope, q_scale=q_scale),
        grid=(bsz, t_len // tm),
        in_specs=in_specs,
        out_specs=[pl.BlockSpec((1, tm, ow), lambda b, i: (b, i, 0)) for ow in out_widths],
        out_shape=[jax.ShapeDtypeStruct((bsz, t_len, ow), BF16) for ow in out_widths],
        compiler_params=_cparams(2),
        name=name,
    )(*args)


def _lane_lo(shape):
    return lax.broadcasted_iota(jnp.int32, shape, len(shape) - 1) < HALF


def _dense_attn_kernel(*refs, mode, n_q, n_k, rows, tq, tk, n_chunks, n_parts, unroll, lambda_init):
    it = iter(refs)
    q_refs = [next(it) for _ in range(n_q)]
    kc_refs = [next(it) for _ in range(n_k)]
    vc_ref = next(it)
    if n_chunks:
        k_refs = [next(it) for _ in range(n_k)]
        v_ref = next(it)
    if mode == "diff":
        lam_ref, gsub_ref = next(it), next(it)
    o_ref = next(it)
    qs_scr, m_scr, acc_scr = next(it), next(it), next(it)
    buf_a, buf_b = next(it), next(it)

    n_rows = len(rows)
    for r, (qi, mask, _) in enumerate(rows):
        q = q_refs[qi][0]
        if mask is not None:
            lo = _lane_lo(q.shape)
            q = jnp.where(lo if mask == "lo" else jnp.logical_not(lo), q, jnp.zeros_like(q))
        qs_scr[r * tq:(r + 1) * tq, :] = q
    spans = []
    for ks in range(n_k):
        idx = [r for r, row in enumerate(rows) if row[2] == ks]
        spans.append((idx[0] * tq, (idx[-1] + 1) * tq))
    blocks_per_span = max(1, n_parts // n_k)
    row_blocks = []
    for (r0, r1) in spans:
        nb = (r1 - r0) // blocks_per_span
        row_blocks += [(r0 + i * nb, r0 + (i + 1) * nb) for i in range(blocks_per_span)]

    def qk_part(k_load, s_ref, c0, c1):
        for ks, (r0, r1) in enumerate(spans):
            s_ref[r0:r1, c0:c1] = _dot_nt(qs_scr[r0:r1, :], k_load(ks, c0, c1))

    def sm_part(v_ext, s_ref, a, b, first):
        n_cols = v_ext.shape[0] // LANE
        mp = s_ref[a:b, 0:LANE]
        for j in range(1, n_cols):
            mp = jnp.maximum(mp, s_ref[a:b, j * LANE:(j + 1) * LANE])
        smax = jnp.max(mp, axis=1, keepdims=True)
        if first:
            m_new = jnp.broadcast_to(smax, (b - a, LANE))
        else:
            m_prev = m_scr[a:b, :]
            m_new = jnp.maximum(m_prev, smax)
        ps = [jnp.exp2(s_ref[a:b, j * LANE:(j + 1) * LANE] - m_new).astype(BF16) for j in range(n_cols)]
        p = jnp.concatenate(ps, axis=1) if n_cols > 1 else ps[0]
        pv = _dot(p, v_ext)
        if first:
            acc_scr[a:b, :] = pv
        else:
            alpha = jnp.exp2(m_prev - m_new)
            acc_scr[a:b, :] = jnp.concatenate([alpha, alpha], axis=1) * acc_scr[a:b, :] + pv
        m_scr[a:b, :] = m_new

    def ext(v):
        return jnp.concatenate([v, jnp.ones(v.shape, BF16)], axis=1)

    def k_loader(off):
        return lambda ks, c0, c1: k_refs[ks][0, pl.ds(off + c0, c1 - c0), :]

    def v_chunk(off):
        return ext(v_ref[0, pl.ds(off, tk), :])

    def pipelined(v_ext, s_cur, k_next, s_next):
        kw = tk // n_parts
        for pi, (a, b) in enumerate(row_blocks):
            sm_part(v_ext, s_cur, a, b, False)
            if k_next is not None:
                qk_part(k_next, s_next, pi * kw, (pi + 1) * kw)

    t_c = vc_ref.shape[1]
    qk_part(lambda ks, c0, c1: kc_refs[ks][0, c0:c1, :], buf_a, 0, t_c)
    if n_chunks:
        qk_part(k_loader(0), buf_b, 0, tk)
    vc_ext = ext(vc_ref[0])
    for (a, b) in row_blocks:
        sm_part(vc_ext, buf_a, a, b, True)
    if n_chunks:
        bufs = (buf_b, buf_a)
        n_trips = (n_chunks - 1) // unroll

        def step(ti, carry):
            base = ti * (unroll * tk)
            for u in range(unroll):
                off = pl.multiple_of(base + u * tk, tk)
                nxt = pl.multiple_of(base + (u + 1) * tk, tk)
                pipelined(v_chunk(off), bufs[u % 2], k_loader(nxt), bufs[(u + 1) % 2])
            return carry
        if n_trips:
            lax.fori_loop(0, n_trips, step, 0)
        for c in range(unroll * n_trips, n_chunks):
            last = c == n_chunks - 1
            pipelined(v_chunk(c * tk), bufs[c % 2], None if last else k_loader((c + 1) * tk), bufs[(c + 1) % 2])

    acc = acc_scr[...]
    o = acc[:, 0:LANE] / acc[:, LANE:2 * LANE]
    if mode == "pair":
        out = jnp.where(_lane_lo((tq, LANE)), o[0:tq], o[tq:2 * tq])
    elif mode == "diff":
        lam = lam_ref[...]
        a = jnp.sum(lam[0:1, :] * lam[1:2, :], axis=1, keepdims=True)
        b = jnp.sum(lam[2:3, :] * lam[3:4, :], axis=1, keepdims=True)
        lam_full = jnp.exp(a) - jnp.exp(b) + lambda_init
        dlt = o[0:tq] - lam_full * o[tq:2 * tq]
        out = _rms(dlt, gsub_ref[...]) * (1.0 - lambda_init)
    else:
        out = jnp.concatenate([o[r * tq:(r + 1) * tq] for r in range(n_rows)], axis=1)
    o_ref[0] = out.astype(o_ref.dtype)


def _dense_attn_call(mode, q, kc, vc, k, v, extras, n_groups, q_map, k_map, v_map, rows, out_width, tq, tk,
                     lambda_init, name):
    bsz, t_q, _ = q.shape
    t_c = kc.shape[1]
    n_q = 1 + max(r[0] for r in rows)
    n_k = 1 + max(r[2] for r in rows)
    n_rows = len(rows)
    has_latent = k is not None
    n_chunks = (k.shape[1] // tk) if has_latent else 0
    args, in_specs = [], []
    for j in range(n_q):
        args.append(q)
        in_specs.append(pl.BlockSpec((1, tq, LANE), lambda b, g, i, j=j: (b, i, q_map(g, j))))
    for j in range(n_k):
        args.append(kc)
        in_specs.append(pl.BlockSpec((1, t_c, LANE), lambda b, g, i, j=j: (b, 0, k_map(g, j))))
    args.append(vc)
    in_specs.append(pl.BlockSpec((1, t_c, LANE), lambda b, g, i: (b, 0, v_map(g))))
    if has_latent:
        t_k = k.shape[1]
        for j in range(n_k):
            args.append(k)
            in_specs.append(pl.BlockSpec((1, t_k, LANE), lambda b, g, i, j=j: (b, 0, k_map(g, j))))
        args.append(v)
        in_specs.append(pl.BlockSpec((1, t_k, LANE), lambda b, g, i: (b, 0, v_map(g))))
    for e in extras:
        args.append(e)
        in_specs.append(pl.BlockSpec(e.shape, lambda b, g, i: (0, 0)))
    ow = out_width // n_groups
    s_width = max(tk, t_c) if has_latent else t_c
    return pl.pallas_call(
        functools.partial(_dense_attn_kernel, mode=mode, n_q=n_q, n_k=n_k, rows=rows, tq=tq, tk=tk,
                          n_chunks=n_chunks, n_parts=ATTN_PARTS, unroll=ATTN_UNROLL, lambda_init=lambda_init),
        grid=(bsz, n_groups, t_q // tq),
        in_specs=in_specs,
        out_specs=pl.BlockSpec((1, tq, ow), lambda b, g, i: (b, i, g)),
        out_shape=jax.ShapeDtypeStruct((bsz, t_q, out_width), BF16),
        scratch_shapes=[
            pltpu.VMEM((n_rows * tq, LANE), BF16),
            pltpu.VMEM((n_rows * tq, LANE), F32),
            pltpu.VMEM((n_rows * tq, 2 * LANE), F32),
            pltpu.VMEM((n_rows * tq, s_width), F32),
            pltpu.VMEM((n_rows * tq, s_width), F32),
        ],
        compiler_params=_cparams(3),
        name=name,
    )(*args)


def _swa_kernel(*refs, latent, tq, n_blocks):
    it = iter(refs)
    sink_ref = next(it)
    q_ref = next(it)
    kc_ref, vc_ref = next(it), next(it)
    if latent:
        kp_ref, k0_ref, kn_ref = next(it), next(it), next(it)
        vp_ref, v0_ref, vn_ref = next(it), next(it), next(it)
    o_ref = next(it)
    i = pl.program_id(1)
    per_g = D_HEADS // D_KV_HEADS
    n_pairs = per_g // 2
    lo = _lane_lo((tq, LANE))
    if latent:
        r = lax.broadcasted_iota(jnp.int32, (tq, tq), 0)
        cidx = lax.broadcasted_iota(jnp.int32, (tq, tq), 1)
        ok_prev = jnp.logical_and(cidx >= r, i > 0)
        ok_next = jnp.logical_and(cidx <= r, i < n_blocks - 1)
        ok_prev = jnp.concatenate([ok_prev] * n_pairs, axis=0)
        ok_next = jnp.concatenate([ok_next] * n_pairs, axis=0)
    for g in range(D_KV_HEADS):
        halves = []
        for e in range(2):
            slab = g ^ e
            sl = slice(slab * LANE, (slab + 1) * LANE)
            qrows, sinks = [], []
            for pi in range(n_pairs):
                ps = g * n_pairs + pi
                qv = q_ref[0, :, ps * LANE:(ps + 1) * LANE]
                qrows.append(jnp.where(lo if e == 0 else jnp.logical_not(lo), qv, jnp.zeros_like(qv)))
                sinks.append(jnp.full((tq, 1), sink_ref[g * per_g + 2 * pi + e], F32))
            qst = jnp.concatenate(qrows, axis=0)
            sink = jnp.concatenate(sinks, axis=0)
            s_list = [_dot_nt(qst, kc_ref[0, :, sl])]
            v_list = [vc_ref[0, :, sl]]
            if latent:
                s_list.append(jnp.where(ok_prev, _dot_nt(qst, kp_ref[0, :, sl]), NEG_BIG))
                s_list.append(_dot_nt(qst, k0_ref[0, :, sl]))
                s_list.append(jnp.where(ok_next, _dot_nt(qst, kn_ref[0, :, sl]), NEG_BIG))
                v_list += [vp_ref[0, :, sl], v0_ref[0, :, sl], vn_ref[0, :, sl]]
            s = jnp.concatenate(s_list, axis=1)
            m = jnp.maximum(jnp.max(s, axis=1, keepdims=True), sink)
            p = jnp.exp2(s - m)
            l = jnp.sum(p, axis=1, keepdims=True) + jnp.exp2(sink - m)
            vv = jnp.concatenate(v_list, axis=0)
            halves.append(_dot(p.astype(BF16), vv) * (1.0 / l))
        for pi in range(n_pairs):
            ps = g * n_pairs + pi
            out = jnp.where(lo, halves[0][pi * tq:(pi + 1) * tq], halves[1][pi * tq:(pi + 1) * tq])
            o_ref[0, :, ps * LANE:(ps + 1) * LANE] = out.astype(o_ref.dtype)


def _swa_call(q, kc, vc, k, v, sinks_log2, tq, name):
    bsz, t_q, width = q.shape
    t_c = kc.shape[1]
    latent = k is not None
    n_blocks = t_q // tq
    args = [sinks_log2, q, kc, vc]
    in_specs = [
        pl.BlockSpec(memory_space=pltpu.SMEM),
        pl.BlockSpec((1, tq, width), lambda b, i: (b, i, 0)),
        pl.BlockSpec((1, t_c, 2 * LANE), lambda b, i: (b, 0, 0)),
        pl.BlockSpec((1, t_c, 2 * LANE), lambda b, i: (b, 0, 0)),
    ]
    if latent:
        assert tq == WINDOW
        prev_map = lambda b, i: (b, jnp.maximum(i - 1, 0), 0)
        cur_map = lambda b, i: (b, i, 0)
        next_map = lambda b, i: (b, jnp.minimum(i + 1, n_blocks - 1), 0)
        for arr in (k, v):
            for mp in (prev_map, cur_map, next_map):
                args.append(arr)
                in_specs.append(pl.BlockSpec((1, tq, 2 * LANE), mp))
    return pl.pallas_call(
        functools.partial(_swa_kernel, latent=latent, tq=tq, n_blocks=n_blocks),
        grid=(bsz, n_blocks),
        in_specs=in_specs,
        out_specs=pl.BlockSpec((1, tq, width), lambda b, i: (b, i, 0)),
        out_shape=jax.ShapeDtypeStruct((bsz, t_q, width), BF16),
        compiler_params=_cparams(2),
        name=name,
    )(*args)


def _out_kernel(x_ref, o_ref, g_ref, mod_ref, w_ref, lng_ref, lnb_ref, y_ref, *, alpha):
    d = x_ref.shape[-1]
    u = (o_ref[0].astype(F32) * g_ref[0].astype(F32)).astype(BF16)
    br = _dot(u, w_ref[...])
    gate = mod_ref[0, :, 2 * d:3 * d]
    z = alpha * x_ref[0] + gate * br
    zc = z - jnp.mean(z, axis=-1, keepdims=True)
    var = jnp.mean(zc * zc, axis=-1, keepdims=True)
    y_ref[0] = zc * lax.rsqrt(var + NORM_EPS) * lng_ref[...] + lnb_ref[...]


def _out_call(x, o, g, mod, w, ln_g, ln_b, alpha, tm, name):
    bsz, t_len, d = x.shape
    width = o.shape[-1]
    return pl.pallas_call(
        functools.partial(_out_kernel, alpha=alpha),
        grid=(bsz, t_len // tm),
        in_specs=[
            pl.BlockSpec((1, tm, d), lambda b, i: (b, i, 0)),
            pl.BlockSpec((1, tm, width), lambda b, i: (b, i, 0)),
            pl.BlockSpec((1, tm, width), lambda b, i: (b, i, 0)),
            pl.BlockSpec((1, 1, mod.shape[-1]), lambda b, i: (b, 0, 0)),
            pl.BlockSpec(w.shape, lambda b, i: (0, 0)),
            pl.BlockSpec((1, d), lambda b, i: (0, 0)),
            pl.BlockSpec((1, d), lambda b, i: (0, 0)),
        ],
        out_specs=pl.BlockSpec((1, tm, d), lambda b, i: (b, i, 0)),
        out_shape=jax.ShapeDtypeStruct((bsz, t_len, d), F32),
        compiler_params=_cparams(2),
        name=name,
    )(x, o, g, mod, w, ln_g.reshape(1, d), ln_b.reshape(1, d))


ATTN_ROWS = 1024
ATTN_TK = 1024
ATTN_UNROLL = 4
ATTN_PARTS = 4


def _tiles(t_len, n_row_groups=1):
    tm = min(512, t_len)
    tq = min(ATTN_ROWS // n_row_groups, t_len)
    tk = min(ATTN_TK, t_len)
    return tm, tq, tk


def _mla_layer(x, ctx, mod, mod_c, need_ctx, w_in, g_qa, w_qb, g_kva, w_kvb):
    t_len, t_c = x.shape[1], ctx.shape[1]
    tm, tq, tk = _tiles(t_len, 2)
    d = w_in.shape[0]
    o0, o1, o2 = A_Q_LORA, A_Q_LORA + A_KV_LORA, A_Q_LORA + A_KV_LORA + A_ROPE
    kpe_pad = jnp.concatenate([jnp.zeros((d, A_NOPE), F32), w_in[:, o1:o2],
                               jnp.zeros((d, LANE - A_NOPE - A_ROPE), F32)], axis=1)
    w_in_r = jnp.concatenate([w_in[:, :o1], kpe_pad, w_in[:, o2:]], axis=1).astype(BF16)
    w_qb_r = jnp.pad(w_qb.reshape(A_Q_LORA, A_HEADS, A_NOPE + A_ROPE),
                     ((0, 0), (0, 0), (0, LANE - A_NOPE - A_ROPE))).reshape(A_Q_LORA, A_HEADS * LANE).astype(BF16)
    kvb = w_kvb.reshape(A_KV_LORA, A_HEADS, A_NOPE + A_V)
    k_pad = jnp.pad(kvb[:, :, :A_NOPE], ((0, 0), (0, 0), (0, LANE - A_NOPE))).reshape(A_KV_LORA, A_HEADS * LANE)
    w_kvb_r = jnp.concatenate([k_pad, kvb[:, :, A_NOPE:].reshape(A_KV_LORA, A_HEADS * A_V)], axis=1).astype(BF16)
    q_scale = (A_NOPE + A_ROPE) ** -0.5 * LOG2E
    gq, gk = g_qa.reshape(1, -1), g_kva.reshape(1, -1)
    tabs = _rope_tables(t_len, A_ROPE, "mla")
    q, k, v, g = _mla_proj_call(x, mod, tabs, w_in_r, gq, w_qb_r, gk, w_kvb_r, q_scale, tm, "mla_proj")
    q_c, k_c, v_c, g_c = _mla_proj_call(ctx, mod_c, None, w_in_r, gq, w_qb_r, gk, w_kvb_r, q_scale, t_c, "mla_proj_ctx")
    rows = ((0, None, 0), (1, None, 1))
    maps = dict(n_groups=A_HEADS // 2, q_map=lambda gi, j: 2 * gi + j, k_map=lambda gi, j: 2 * gi + j,
                v_map=lambda gi: gi, rows=rows, out_width=A_HEADS * A_V, lambda_init=0.0)
    o = _dense_attn_call("pair", q, k_c, v_c, k, v, (), tq=tq, tk=tk, name="mla_attn", **maps)
    o_c = _dense_attn_call("pair", q_c, k_c, v_c, None, None, (), tq=t_c, tk=tk, name="mla_attn_ctx", **maps) \
        if need_ctx else None
    return o, g, o_c, g_c


def _diff_layer(x, ctx, mod, mod_c, need_ctx, layer_idx, w_in, lam, g_sub):
    t_len, t_c = x.shape[1], ctx.shape[1]
    tm, tq, tk = _tiles(t_len, 2)
    n = 2 * B_HEADS * B_HEAD
    q_scale = B_HEAD ** -0.5 * LOG2E
    segs = ((0, 0, n, "rope", q_scale, None), (1, n, n, "rope", 1.0, None),
            (2, 2 * n, n, "plain", 1.0, None), (3, 3 * n, D_MODEL, "silu", 1.0, None))
    widths = (n, n, n, D_MODEL)
    wb = w_in.astype(BF16)
    tabs = _rope_tables(t_len, B_HEAD, "unit")
    q, k, v, g = _proj_call(x, mod, tabs, wb, None, segs, widths, B_HEAD // 2, tm, "diff_proj")
    q_c, k_c, v_c, g_c = _proj_call(ctx, mod_c, None, wb, None, segs, widths, B_HEAD // 2, t_c, "diff_proj_ctx")
    lambda_init = 0.8 - 0.6 * math.exp(-0.3 * layer_idx)
    rows = ((0, "lo", 0), (0, "hi", 0))
    extras = (lam.astype(F32), g_sub.reshape(1, -1).astype(F32))
    maps = dict(n_groups=B_HEADS, q_map=lambda gi, j: gi, k_map=lambda gi, j: gi, v_map=lambda gi: gi,
                rows=rows, out_width=n, lambda_init=lambda_init)
    o = _dense_attn_call("diff", q, k_c, v_c, k, v, extras, tq=tq, tk=tk, name="diff_attn", **maps)
    o_c = _dense_attn_call("diff", q_c, k_c, v_c, None, None, extras, tq=t_c, tk=tk, name="diff_attn_ctx", **maps) \
        if need_ctx else None
    return o, g, o_c, g_c


def _gqa_layer(x, ctx, mod, mod_c, need_ctx, w_in, g_q, g_k):
    t_len, t_c = x.shape[1], ctx.shape[1]
    tm, tq, tk = _tiles(t_len, C_HEADS // C_KV_HEADS)
    nq, nkv = C_HEADS * C_HEAD, C_KV_HEADS * C_HEAD
    q_scale = C_HEAD ** -0.5 * LOG2E
    segs = ((0, 0, nq, "rope", q_scale, 0), (1, nq, nkv, "rope", 1.0, 1),
            (2, nq + nkv, nkv, "plain", 1.0, None), (3, nq + 2 * nkv, D_MODEL, "silu", 1.0, None))
    widths = (nq, nkv, nkv, D_MODEL)
    wb = w_in.astype(BF16)
    gains = jnp.stack([g_q, g_k]).astype(F32)
    tabs = _rope_tables(t_len, C_HEAD, "unit")
    q, k, v, g = _proj_call(x, mod, tabs, wb, gains, segs, widths, C_HEAD // 2, tm, "gqa_proj")
    q_c, k_c, v_c, g_c = _proj_call(ctx, mod_c, None, wb, gains, segs, widths, C_HEAD // 2, t_c, "gqa_proj_ctx")
    per = C_HEADS // C_KV_HEADS
    rows = tuple((j, None, 0) for j in range(per))
    maps = dict(n_groups=C_KV_HEADS, q_map=lambda gi, j: per * gi + j, k_map=lambda gi, j: gi, v_map=lambda gi: gi,
                rows=rows, out_width=nq, lambda_init=0.0)
    o = _dense_attn_call("stack", q, k_c, v_c, k, v, (), tq=tq, tk=tk, name="gqa_attn", **maps)
    o_c = _dense_attn_call("stack", q_c, k_c, v_c, None, None, (), tq=t_c, tk=tk, name="gqa_attn_ctx", **maps) \
        if need_ctx else None
    return o, g, o_c, g_c


def _swa_layer(x, ctx, mod, mod_c, need_ctx, w_in, sinks):
    t_len, t_c = x.shape[1], ctx.shape[1]
    tm, _, _ = _tiles(t_len)
    nq, nkv = D_HEADS * D_HEAD, D_KV_HEADS * D_HEAD
    wk = w_in[:, nq:nq + nkv]
    wv = w_in[:, nq + nkv:nq + 2 * nkv]
    swap = lambda w: jnp.concatenate([w, w[:, D_HEAD:], w[:, :D_HEAD]], axis=1)
    wb = jnp.concatenate([w_in[:, :nq], swap(wk), swap(wv), w_in[:, nq + 2 * nkv:]], axis=1).astype(BF16)
    q_scale = D_HEAD ** -0.5 * LOG2E
    segs = ((0, 0, nq, "rope", q_scale, None), (1, nq, 2 * nkv, "rope", 1.0, None),
            (2, nq + 2 * nkv, 2 * nkv, "plain", 1.0, None), (3, nq + 4 * nkv, D_MODEL, "silu", 1.0, None))
    widths = (nq, 2 * nkv, 2 * nkv, D_MODEL)
    tabs = _rope_tables(t_len, D_HEAD, "unit")
    q, k, v, g = _proj_call(x, mod, tabs, wb, None, segs, widths, D_HEAD // 2, tm, "swa_proj")
    q_c, k_c, v_c, g_c = _proj_call(ctx, mod_c, None, wb, None, segs, widths, D_HEAD // 2, t_c, "swa_proj_ctx")
    sinks_log2 = sinks.astype(F32) * LOG2E
    o = _swa_call(q, k_c, v_c, k, v, sinks_log2, WINDOW, "swa_attn")
    o_c = _swa_call(q_c, k_c, v_c, None, None, sinks_log2, t_c, "swa_attn_ctx") if need_ctx else None
    return o, g, o_c, g_c


def kernel(x, c, ctx, c_ctx, ada_w, ada_b, out_w, ln_g, ln_b, mla_w_in, mla_g_qa, mla_w_qb, mla_g_kva, mla_w_kvb,
           diff_w_in, diff_lambda, diff_g_sub, gqa_w_in, gqa_g_q, gqa_g_k, swa_w_in, swa_sink):
    depth = ada_w.shape[0]
    bsz, t_len, d = x.shape
    t_c = ctx.shape[1]
    alpha = (2 * depth) ** 0.25
    n_rows = 8
    assert bsz + 1 <= n_rows
    cvec = jnp.concatenate([c, c_ctx[None, :], jnp.zeros((n_rows - bsz - 1, d), F32)], axis=0)
    mods = _adaln_call(cvec, ada_w, ada_b)
    tm = _tiles(t_len)[0]
    for i in range(depth):
        kind, j = i % 4, i // 4
        need_ctx = i < depth - 1
        mod = mods[i, 0:bsz][:, None, :]
        mod_c = jnp.broadcast_to(mods[i, bsz][None, None, :], (bsz, 1, 3 * d))
        if kind == 0:
            o, g, o_c, g_c = _mla_layer(x, ctx, mod, mod_c, need_ctx, mla_w_in[j], mla_g_qa[j], mla_w_qb[j],
                                        mla_g_kva[j], mla_w_kvb[j])
        elif kind == 1:
            o, g, o_c, g_c = _diff_layer(x, ctx, mod, mod_c, need_ctx, i, diff_w_in[j], diff_lambda[j], diff_g_sub[j])
        elif kind == 2:
            o, g, o_c, g_c = _gqa_layer(x, ctx, mod, mod_c, need_ctx, gqa_w_in[j], gqa_g_q[j], gqa_g_k[j])
        else:
            o, g, o_c, g_c = _swa_layer(x, ctx, mod, mod_c, need_ctx, swa_w_in[j], swa_sink[j])
        wo = out_w[i].astype(BF16)
        x = _out_call(x, o, g, mod, wo, ln_g[i], ln_b[i], alpha, tm, "out_proj")
        if need_ctx:
            ctx = _out_call(ctx, o_c, g_c, mod_c, wo, ln_g[i], ln_b[i], alpha, t_c, "out_proj_ctx")
    return x
```

```python
import functools
import math

import jax
import jax.numpy as jnp
from jax import lax
from jax.experimental import pallas as pl
from jax.experimental.pallas import tpu as pltpu

D_MODEL = 1024
GRID_W = 64
WINDOW = 128
ROPE_THETA = 10000.0
NORM_EPS = 1e-6
A_HEADS, A_Q_LORA, A_KV_LORA, A_NOPE, A_ROPE, A_V = 16, 256, 128, 64, 32, 64
B_HEADS, B_HEAD = 8, 64
C_HEADS, C_KV_HEADS, C_HEAD = 8, 2, 128
D_HEADS, D_KV_HEADS, D_HEAD = 16, 2, 64

LANE = 128
HALF = LANE // 2
LOG2E = math.log2(math.e)
NEG_BIG = -1e30
VMEM_LIMIT = 56 * 1024 * 1024
BF16 = jnp.bfloat16
F32 = jnp.float32


def _cparams(n_grid):
    return pltpu.CompilerParams(dimension_semantics=("arbitrary",) * n_grid, vmem_limit_bytes=VMEM_LIMIT)


def _silu(x):
    return x / (1.0 + jnp.exp(-x))


def _dot(a, b):
    return jnp.dot(a, b, preferred_element_type=F32)


def _dot_nt(a, b):
    return lax.dot_general(a, b, (((1,), (1,)), ((), ())), preferred_element_type=F32)


def _adaln_kernel(c_ref, w_ref, b_ref, o_ref):
    sc = _silu(c_ref[...])
    o_ref[0] = jnp.dot(sc, w_ref[0], preferred_element_type=F32, precision=lax.Precision.HIGHEST) + b_ref[0]


def _adaln_call(cvec, ada_w, ada_b):
    depth, d, n = ada_w.shape
    rows = cvec.shape[0]
    tn = 1024
    return pl.pallas_call(
        _adaln_kernel,
        grid=(depth, n // tn),
        in_specs=[
            pl.BlockSpec((rows, d), lambda i, j: (0, 0)),
            pl.BlockSpec((1, d, tn), lambda i, j: (i, 0, j)),
            pl.BlockSpec((1, 1, tn), lambda i, j: (i, 0, j)),
        ],
        out_specs=pl.BlockSpec((1, rows, tn), lambda i, j: (i, 0, j)),
        out_shape=jax.ShapeDtypeStruct((depth, rows, n), F32),
        compiler_params=_cparams(2),
        name="adaln",
    )(cvec, ada_w, ada_b.reshape(depth, 1, n))


def _rope_tables(t_len, rot_dim, layout):
    rows = t_len // GRID_W
    row = jnp.repeat(jnp.arange(rows, dtype=F32), GRID_W)
    col = jnp.tile(jnp.arange(GRID_W, dtype=F32), rows)
    n_freq = rot_dim // 4
    inv_freq = ROPE_THETA ** (-jnp.arange(n_freq, dtype=F32) / n_freq)
    ang = jnp.concatenate([row[:, None] * inv_freq, col[:, None] * inv_freq], axis=-1)
    cos, sin = jnp.cos(ang), jnp.sin(ang)
    zero = jnp.zeros_like(sin)
    if layout == "mla":
        one = jnp.ones((t_len, A_NOPE), F32)
        pad = jnp.zeros((t_len, LANE - A_NOPE - A_ROPE), F32)
        c = jnp.concatenate([one, cos, cos, pad], axis=-1)
        s_lo = jnp.concatenate([0 * one, -sin, zero, pad], axis=-1)
        s_hi = jnp.concatenate([0 * one, zero, sin, pad], axis=-1)
    else:
        reps = LANE // rot_dim
        c = jnp.tile(jnp.concatenate([cos, cos], axis=-1), (1, reps))
        s_lo = jnp.tile(jnp.concatenate([-sin, zero], axis=-1), (1, reps))
        s_hi = jnp.tile(jnp.concatenate([zero, sin], axis=-1), (1, reps))
    return c, s_lo, s_hi


def _rope_slab(x, c, s_lo, s_hi, half):
    if half == HALF:
        return x * c + pltpu.roll(x, HALF, 1) * (s_lo + s_hi)
    return x * c + pltpu.roll(x, LANE - half, 1) * s_lo + pltpu.roll(x, half, 1) * s_hi


def _modulate(x_ref, mod_ref):
    d = x_ref.shape[-1]
    shift = mod_ref[0, :, 0:d]
    scale = mod_ref[0, :, d:2 * d]
    return (x_ref[0] * (1.0 + scale) + shift).astype(BF16)


def _rms(x, g):
    return x * lax.rsqrt(jnp.mean(x * x, axis=-1, keepdims=True) + NORM_EPS) * g


PROJ_CHUNK = 512


def _proj_kernel(*refs, segs, rope_half, has_rope, has_norm):
    it = iter(refs)
    x_ref, mod_ref = next(it), next(it)
    if has_rope:
        c = next(it)[...]
        s_lo = next(it)[...]
        s_hi = next(it)[...]
    w_ref = next(it)
    gn_ref = next(it) if has_norm else None
    out_refs = list(it)
    hb = _modulate(x_ref, mod_ref)
    for (oi, col0, width, kind, q_scale, norm_row) in segs:
        for cc in range(0, width, PROJ_CHUNK):
            cw = min(PROJ_CHUNK, width - cc)
            t = _dot(hb, w_ref[:, col0 + cc:col0 + cc + cw])
            if kind == "silu":
                y = _silu(t)
            elif kind == "plain":
                y = t
            else:
                slabs = []
                for s in range(cw // LANE):
                    xs = t[:, s * LANE:(s + 1) * LANE]
                    if norm_row is not None:
                        xs = _rms(xs, gn_ref[norm_row:norm_row + 1, :])
                    if has_rope:
                        xs = _rope_slab(xs, c, s_lo, s_hi, rope_half)
                    if q_scale != 1.0:
                        xs = xs * q_scale
                    slabs.append(xs)
                y = jnp.concatenate(slabs, axis=1) if len(slabs) > 1 else slabs[0]
            out_refs[oi][0, :, cc:cc + cw] = y.astype(BF16)


def _proj_call(x, mod, tabs, w, gains, segs, out_widths, rope_half, tm, name):
    bsz, t_len, d = x.shape
    has_rope = tabs is not None
    has_norm = gains is not None
    args = [x, mod]
    in_specs = [
        pl.BlockSpec((1, tm, d), lambda b, i: (b, i, 0)),
        pl.BlockSpec((1, 1, mod.shape[-1]), lambda b, i: (b, 0, 0)),
    ]
    if has_rope:
        for tab in tabs:
            args.append(tab)
            in_specs.append(pl.BlockSpec((tm, LANE), lambda b, i: (i, 0)))
    args.append(w)
    in_specs.append(pl.BlockSpec(w.shape, lambda b, i: (0, 0)))
    if has_norm:
        args.append(gains)
        in_specs.append(pl.BlockSpec(gains.shape, lambda b, i: (0, 0)))
    return pl.pallas_call(
        functools.partial(_proj_kernel, segs=segs, rope_half=rope_half, has_rope=has_rope, has_norm=has_norm),
        grid=(bsz, t_len // tm),
        in_specs=in_specs,
        out_specs=[pl.BlockSpec((1, tm, ow), lambda b, i: (b, i, 0)) for ow in out_widths],
        out_shape=[jax.ShapeDtypeStruct((bsz, t_len, ow), BF16) for ow in out_widths],
        compiler_params=_cparams(2),
        name=name,
    )(*args)


def _mla_proj_kernel(*refs, has_rope, q_scale):
    it = iter(refs)
    x_ref, mod_ref = next(it), next(it)
    if has_rope:
        c = next(it)[...]
        s_lo = next(it)[...]
        s_hi = next(it)[...]
    w_in_ref, gqa_ref, w_qb_ref, gkva_ref, w_kvb_ref = next(it), next(it), next(it), next(it), next(it)
    q_ref, k_ref, v_ref, g_ref = next(it), next(it), next(it), next(it)
    half = A_ROPE // 2
    n_lat = A_Q_LORA + A_KV_LORA + LANE
    hb = _modulate(x_ref, mod_ref)
    t = _dot(hb, w_in_ref[:, 0:n_lat])
    qn = _rms(t[:, 0:A_Q_LORA], gqa_ref[...]).astype(BF16)
    kn = _rms(t[:, A_Q_LORA:A_Q_LORA + A_KV_LORA], gkva_ref[...]).astype(BF16)
    kpe = t[:, A_Q_LORA + A_KV_LORA:n_lat]
    if has_rope:
        kpe = _rope_slab(kpe, c, s_lo, s_hi, half)
    n_qk = A_HEADS * LANE
    for cc in range(0, n_qk, PROJ_CHUNK):
        tq = _dot(qn, w_qb_ref[:, cc:cc + PROJ_CHUNK])
        tk = _dot(kn, w_kvb_ref[:, cc:cc + PROJ_CHUNK])
        qs, ks = [], []
        for s in range(PROJ_CHUNK // LANE):
            xq = tq[:, s * LANE:(s + 1) * LANE]
            if has_rope:
                xq = _rope_slab(xq, c, s_lo, s_hi, half)
            qs.append(xq * q_scale)
            ks.append(tk[:, s * LANE:(s + 1) * LANE] + kpe)
        q_ref[0, :, cc:cc + PROJ_CHUNK] = jnp.concatenate(qs, axis=1).astype(BF16)
        k_ref[0, :, cc:cc + PROJ_CHUNK] = jnp.concatenate(ks, axis=1).astype(BF16)
    n_v = A_HEADS * A_V
    for cc in range(0, n_v, PROJ_CHUNK):
        v_ref[0, :, cc:cc + PROJ_CHUNK] = _dot(kn, w_kvb_ref[:, n_qk + cc:n_qk + cc + PROJ_CHUNK]).astype(BF16)
    for cc in range(0, D_MODEL, PROJ_CHUNK):
        g_ref[0, :, cc:cc + PROJ_CHUNK] = _silu(_dot(hb, w_in_ref[:, n_lat + cc:n_lat + cc + PROJ_CHUNK])).astype(BF16)


def _mla_proj_call(x, mod, tabs, w_in, g_qa, w_qb, g_kva, w_kvb, q_scale, tm, name):
    bsz, t_len, d = x.shape
    has_rope = tabs is not None
    args = [x, mod]
    in_specs = [
        pl.BlockSpec((1, tm, d), lambda b, i: (b, i, 0)),
        pl.BlockSpec((1, 1, mod.shape[-1]), lambda b, i: (b, 0, 0)),
    ]
    if has_rope:
        for tab in tabs:
            args.append(tab)
            in_specs.append(pl.BlockSpec((tm, LANE), lambda b, i: (i, 0)))
    for a in (w_in, g_qa, w_qb, g_kva, w_kvb):
        args.append(a)
        in_specs.append(pl.BlockSpec(a.shape, lambda b, i: (0, 0)))
    out_widths = (A_HEADS * LANE, A_HEADS * LANE, A_HEADS * A_V, D_MODEL)
    return pl.pallas_call(
        functools.partial(_mla_proj_kernel, has_rope=has_rope, q_scale=q_scale),
        grid=(bsz, t_len // tm),
        in_specs=in_specs,
        out_specs=[pl.BlockSpec((1, tm, ow), lambda b, i: (b, i, 0)) for ow in out_widths],
        out_shape=[jax.ShapeDtypeStruct((bsz, t_len, ow), BF16) for ow in out_widths],
        compiler_params=_cparams(2),
        name=name,
    )(*args)


def _lane_lo(shape):
    return lax.broadcasted_iota(jnp.int32, shape, len(shape) - 1) < HALF


def _dense_attn_kernel(*refs, mode, n_q, n_k, rows, tq, tk, n_chunks, n_parts, unroll, lambda_init):
    it = iter(refs)
    q_refs = [next(it) for _ in range(n_q)]
    kc_refs = [next(it) for _ in range(n_k)]
    vc_ref = next(it)
    if n_chunks:
        k_refs = [next(it) for _ in range(n_k)]
        v_ref = next(it)
    if mode == "diff":
        lam_ref, gsub_ref = next(it), next(it)
    o_ref = next(it)
    qs_scr, m_scr, acc_scr = next(it), next(it), next(it)
    buf_a, buf_b = next(it), next(it)

    n_rows = len(rows)
    for r, (qi, mask, _) in enumerate(rows):
        q = q_refs[qi][0]
        if mask is not None:
            lo = _lane_lo(q.shape)
            q = jnp.where(lo if mask == "lo" else jnp.logical_not(lo), q, jnp.zeros_like(q))
        qs_scr[r * tq:(r + 1) * tq, :] = q
    spans = []
    for ks in range(n_k):
        idx = [r for r, row in enumerate(rows) if row[2] == ks]
        spans.append((idx[0] * tq, (idx[-1] + 1) * tq))
    blocks_per_span = max(1, n_parts // n_k)
    row_blocks = []
    for (r0, r1) in spans:
        nb = (r1 - r0) // blocks_per_span
        row_blocks += [(r0 + i * nb, r0 + (i + 1) * nb) for i in range(blocks_per_span)]

    def qk_part(k_load, s_ref, c0, c1):
        for ks, (r0, r1) in enumerate(spans):
            s_ref[r0:r1, c0:c1] = _dot_nt(qs_scr[r0:r1, :], k_load(ks, c0, c1))

    def sm_part(v_ext, s_ref, a, b):
        n_cols = v_ext.shape[0] // LANE
        mp = s_ref[a:b, 0:LANE]
        for j in range(1, n_cols):
            mp = jnp.maximum(mp, s_ref[a:b, j * LANE:(j + 1) * LANE])
        m_prev = m_scr[a:b, :]
        m_new = jnp.maximum(m_prev, jnp.max(mp, axis=1, keepdims=True))
        ps = [jnp.exp2(s_ref[a:b, j * LANE:(j + 1) * LANE] - m_new).astype(BF16) for j in range(n_cols)]
        p = jnp.concatenate(ps, axis=1) if n_cols > 1 else ps[0]
        pv = _dot(p, v_ext)
        alpha = jnp.exp2(m_prev - m_new)
        acc_scr[a:b, :] = jnp.concatenate([alpha, alpha], axis=1) * acc_scr[a:b, :] + pv
        m_scr[a:b, :] = m_new

    def ext(v):
        return jnp.concatenate([v, jnp.ones(v.shape, BF16)], axis=1)

    def k_loader(off):
        return lambda ks, c0, c1: k_refs[ks][0, pl.ds(off + c0, c1 - c0), :]

    def v_chunk(off):
        return ext(v_ref[0, pl.ds(off, tk), :])

    def pipelined(v_ext, s_cur, k_next, s_next, next_width):
        kw = max(LANE, next_width // n_parts)
        for pi, (a, b) in enumerate(row_blocks):
            sm_part(v_ext, s_cur, a, b)
            if pi * kw < next_width:
                qk_part(k_next, s_next, pi * kw, (pi + 1) * kw)

    bufs = (buf_a, buf_b)
    t_c = vc_ref.shape[1]
    ctx_loader = lambda ks, c0, c1: kc_refs[ks][0, c0:c1, :]
    if n_chunks:
        qk_part(k_loader(0), bufs[0], 0, tk)
    else:
        qk_part(ctx_loader, bufs[0], 0, t_c)
    m_scr[...] = jnp.full(m_scr.shape, NEG_BIG, F32)
    acc_scr[...] = jnp.zeros(acc_scr.shape, F32)
    if n_chunks:
        n_trips = (n_chunks - 1) // unroll

        def step(ti, carry):
            base = ti * (unroll * tk)
            for u in range(unroll):
                off = pl.multiple_of(base + u * tk, tk)
                nxt = pl.multiple_of(base + (u + 1) * tk, tk)
                pipelined(v_chunk(off), bufs[u % 2], k_loader(nxt), bufs[(u + 1) % 2], tk)
            return carry
        if n_trips:
            lax.fori_loop(0, n_trips, step, 0)
        for c in range(unroll * n_trips, n_chunks):
            if c == n_chunks - 1:
                pipelined(v_chunk(c * tk), bufs[c % 2], ctx_loader, bufs[(c + 1) % 2], t_c)
            else:
                pipelined(v_chunk(c * tk), bufs[c % 2], k_loader((c + 1) * tk), bufs[(c + 1) % 2], tk)
    vc_ext = ext(vc_ref[0])
    for (a, b) in row_blocks:
        sm_part(vc_ext, bufs[n_chunks % 2], a, b)

    acc = acc_scr[...]
    o = acc[:, 0:LANE] / acc[:, LANE:2 * LANE]
    if mode == "pair":
        out = jnp.where(_lane_lo((tq, LANE)), o[0:tq], o[tq:2 * tq])
    elif mode == "diff":
        lam = lam_ref[...]
        a = jnp.sum(lam[0:1, :] * lam[1:2, :], axis=1, keepdims=True)
        b = jnp.sum(lam[2:3, :] * lam[3:4, :], axis=1, keepdims=True)
        lam_full = jnp.exp(a) - jnp.exp(b) + lambda_init
        dlt = o[0:tq] - lam_full * o[tq:2 * tq]
        out = _rms(dlt, gsub_ref[...]) * (1.0 - lambda_init)
    else:
        out = jnp.concatenate([o[r * tq:(r + 1) * tq] for r in range(n_rows)], axis=1)
    o_ref[0] = out.astype(o_ref.dtype)


def _dense_attn_call(mode, q, kc, vc, k, v, extras, n_groups, q_map, k_map, v_map, rows, out_width, tq, tk,
                     lambda_init, name):
    bsz, t_q, _ = q.shape
    t_c = kc.shape[1]
    n_q = 1 + max(r[0] for r in rows)
    n_k = 1 + max(r[2] for r in rows)
    n_rows = len(rows)
    has_latent = k is not None
    n_chunks = (k.shape[1] // tk) if has_latent else 0
    args, in_specs = [], []
    for j in range(n_q):
        args.append(q)
        in_specs.append(pl.BlockSpec((1, tq, LANE), lambda b, g, i, j=j: (b, i, q_map(g, j))))
    for j in range(n_k):
        args.append(kc)
        in_specs.append(pl.BlockSpec((1, t_c, LANE), lambda b, g, i, j=j: (b, 0, k_map(g, j))))
    args.append(vc)
    in_specs.append(pl.BlockSpec((1, t_c, LANE), lambda b, g, i: (b, 0, v_map(g))))
    if has_latent:
        t_k = k.shape[1]
        for j in range(n_k):
            args.append(k)
            in_specs.append(pl.BlockSpec((1, t_k, LANE), lambda b, g, i, j=j: (b, 0, k_map(g, j))))
        args.append(v)
        in_specs.append(pl.BlockSpec((1, t_k, LANE), lambda b, g, i: (b, 0, v_map(g))))
    for e in extras:
        args.append(e)
        in_specs.append(pl.BlockSpec(e.shape, lambda b, g, i: (0, 0)))
    ow = out_width // n_groups
    s_width = max(tk, t_c) if has_latent else t_c
    return pl.pallas_call(
        functools.partial(_dense_attn_kernel, mode=mode, n_q=n_q, n_k=n_k, rows=rows, tq=tq, tk=tk,
                          n_chunks=n_chunks, n_parts=ATTN_PARTS, unroll=ATTN_UNROLL, lambda_init=lambda_init),
        grid=(bsz, n_groups, t_q // tq),
        in_specs=in_specs,
        out_specs=pl.BlockSpec((1, tq, ow), lambda b, g, i: (b, i, g)),
        out_shape=jax.ShapeDtypeStruct((bsz, t_q, out_width), BF16),
        scratch_shapes=[
            pltpu.VMEM((n_rows * tq, LANE), BF16),
            pltpu.VMEM((n_rows * tq, LANE), F32),
            pltpu.VMEM((n_rows * tq, 2 * LANE), F32),
            pltpu.VMEM((n_rows * tq, s_width), F32),
            pltpu.VMEM((n_rows * tq, s_width), F32),
        ],
        compiler_params=_cparams(3),
        name=name,
    )(*args)


def _swa_kernel(*refs, latent, tq, n_blocks):
    it = iter(refs)
    sink_ref = next(it)
    q_ref = next(it)
    kc_ref, vc_ref = next(it), next(it)
    if latent:
        kp_ref, k0_ref, kn_ref = next(it), next(it), next(it)
        vp_ref, v0_ref, vn_ref = next(it), next(it), next(it)
    o_ref = next(it)
    i = pl.program_id(1)
    per_g = D_HEADS // D_KV_HEADS
    n_pairs = per_g // 2
    lo = _lane_lo((tq, LANE))
    if latent:
        r = lax.broadcasted_iota(jnp.int32, (tq, tq), 0)
        cidx = lax.broadcasted_iota(jnp.int32, (tq, tq), 1)
        ok_prev = jnp.logical_and(cidx >= r, i > 0)
        ok_next = jnp.logical_and(cidx <= r, i < n_blocks - 1)
        ok_pc = jnp.concatenate([ok_prev, jnp.full((tq, tq), True)], axis=1)
        ok_pc = jnp.concatenate([ok_pc] * n_pairs, axis=0)
        ok_next = jnp.concatenate([ok_next] * n_pairs, axis=0)

    def scores(g, e):
        slab = g ^ e
        sl = slice(slab * LANE, (slab + 1) * LANE)
        qrows, sinks = [], []
        for pi in range(n_pairs):
            ps = g * n_pairs + pi
            qv = q_ref[0, :, ps * LANE:(ps + 1) * LANE]
            qrows.append(jnp.where(lo if e == 0 else jnp.logical_not(lo), qv, jnp.zeros_like(qv)))
            sinks.append(jnp.full((tq, LANE), sink_ref[g * per_g + 2 * pi + e], F32))
        qst = jnp.concatenate(qrows, axis=0)
        sink = jnp.concatenate(sinks, axis=0)
        s_list = [_dot_nt(qst, kc_ref[0, :, sl])]
        if latent:
            k_pc = jnp.concatenate([kp_ref[0, :, sl], k0_ref[0, :, sl]], axis=0)
            s_list.append(jnp.where(ok_pc, _dot_nt(qst, k_pc), NEG_BIG))
            s_list.append(jnp.where(ok_next, _dot_nt(qst, kn_ref[0, :, sl]), NEG_BIG))
        return jnp.concatenate(s_list, axis=1), sink, sl

    def softmax_pv(s, sink, sl):
        v_list = [vc_ref[0, :, sl]]
        if latent:
            v_list += [vp_ref[0, :, sl], v0_ref[0, :, sl], vn_ref[0, :, sl]]
        vv = jnp.concatenate(v_list, axis=0)
        vv = jnp.concatenate([vv, jnp.ones(vv.shape, BF16)], axis=1)
        m = jnp.maximum(jnp.max(s, axis=1, keepdims=True), sink)
        ps = [jnp.exp2(s[:, j * LANE:(j + 1) * LANE] - m).astype(BF16) for j in range(s.shape[1] // LANE)]
        pv = _dot(jnp.concatenate(ps, axis=1), vv)
        return pv[:, 0:LANE] / (pv[:, LANE:2 * LANE] + jnp.exp2(sink - m))

    groups = [(g, e) for g in range(D_KV_HEADS) for e in range(2)]
    outs = []
    pending = scores(*groups[0])
    for gi in range(len(groups)):
        nxt = scores(*groups[gi + 1]) if gi + 1 < len(groups) else None
        outs.append(softmax_pv(*pending))
        pending = nxt
    for g in range(D_KV_HEADS):
        for pi in range(n_pairs):
            ps = g * n_pairs + pi
            out = jnp.where(lo, outs[2 * g][pi * tq:(pi + 1) * tq], outs[2 * g + 1][pi * tq:(pi + 1) * tq])
            o_ref[0, :, ps * LANE:(ps + 1) * LANE] = out.astype(o_ref.dtype)


def _swa_call(q, kc, vc, k, v, sinks_log2, tq, name):
    bsz, t_q, width = q.shape
    t_c = kc.shape[1]
    latent = k is not None
    n_blocks = t_q // tq
    args = [sinks_log2, q, kc, vc]
    in_specs = [
        pl.BlockSpec(memory_space=pltpu.SMEM),
        pl.BlockSpec((1, tq, width), lambda b, i: (b, i, 0)),
        pl.BlockSpec((1, t_c, 2 * LANE), lambda b, i: (b, 0, 0)),
        pl.BlockSpec((1, t_c, 2 * LANE), lambda b, i: (b, 0, 0)),
    ]
    if latent:
        assert tq == WINDOW
        prev_map = lambda b, i: (b, jnp.maximum(i - 1, 0), 0)
        cur_map = lambda b, i: (b, i, 0)
        next_map = lambda b, i: (b, jnp.minimum(i + 1, n_blocks - 1), 0)
        for arr in (k, v):
            for mp in (prev_map, cur_map, next_map):
                args.append(arr)
                in_specs.append(pl.BlockSpec((1, tq, 2 * LANE), mp))
    return pl.pallas_call(
        functools.partial(_swa_kernel, latent=latent, tq=tq, n_blocks=n_blocks),
        grid=(bsz, n_blocks),
        in_specs=in_specs,
        out_specs=pl.BlockSpec((1, tq, width), lambda b, i: (b, i, 0)),
        out_shape=jax.ShapeDtypeStruct((bsz, t_q, width), BF16),
        compiler_params=_cparams(2),
        name=name,
    )(*args)


def _out_kernel(x_ref, o_ref, g_ref, mod_ref, w_ref, lng_ref, lnb_ref, y_ref, *, alpha):
    d = x_ref.shape[-1]
    u = (o_ref[0].astype(F32) * g_ref[0].astype(F32)).astype(BF16)
    br = _dot(u, w_ref[...])
    gate = mod_ref[0, :, 2 * d:3 * d]
    z = alpha * x_ref[0] + gate * br
    zc = z - jnp.mean(z, axis=-1, keepdims=True)
    var = jnp.mean(zc * zc, axis=-1, keepdims=True)
    y_ref[0] = zc * lax.rsqrt(var + NORM_EPS) * lng_ref[...] + lnb_ref[...]


def _out_call(x, o, g, mod, w, ln_g, ln_b, alpha, tm, name):
    bsz, t_len, d = x.shape
    width = o.shape[-1]
    return pl.pallas_call(
        functools.partial(_out_kernel, alpha=alpha),
        grid=(bsz, t_len // tm),
        in_specs=[
            pl.BlockSpec((1, tm, d), lambda b, i: (b, i, 0)),
            pl.BlockSpec((1, tm, width), lambda b, i: (b, i, 0)),
            pl.BlockSpec((1, tm, width), lambda b, i: (b, i, 0)),
            pl.BlockSpec((1, 1, mod.shape[-1]), lambda b, i: (b, 0, 0)),
            pl.BlockSpec(w.shape, lambda b, i: (0, 0)),
            pl.BlockSpec((1, d), lambda b, i: (0, 0)),
            pl.BlockSpec((1, d), lambda b, i: (0, 0)),
        ],
        out_specs=pl.BlockSpec((1, tm, d), lambda b, i: (b, i, 0)),
        out_shape=jax.ShapeDtypeStruct((bsz, t_len, d), F32),
        compiler_params=_cparams(2),
        name=name,
    )(x, o, g, mod, w, ln_g.reshape(1, d), ln_b.reshape(1, d))


ATTN_ROWS = 1024
ATTN_TK = 1024
ATTN_UNROLL = 4
ATTN_PARTS = 4


def _tiles(t_len, n_row_groups=1):
    tm = min(512, t_len)
    tq = min(ATTN_ROWS // n_row_groups, t_len)
    tk = min(ATTN_TK, t_len)
    return tm, tq, tk


def _mla_layer(x, ctx, mod, mod_c, need_ctx, w_in, g_qa, w_qb, g_kva, w_kvb):
    t_len, t_c = x.shape[1], ctx.shape[1]
    tm, tq, tk = _tiles(t_len, 2)
    d = w_in.shape[0]
    o0, o1, o2 = A_Q_LORA, A_Q_LORA + A_KV_LORA, A_Q_LORA + A_KV_LORA + A_ROPE
    kpe_pad = jnp.concatenate([jnp.zeros((d, A_NOPE), F32), w_in[:, o1:o2],
                               jnp.zeros((d, LANE - A_NOPE - A_ROPE), F32)], axis=1)
    w_in_r = jnp.concatenate([w_in[:, :o1], kpe_pad, w_in[:, o2:]], axis=1).astype(BF16)
    w_qb_r = jnp.pad(w_qb.reshape(A_Q_LORA, A_HEADS, A_NOPE + A_ROPE),
                     ((0, 0), (0, 0), (0, LANE - A_NOPE - A_ROPE))).reshape(A_Q_LORA, A_HEADS * LANE).astype(BF16)
    kvb = w_kvb.reshape(A_KV_LORA, A_HEADS, A_NOPE + A_V)
    k_pad = jnp.pad(kvb[:, :, :A_NOPE], ((0, 0), (0, 0), (0, LANE - A_NOPE))).reshape(A_KV_LORA, A_HEADS * LANE)
    w_kvb_r = jnp.concatenate([k_pad, kvb[:, :, A_NOPE:].reshape(A_KV_LORA, A_HEADS * A_V)], axis=1).astype(BF16)
    q_scale = (A_NOPE + A_ROPE) ** -0.5 * LOG2E
    gq, gk = g_qa.reshape(1, -1), g_kva.reshape(1, -1)
    tabs = _rope_tables(t_len, A_ROPE, "mla")
    q, k, v, g = _mla_proj_call(x, mod, tabs, w_in_r, gq, w_qb_r, gk, w_kvb_r, q_scale, tm, "mla_proj")
    q_c, k_c, v_c, g_c = _mla_proj_call(ctx, mod_c, None, w_in_r, gq, w_qb_r, gk, w_kvb_r, q_scale, t_c, "mla_proj_ctx")
    rows = ((0, None, 0), (1, None, 1))
    maps = dict(n_groups=A_HEADS // 2, q_map=lambda gi, j: 2 * gi + j, k_map=lambda gi, j: 2 * gi + j,
                v_map=lambda gi: gi, rows=rows, out_width=A_HEADS * A_V, lambda_init=0.0)
    o = _dense_attn_call("pair", q, k_c, v_c, k, v, (), tq=tq, tk=tk, name="mla_attn", **maps)
    o_c = _dense_attn_call("pair", q_c, k_c, v_c, None, None, (), tq=t_c, tk=tk, name="mla_attn_ctx", **maps) \
        if need_ctx else None
    return o, g, o_c, g_c


def _diff_layer(x, ctx, mod, mod_c, need_ctx, layer_idx, w_in, lam, g_sub):
    t_len, t_c = x.shape[1], ctx.shape[1]
    tm, tq, tk = _tiles(t_len, 2)
    n = 2 * B_HEADS * B_HEAD
    q_scale = B_HEAD ** -0.5 * LOG2E
    segs = ((0, 0, n, "rope", q_scale, None), (1, n, n, "rope", 1.0, None),
            (2, 2 * n, n, "plain", 1.0, None), (3, 3 * n, D_MODEL, "silu", 1.0, None))
    widths = (n, n, n, D_MODEL)
    wb = w_in.astype(BF16)
    tabs = _rope_tables(t_len, B_HEAD, "unit")
    q, k, v, g = _proj_call(x, mod, tabs, wb, None, segs, widths, B_HEAD // 2, tm, "diff_proj")
    q_c, k_c, v_c, g_c = _proj_call(ctx, mod_c, None, wb, None, segs, widths, B_HEAD // 2, t_c, "diff_proj_ctx")
    lambda_init = 0.8 - 0.6 * math.exp(-0.3 * layer_idx)
    rows = ((0, "lo", 0), (0, "hi", 0))
    extras = (lam.astype(F32), g_sub.reshape(1, -1).astype(F32))
    maps = dict(n_groups=B_HEADS, q_map=lambda gi, j: gi, k_map=lambda gi, j: gi, v_map=lambda gi: gi,
                rows=rows, out_width=n, lambda_init=lambda_init)
    o = _dense_attn_call("diff", q, k_c, v_c, k, v, extras, tq=tq, tk=tk, name="diff_attn", **maps)
    o_c = _dense_attn_call("diff", q_c, k_c, v_c, None, None, extras, tq=t_c, tk=tk, name="diff_attn_ctx", **maps) \
        if need_ctx else None
    return o, g, o_c, g_c


def _gqa_layer(x, ctx, mod, mod_c, need_ctx, w_in, g_q, g_k):
    t_len, t_c = x.shape[1], ctx.shape[1]
    tm, tq, tk = _tiles(t_len, C_HEADS // C_KV_HEADS)
    nq, nkv = C_HEADS * C_HEAD, C_KV_HEADS * C_HEAD
    q_scale = C_HEAD ** -0.5 * LOG2E
    segs = ((0, 0, nq, "rope", q_scale, 0), (1, nq, nkv, "rope", 1.0, 1),
            (2, nq + nkv, nkv, "plain", 1.0, None), (3, nq + 2 * nkv, D_MODEL, "silu", 1.0, None))
    widths = (nq, nkv, nkv, D_MODEL)
    wb = w_in.astype(BF16)
    gains = jnp.stack([g_q, g_k]).astype(F32)
    tabs = _rope_tables(t_len, C_HEAD, "unit")
    q, k, v, g = _proj_call(x, mod, tabs, wb, gains, segs, widths, C_HEAD // 2, tm, "gqa_proj")
    q_c, k_c, v_c, g_c = _proj_call(ctx, mod_c, None, wb, gains, segs, widths, C_HEAD // 2, t_c, "gqa_proj_ctx")
    per = C_HEADS // C_KV_HEADS
    rows = tuple((j, None, 0) for j in range(per))
    maps = dict(n_groups=C_KV_HEADS, q_map=lambda gi, j: per * gi + j, k_map=lambda gi, j: gi, v_map=lambda gi: gi,
                rows=rows, out_width=nq, lambda_init=0.0)
    o = _dense_attn_call("stack", q, k_c, v_c, k, v, (), tq=tq, tk=tk, name="gqa_attn", **maps)
    o_c = _dense_attn_call("stack", q_c, k_c, v_c, None, None, (), tq=t_c, tk=tk, name="gqa_attn_ctx", **maps) \
        if need_ctx else None
    return o, g, o_c, g_c


def _swa_layer(x, ctx, mod, mod_c, need_ctx, w_in, sinks):
    t_len, t_c = x.shape[1], ctx.shape[1]
    tm, _, _ = _tiles(t_len)
    nq, nkv = D_HEADS * D_HEAD, D_KV_HEADS * D_HEAD
    wk = w_in[:, nq:nq + nkv]
    wv = w_in[:, nq + nkv:nq + 2 * nkv]
    swap = lambda w: jnp.concatenate([w, w[:, D_HEAD:], w[:, :D_HEAD]], axis=1)
    wb = jnp.concatenate([w_in[:, :nq], swap(wk), swap(wv), w_in[:, nq + 2 * nkv:]], axis=1).astype(BF16)
    q_scale = D_HEAD ** -0.5 * LOG2E
    segs = ((0, 0, nq, "rope", q_scale, None), (1, nq, 2 * nkv, "rope", 1.0, None),
            (2, nq + 2 * nkv, 2 * nkv, "plain", 1.0, None), (3, nq + 4 * nkv, D_MODEL, "silu", 1.0, None))
    widths = (nq, 2 * nkv, 2 * nkv, D_MODEL)
    tabs = _rope_tables(t_len, D_HEAD, "unit")
    q, k, v, g = _proj_call(x, mod, tabs, wb, None, segs, widths, D_HEAD // 2, tm, "swa_proj")
    q_c, k_c, v_c, g_c = _proj_call(ctx, mod_c, None, wb, None, segs, widths, D_HEAD // 2, t_c, "swa_proj_ctx")
    sinks_log2 = sinks.astype(F32) * LOG2E
    o = _swa_call(q, k_c, v_c, k, v, sinks_log2, WINDOW, "swa_attn")
    o_c = _swa_call(q_c, k_c, v_c, None, None, sinks_log2, t_c, "swa_attn_ctx") if need_ctx else None
    return o, g, o_c, g_c


def kernel(x, c, ctx, c_ctx, ada_w, ada_b, out_w, ln_g, ln_b, mla_w_in, mla_g_qa, mla_w_qb, mla_g_kva, mla_w_kvb,
           diff_w_in, diff_lambda, diff_g_sub, gqa_w_in, gqa_g_q, gqa_g_k, swa_w_in, swa_sink):
    depth = ada_w.shape[0]
    bsz, t_len, d = x.shape
    t_c = ctx.shape[1]
    alpha = (2 * depth) ** 0.25
    n_rows = 8
    assert bsz + 1 <= n_rows
    cvec = jnp.concatenate([c, c_ctx[None, :], jnp.zeros((n_rows - bsz - 1, d), F32)], axis=0)
    mods = _adaln_call(cvec, ada_w, ada_b)
    tm = _tiles(t_len)[0]
    for i in range(depth):
        kind, j = i % 4, i // 4
        need_ctx = i < depth - 1
        mod = mods[i, 0:bsz][:, None, :]
        mod_c = jnp.broadcast_to(mods[i, bsz][None, None, :], (bsz, 1, 3 * d))
        if kind == 0:
            o, g, o_c, g_c = _mla_layer(x, ctx, mod, mod_c, need_ctx, mla_w_in[j], mla_g_qa[j], mla_w_qb[j],
                                        mla_g_kva[j], mla_w_kvb[j])
        elif kind == 1:
            o, g, o_c, g_c = _diff_layer(x, ctx, mod, mod_c, need_ctx, i, diff_w_in[j], diff_lambda[j], diff_g_sub[j])
        elif kind == 2:
            o, g, o_c, g_c = _gqa_layer(x, ctx, mod, mod_c, need_ctx, gqa_w_in[j], gqa_g_q[j], gqa_g_k[j])
        else:
            o, g, o_c, g_c = _swa_layer(x, ctx, mod, mod_c, need_ctx, swa_w_in[j], swa_sink[j])
        wo = out_w[i].astype(BF16)
        x = _out_call(x, o, g, mod, wo, ln_g[i], ln_b[i], alpha, tm, "out_proj")
        if need_ctx:
            ctx = _out_call(ctx, o_c, g_c, mod_c, wo, ln_g[i], ln_b[i], alpha, t_c, "out_proj_ctx")
    return x
```

```python
import functools
import math

import jax
import jax.numpy as jnp
from jax import lax
from jax.experimental import pallas as pl
from jax.experimental.pallas import tpu as pltpu

D_MODEL = 1024
GRID_W = 64
WINDOW = 128
ROPE_THETA = 10000.0
NORM_EPS = 1e-6
A_HEADS, A_Q_LORA, A_KV_LORA, A_NOPE, A_ROPE, A_V = 16, 256, 128, 64, 32, 64
B_HEADS, B_HEAD = 8, 64
C_HEADS, C_KV_HEADS, C_HEAD = 8, 2, 128
D_HEADS, D_KV_HEADS, D_HEAD = 16, 2, 64

LANE = 128
HALF = LANE // 2
LOG2E = math.log2(math.e)
NEG_BIG = -1e30
VMEM_LIMIT = 56 * 1024 * 1024
BF16 = jnp.bfloat16
F32 = jnp.float32


def _cparams(n_grid):
    return pltpu.CompilerParams(dimension_semantics=("arbitrary",) * n_grid, vmem_limit_bytes=VMEM_LIMIT)


def _silu(x):
    return x / (1.0 + jnp.exp(-x))


def _dot(a, b):
    return jnp.dot(a, b, preferred_element_type=F32)


def _dot_nt(a, b):
    return lax.dot_general(a, b, (((1,), (1,)), ((), ())), preferred_element_type=F32)


def _adaln_kernel(c_ref, w_ref, b_ref, o_ref):
    sc = _silu(c_ref[...])
    o_ref[0] = jnp.dot(sc, w_ref[0], preferred_element_type=F32, precision=lax.Precision.HIGHEST) + b_ref[0]


def _adaln_call(cvec, ada_w, ada_b):
    depth, d, n = ada_w.shape
    rows = cvec.shape[0]
    tn = 1024
    return pl.pallas_call(
        _adaln_kernel,
        grid=(depth, n // tn),
        in_specs=[
            pl.BlockSpec((rows, d), lambda i, j: (0, 0)),
            pl.BlockSpec((1, d, tn), lambda i, j: (i, 0, j)),
            pl.BlockSpec((1, 1, tn), lambda i, j: (i, 0, j)),
        ],
        out_specs=pl.BlockSpec((1, rows, tn), lambda i, j: (i, 0, j)),
        out_shape=jax.ShapeDtypeStruct((depth, rows, n), F32),
        compiler_params=_cparams(2),
        name="adaln",
    )(cvec, ada_w, ada_b.reshape(depth, 1, n))


def _rope_tables(t_len, rot_dim, layout):
    rows = t_len // GRID_W
    row = jnp.repeat(jnp.arange(rows, dtype=F32), GRID_W)
    col = jnp.tile(jnp.arange(GRID_W, dtype=F32), rows)
    n_freq = rot_dim // 4
    inv_freq = ROPE_THETA ** (-jnp.arange(n_freq, dtype=F32) / n_freq)
    ang = jnp.concatenate([row[:, None] * inv_freq, col[:, None] * inv_freq], axis=-1)
    cos, sin = jnp.cos(ang), jnp.sin(ang)
    zero = jnp.zeros_like(sin)
    if layout == "mla":
        one = jnp.ones((t_len, A_NOPE), F32)
        pad = jnp.zeros((t_len, LANE - A_NOPE - A_ROPE), F32)
        c = jnp.concatenate([one, cos, cos, pad], axis=-1)
        s_lo = jnp.concatenate([0 * one, -sin, zero, pad], axis=-1)
        s_hi = jnp.concatenate([0 * one, zero, sin, pad], axis=-1)
    else:
        reps = LANE // rot_dim
        c = jnp.tile(jnp.concatenate([cos, cos], axis=-1), (1, reps))
        s_lo = jnp.tile(jnp.concatenate([-sin, zero], axis=-1), (1, reps))
        s_hi = jnp.tile(jnp.concatenate([zero, sin], axis=-1), (1, reps))
    return c, s_lo, s_hi


def _rope_slab(x, c, s_lo, s_hi, half):
    if half == HALF:
        return x * c + pltpu.roll(x, HALF, 1) * (s_lo + s_hi)
    return x * c + pltpu.roll(x, LANE - half, 1) * s_lo + pltpu.roll(x, half, 1) * s_hi


def _modulate(x_ref, mod_ref):
    d = x_ref.shape[-1]
    shift = mod_ref[0, :, 0:d]
    scale = mod_ref[0, :, d:2 * d]
    return (x_ref[0] * (1.0 + scale) + shift).astype(BF16)


def _rms(x, g):
    return x * lax.rsqrt(jnp.mean(x * x, axis=-1, keepdims=True) + NORM_EPS) * g


PROJ_CHUNK = 512


def _proj_kernel(*refs, segs, rope_half, has_rope, has_norm):
    it = iter(refs)
    x_ref, mod_ref = next(it), next(it)
    if has_rope:
        c = next(it)[...]
        s_lo = next(it)[...]
        s_hi = next(it)[...]
    w_ref = next(it)
    gn_ref = next(it) if has_norm else None
    out_refs = list(it)
    hb = _modulate(x_ref, mod_ref)
    items = [(seg, cc, min(PROJ_CHUNK, seg[2] - cc)) for seg in segs for cc in range(0, seg[2], PROJ_CHUNK)]

    def matmul(item):
        (_, col0, _, _, _, _), cc, cw = item
        return _dot(hb, w_ref[:, col0 + cc:col0 + cc + cw])

    def finish(item, t):
        (oi, _, _, kind, q_scale, norm_row), cc, cw = item
        if kind == "silu":
            y = _silu(t)
        elif kind == "plain":
            y = t
        else:
            slabs = []
            for s in range(cw // LANE):
                xs = t[:, s * LANE:(s + 1) * LANE]
                if norm_row is not None:
                    xs = _rms(xs, gn_ref[norm_row:norm_row + 1, :])
                if has_rope:
                    xs = _rope_slab(xs, c, s_lo, s_hi, rope_half)
                if q_scale != 1.0:
                    xs = xs * q_scale
                slabs.append(xs)
            y = jnp.concatenate(slabs, axis=1) if len(slabs) > 1 else slabs[0]
        out_refs[oi][0, :, cc:cc + cw] = y.astype(BF16)

    t_cur = matmul(items[0])
    for n, item in enumerate(items):
        t_next = matmul(items[n + 1]) if n + 1 < len(items) else None
        finish(item, t_cur)
        t_cur = t_next


def _proj_call(x, mod, tabs, w, gains, segs, out_widths, rope_half, tm, name):
    bsz, t_len, d = x.shape
    has_rope = tabs is not None
    has_norm = gains is not None
    args = [x, mod]
    in_specs = [
        pl.BlockSpec((1, tm, d), lambda b, i: (b, i, 0)),
        pl.BlockSpec((1, 1, mod.shape[-1]), lambda b, i: (b, 0, 0)),
    ]
    if has_rope:
        for tab in tabs:
            args.append(tab)
            in_specs.append(pl.BlockSpec((tm, LANE), lambda b, i: (i, 0)))
    args.append(w)
    in_specs.append(pl.BlockSpec(w.shape, lambda b, i: (0, 0)))
    if has_norm:
        args.append(gains)
        in_specs.append(pl.BlockSpec(gains.shape, lambda b, i: (0, 0)))
    return pl.pallas_call(
        functools.partial(_proj_kernel, segs=segs, rope_half=rope_half, has_rope=has_rope, has_norm=has_norm),
        grid=(bsz, t_len // tm),
        in_specs=in_specs,
        out_specs=[pl.BlockSpec((1, tm, ow), lambda b, i: (b, i, 0)) for ow in out_widths],
        out_shape=[jax.ShapeDtypeStruct((bsz, t_len, ow), BF16) for ow in out_widths],
        compiler_params=_cparams(2),
        name=name,
    )(*args)


def _mla_proj_kernel(*refs, has_rope, q_scale):
    it = iter(refs)
    x_ref, mod_ref = next(it), next(it)
    if has_rope:
        c = next(it)[...]
        s_lo = next(it)[...]
        s_hi = next(it)[...]
    w_in_ref, gqa_ref, w_qb_ref, gkva_ref, w_kvb_ref = next(it), next(it), next(it), next(it), next(it)
    q_ref, k_ref, v_ref, g_ref = next(it), next(it), next(it), next(it)
    half = A_ROPE // 2
    n_lat = A_Q_LORA + A_KV_LORA + LANE
    hb = _modulate(x_ref, mod_ref)
    t = _dot(hb, w_in_ref[:, 0:n_lat])
    qn = _rms(t[:, 0:A_Q_LORA], gqa_ref[...]).astype(BF16)
    kn = _rms(t[:, A_Q_LORA:A_Q_LORA + A_KV_LORA], gkva_ref[...]).astype(BF16)
    kpe = t[:, A_Q_LORA + A_KV_LORA:n_lat]
    if has_rope:
        kpe = _rope_slab(kpe, c, s_lo, s_hi, half)
    n_qk = A_HEADS * LANE
    for cc in range(0, n_qk, PROJ_CHUNK):
        tq = _dot(qn, w_qb_ref[:, cc:cc + PROJ_CHUNK])
        tk = _dot(kn, w_kvb_ref[:, cc:cc + PROJ_CHUNK])
        qs, ks = [], []
        for s in range(PROJ_CHUNK // LANE):
            xq = tq[:, s * LANE:(s + 1) * LANE]
            if has_rope:
                xq = _rope_slab(xq, c, s_lo, s_hi, half)
            qs.append(xq * q_scale)
            ks.append(tk[:, s * LANE:(s + 1) * LANE] + kpe)
        q_ref[0, :, cc:cc + PROJ_CHUNK] = jnp.concatenate(qs, axis=1).astype(BF16)
        k_ref[0, :, cc:cc + PROJ_CHUNK] = jnp.concatenate(ks, axis=1).astype(BF16)
    for cc in range(0, D_MODEL, PROJ_CHUNK):
        g_ref[0, :, cc:cc + PROJ_CHUNK] = _silu(_dot(hb, w_in_ref[:, n_lat + cc:n_lat + cc + PROJ_CHUNK])).astype(BF16)
    n_v = A_HEADS * A_V
    for cc in range(0, n_v, PROJ_CHUNK):
        v_ref[0, :, cc:cc + PROJ_CHUNK] = _dot(kn, w_kvb_ref[:, n_qk + cc:n_qk + cc + PROJ_CHUNK]).astype(BF16)


def _mla_proj_call(x, mod, tabs, w_in, g_qa, w_qb, g_kva, w_kvb, q_scale, tm, name):
    bsz, t_len, d = x.shape
    has_rope = tabs is not None
    args = [x, mod]
    in_specs = [
        pl.BlockSpec((1, tm, d), lambda b, i: (b, i, 0)),
        pl.BlockSpec((1, 1, mod.shape[-1]), lambda b, i: (b, 0, 0)),
    ]
    if has_rope:
        for tab in tabs:
            args.append(tab)
            in_specs.append(pl.BlockSpec((tm, LANE), lambda b, i: (i, 0)))
    for a in (w_in, g_qa, w_qb, g_kva, w_kvb):
        args.append(a)
        in_specs.append(pl.BlockSpec(a.shape, lambda b, i: (0, 0)))
    out_widths = (A_HEADS * LANE, A_HEADS * LANE, A_HEADS * A_V, D_MODEL)
    return pl.pallas_call(
        functools.partial(_mla_proj_kernel, has_rope=has_rope, q_scale=q_scale),
        grid=(bsz, t_len // tm),
        in_specs=in_specs,
        out_specs=[pl.BlockSpec((1, tm, ow), lambda b, i: (b, i, 0)) for ow in out_widths],
        out_shape=[jax.ShapeDtypeStruct((bsz, t_len, ow), BF16) for ow in out_widths],
        compiler_params=_cparams(2),
        name=name,
    )(*args)


def _lane_lo(shape):
    return lax.broadcasted_iota(jnp.int32, shape, len(shape) - 1) < HALF


def _dense_attn_kernel(*refs, mode, n_q, n_k, rows, tq, tk, n_chunks, n_parts, unroll, lambda_init):
    it = iter(refs)
    q_refs = [next(it) for _ in range(n_q)]
    kc_refs = [next(it) for _ in range(n_k)]
    vc_ref = next(it)
    if n_chunks:
        k_refs = [next(it) for _ in range(n_k)]
        v_ref = next(it)
    if mode == "diff":
        lam_ref, gsub_ref = next(it), next(it)
    o_ref = next(it)
    qs_scr, m_scr, acc_scr = next(it), next(it), next(it)
    buf_a, buf_b = next(it), next(it)

    n_rows = len(rows)
    for r, (qi, mask, _) in enumerate(rows):
        q = q_refs[qi][0]
        if mask is not None:
            lo = _lane_lo(q.shape)
            q = jnp.where(lo if mask == "lo" else jnp.logical_not(lo), q, jnp.zeros_like(q))
        qs_scr[r * tq:(r + 1) * tq, :] = q
    spans = []
    for ks in range(n_k):
        idx = [r for r, row in enumerate(rows) if row[2] == ks]
        spans.append((idx[0] * tq, (idx[-1] + 1) * tq))
    blocks_per_span = max(1, n_parts // n_k)
    row_blocks = []
    for (r0, r1) in spans:
        nb = (r1 - r0) // blocks_per_span
        row_blocks += [(r0 + i * nb, r0 + (i + 1) * nb) for i in range(blocks_per_span)]

    def qk_part(k_load, s_ref, c0, c1):
        for ks, (r0, r1) in enumerate(spans):
            s_ref[r0:r1, c0:c1] = _dot_nt(qs_scr[r0:r1, :], k_load(ks, c0, c1))

    def sm_part(v_ext, s_ref, a, b):
        n_cols = v_ext.shape[0] // LANE
        mp = s_ref[a:b, 0:LANE]
        for j in range(1, n_cols):
            mp = jnp.maximum(mp, s_ref[a:b, j * LANE:(j + 1) * LANE])
        m_prev = m_scr[a:b, :]
        m_new = jnp.maximum(m_prev, jnp.max(mp, axis=1, keepdims=True))
        ps = [jnp.exp2(s_ref[a:b, j * LANE:(j + 1) * LANE] - m_new).astype(BF16) for j in range(n_cols)]
        p = jnp.concatenate(ps, axis=1) if n_cols > 1 else ps[0]
        pv = _dot(p, v_ext)
        alpha = jnp.exp2(m_prev - m_new)
        acc_scr[a:b, :] = jnp.concatenate([alpha, alpha], axis=1) * acc_scr[a:b, :] + pv
        m_scr[a:b, :] = m_new

    def ext(v):
        return jnp.concatenate([v, jnp.ones(v.shape, BF16)], axis=1)

    def k_loader(off):
        return lambda ks, c0, c1: k_refs[ks][0, pl.ds(off + c0, c1 - c0), :]

    def v_chunk(off):
        return ext(v_ref[0, pl.ds(off, tk), :])

    def pipelined(v_ext, s_cur, k_next, s_next, next_width):
        kw = max(LANE, next_width // n_parts)
        for pi, (a, b) in enumerate(row_blocks):
            sm_part(v_ext, s_cur, a, b)
            if pi * kw < next_width:
                qk_part(k_next, s_next, pi * kw, (pi + 1) * kw)

    bufs = (buf_a, buf_b)
    t_c = vc_ref.shape[1]
    ctx_loader = lambda ks, c0, c1: kc_refs[ks][0, c0:c1, :]
    if n_chunks:
        qk_part(k_loader(0), bufs[0], 0, tk)
    else:
        qk_part(ctx_loader, bufs[0], 0, t_c)
    m_scr[...] = jnp.full(m_scr.shape, NEG_BIG, F32)
    acc_scr[...] = jnp.zeros(acc_scr.shape, F32)
    if n_chunks:
        n_trips = (n_chunks - 1) // unroll

        def step(ti, carry):
            base = ti * (unroll * tk)
            for u in range(unroll):
                off = pl.multiple_of(base + u * tk, tk)
                nxt = pl.multiple_of(base + (u + 1) * tk, tk)
                pipelined(v_chunk(off), bufs[u % 2], k_loader(nxt), bufs[(u + 1) % 2], tk)
            return carry
        if n_trips:
            lax.fori_loop(0, n_trips, step, 0)
        for c in range(unroll * n_trips, n_chunks):
            if c == n_chunks - 1:
                pipelined(v_chunk(c * tk), bufs[c % 2], ctx_loader, bufs[(c + 1) % 2], t_c)
            else:
                pipelined(v_chunk(c * tk), bufs[c % 2], k_loader((c + 1) * tk), bufs[(c + 1) % 2], tk)
    vc_ext = ext(vc_ref[0])
    for (a, b) in row_blocks:
        sm_part(vc_ext, bufs[n_chunks % 2], a, b)

    acc = acc_scr[...]
    o = acc[:, 0:LANE] / acc[:, LANE:2 * LANE]
    if mode == "pair":
        out = jnp.where(_lane_lo((tq, LANE)), o[0:tq], o[tq:2 * tq])
    elif mode == "diff":
        lam = lam_ref[...]
        a = jnp.sum(lam[0:1, :] * lam[1:2, :], axis=1, keepdims=True)
        b = jnp.sum(lam[2:3, :] * lam[3:4, :], axis=1, keepdims=True)
        lam_full = jnp.exp(a) - jnp.exp(b) + lambda_init
        dlt = o[0:tq] - lam_full * o[tq:2 * tq]
        out = _rms(dlt, gsub_ref[...]) * (1.0 - lambda_init)
    else:
        out = jnp.concatenate([o[r * tq:(r + 1) * tq] for r in range(n_rows)], axis=1)
    o_ref[0] = out.astype(o_ref.dtype)


def _dense_attn_call(mode, q, kc, vc, k, v, extras, n_groups, q_map, k_map, v_map, rows, out_width, tq, tk,
                     lambda_init, name):
    bsz, t_q, _ = q.shape
    t_c = kc.shape[1]
    n_q = 1 + max(r[0] for r in rows)
    n_k = 1 + max(r[2] for r in rows)
    n_rows = len(rows)
    has_latent = k is not None
    n_chunks = (k.shape[1] // tk) if has_latent else 0
    args, in_specs = [], []
    for j in range(n_q):
        args.append(q)
        in_specs.append(pl.BlockSpec((1, tq, LANE), lambda b, g, i, j=j: (b, i, q_map(g, j))))
    for j in range(n_k):
        args.append(kc)
        in_specs.append(pl.BlockSpec((1, t_c, LANE), lambda b, g, i, j=j: (b, 0, k_map(g, j))))
    args.append(vc)
    in_specs.append(pl.BlockSpec((1, t_c, LANE), lambda b, g, i: (b, 0, v_map(g))))
    if has_latent:
        t_k = k.shape[1]
        for j in range(n_k):
            args.append(k)
            in_specs.append(pl.BlockSpec((1, t_k, LANE), lambda b, g, i, j=j: (b, 0, k_map(g, j))))
        args.append(v)
        in_specs.append(pl.BlockSpec((1, t_k, LANE), lambda b, g, i: (b, 0, v_map(g))))
    for e in extras:
        args.append(e)
        in_specs.append(pl.BlockSpec(e.shape, lambda b, g, i: (0, 0)))
    ow = out_width // n_groups
    s_width = max(tk, t_c) if has_latent else t_c
    return pl.pallas_call(
        functools.partial(_dense_attn_kernel, mode=mode, n_q=n_q, n_k=n_k, rows=rows, tq=tq, tk=tk,
                          n_chunks=n_chunks, n_parts=ATTN_PARTS, unroll=ATTN_UNROLL, lambda_init=lambda_init),
        grid=(bsz, n_groups, t_q // tq),
        in_specs=in_specs,
        out_specs=pl.BlockSpec((1, tq, ow), lambda b, g, i: (b, i, g)),
        out_shape=jax.ShapeDtypeStruct((bsz, t_q, out_width), BF16),
        scratch_shapes=[
            pltpu.VMEM((n_rows * tq, LANE), BF16),
            pltpu.VMEM((n_rows * tq, LANE), F32),
            pltpu.VMEM((n_rows * tq, 2 * LANE), F32),
            pltpu.VMEM((n_rows * tq, s_width), F32),
            pltpu.VMEM((n_rows * tq, s_width), F32),
        ],
        compiler_params=_cparams(3),
        name=name,
    )(*args)


def _swa_kernel(*refs, latent, tq, n_blocks):
    it = iter(refs)
    sink_ref = next(it)
    q_ref = next(it)
    kc_ref, vc_ref = next(it), next(it)
    if latent:
        kp_ref, k0_ref, kn_ref = next(it), next(it), next(it)
        vp_ref, v0_ref, vn_ref = next(it), next(it), next(it)
    o_ref = next(it)
    i = pl.program_id(1)
    per_g = D_HEADS // D_KV_HEADS
    n_pairs = per_g // 2
    lo = _lane_lo((tq, LANE))
    if latent:
        r = lax.broadcasted_iota(jnp.int32, (tq, tq), 0)
        cidx = lax.broadcasted_iota(jnp.int32, (tq, tq), 1)
        ok_prev = jnp.logical_and(cidx >= r, i > 0)
        ok_next = jnp.logical_and(cidx <= r, i < n_blocks - 1)
        ok_pc = jnp.concatenate([ok_prev, jnp.full((tq, tq), True)], axis=1)
        ok_pc = jnp.concatenate([ok_pc] * n_pairs, axis=0)
        ok_next = jnp.concatenate([ok_next] * n_pairs, axis=0)

    def scores(g, e):
        slab = g ^ e
        sl = slice(slab * LANE, (slab + 1) * LANE)
        qrows, sinks = [], []
        for pi in range(n_pairs):
            ps = g * n_pairs + pi
            qv = q_ref[0, :, ps * LANE:(ps + 1) * LANE]
            qrows.append(jnp.where(lo if e == 0 else jnp.logical_not(lo), qv, jnp.zeros_like(qv)))
            sinks.append(jnp.full((tq, LANE), sink_ref[g * per_g + 2 * pi + e], F32))
        qst = jnp.concatenate(qrows, axis=0)
        sink = jnp.concatenate(sinks, axis=0)
        s_list = [_dot_nt(qst, kc_ref[0, :, sl])]
        if latent:
            k_pc = jnp.concatenate([kp_ref[0, :, sl], k0_ref[0, :, sl]], axis=0)
            s_list.append(jnp.where(ok_pc, _dot_nt(qst, k_pc), NEG_BIG))
            s_list.append(jnp.where(ok_next, _dot_nt(qst, kn_ref[0, :, sl]), NEG_BIG))
        return jnp.concatenate(s_list, axis=1), sink, sl

    def softmax_pv(s, sink, sl):
        v_list = [vc_ref[0, :, sl]]
        if latent:
            v_list += [vp_ref[0, :, sl], v0_ref[0, :, sl], vn_ref[0, :, sl]]
        vv = jnp.concatenate(v_list, axis=0)
        vv = jnp.concatenate([vv, jnp.ones(vv.shape, BF16)], axis=1)
        m = jnp.maximum(jnp.max(s, axis=1, keepdims=True), sink)
        ps = [jnp.exp2(s[:, j * LANE:(j + 1) * LANE] - m).astype(BF16) for j in range(s.shape[1] // LANE)]
        pv = _dot(jnp.concatenate(ps, axis=1), vv)
        return pv[:, 0:LANE] / (pv[:, LANE:2 * LANE] + jnp.exp2(sink - m))

    groups = [(g, e) for g in range(D_KV_HEADS) for e in range(2)]
    outs = []
    pending = scores(*groups[0])
    for gi in range(len(groups)):
        nxt = scores(*groups[gi + 1]) if gi + 1 < len(groups) else None
        outs.append(softmax_pv(*pending))
        pending = nxt
    for g in range(D_KV_HEADS):
        for pi in range(n_pairs):
            ps = g * n_pairs + pi
            out = jnp.where(lo, outs[2 * g][pi * tq:(pi + 1) * tq], outs[2 * g + 1][pi * tq:(pi + 1) * tq])
            o_ref[0, :, ps * LANE:(ps + 1) * LANE] = out.astype(o_ref.dtype)


def _swa_call(q, kc, vc, k, v, sinks_log2, tq, name):
    bsz, t_q, width = q.shape
    t_c = kc.shape[1]
    latent = k is not None
    n_blocks = t_q // tq
    args = [sinks_log2, q, kc, vc]
    in_specs = [
        pl.BlockSpec(memory_space=pltpu.SMEM),
        pl.BlockSpec((1, tq, width), lambda b, i: (b, i, 0)),
        pl.BlockSpec((1, t_c, 2 * LANE), lambda b, i: (b, 0, 0)),
        pl.BlockSpec((1, t_c, 2 * LANE), lambda b, i: (b, 0, 0)),
    ]
    if latent:
        assert tq == WINDOW
        prev_map = lambda b, i: (b, jnp.maximum(i - 1, 0), 0)
        cur_map = lambda b, i: (b, i, 0)
        next_map = lambda b, i: (b, jnp.minimum(i + 1, n_blocks - 1), 0)
        for arr in (k, v):
            for mp in (prev_map, cur_map, next_map):
                args.append(arr)
                in_specs.append(pl.BlockSpec((1, tq, 2 * LANE), mp))
    return pl.pallas_call(
        functools.partial(_swa_kernel, latent=latent, tq=tq, n_blocks=n_blocks),
        grid=(bsz, n_blocks),
        in_specs=in_specs,
        out_specs=pl.BlockSpec((1, tq, width), lambda b, i: (b, i, 0)),
        out_shape=jax.ShapeDtypeStruct((bsz, t_q, width), BF16),
        compiler_params=_cparams(2),
        name=name,
    )(*args)


def _out_kernel(x_ref, o_ref, g_ref, mod_ref, w_ref, lng_ref, lnb_ref, y_ref, *, alpha):
    d = x_ref.shape[-1]
    u = (o_ref[0].astype(F32) * g_ref[0].astype(F32)).astype(BF16)
    br = _dot(u, w_ref[...])
    gate = mod_ref[0, :, 2 * d:3 * d]
    z = alpha * x_ref[0] + gate * br
    zc = z - jnp.mean(z, axis=-1, keepdims=True)
    var = jnp.mean(zc * zc, axis=-1, keepdims=True)
    y_ref[0] = zc * lax.rsqrt(var + NORM_EPS) * lng_ref[...] + lnb_ref[...]


def _out_call(x, o, g, mod, w, ln_g, ln_b, alpha, tm, name):
    bsz, t_len, d = x.shape
    width = o.shape[-1]
    return pl.pallas_call(
        functools.partial(_out_kernel, alpha=alpha),
        grid=(bsz, t_len // tm),
        in_specs=[
            pl.BlockSpec((1, tm, d), lambda b, i: (b, i, 0)),
            pl.BlockSpec((1, tm, width), lambda b, i: (b, i, 0)),
            pl.BlockSpec((1, tm, width), lambda b, i: (b, i, 0)),
            pl.BlockSpec((1, 1, mod.shape[-1]), lambda b, i: (b, 0, 0)),
            pl.BlockSpec(w.shape, lambda b, i: (0, 0)),
            pl.BlockSpec((1, d), lambda b, i: (0, 0)),
            pl.BlockSpec((1, d), lambda b, i: (0, 0)),
        ],
        out_specs=pl.BlockSpec((1, tm, d), lambda b, i: (b, i, 0)),
        out_shape=jax.ShapeDtypeStruct((bsz, t_len, d), F32),
        compiler_params=_cparams(2),
        name=name,
    )(x, o, g, mod, w, ln_g.reshape(1, d), ln_b.reshape(1, d))


ATTN_ROWS = 1024
ATTN_TK = 1024
ATTN_UNROLL = 16
ATTN_PARTS = 4


def _tiles(t_len, n_row_groups=1):
    tm = min(512, t_len)
    tq = min(ATTN_ROWS // n_row_groups, t_len)
    tk = min(ATTN_TK, t_len)
    return tm, tq, tk


def _mla_layer(x, ctx, mod, mod_c, need_ctx, w_in, g_qa, w_qb, g_kva, w_kvb):
    t_len, t_c = x.shape[1], ctx.shape[1]
    tm, tq, tk = _tiles(t_len, 2)
    d = w_in.shape[0]
    o0, o1, o2 = A_Q_LORA, A_Q_LORA + A_KV_LORA, A_Q_LORA + A_KV_LORA + A_ROPE
    kpe_pad = jnp.concatenate([jnp.zeros((d, A_NOPE), F32), w_in[:, o1:o2],
                               jnp.zeros((d, LANE - A_NOPE - A_ROPE), F32)], axis=1)
    w_in_r = jnp.concatenate([w_in[:, :o1], kpe_pad, w_in[:, o2:]], axis=1).astype(BF16)
    w_qb_r = jnp.pad(w_qb.reshape(A_Q_LORA, A_HEADS, A_NOPE + A_ROPE),
                     ((0, 0), (0, 0), (0, LANE - A_NOPE - A_ROPE))).reshape(A_Q_LORA, A_HEADS * LANE).astype(BF16)
    kvb = w_kvb.reshape(A_KV_LORA, A_HEADS, A_NOPE + A_V)
    k_pad = jnp.pad(kvb[:, :, :A_NOPE], ((0, 0), (0, 0), (0, LANE - A_NOPE))).reshape(A_KV_LORA, A_HEADS * LANE)
    w_kvb_r = jnp.concatenate([k_pad, kvb[:, :, A_NOPE:].reshape(A_KV_LORA, A_HEADS * A_V)], axis=1).astype(BF16)
    q_scale = (A_NOPE + A_ROPE) ** -0.5 * LOG2E
    gq, gk = g_qa.reshape(1, -1), g_kva.reshape(1, -1)
    tabs = _rope_tables(t_len, A_ROPE, "mla")
    q, k, v, g = _mla_proj_call(x, mod, tabs, w_in_r, gq, w_qb_r, gk, w_kvb_r, q_scale, tm, "mla_proj")
    q_c, k_c, v_c, g_c = _mla_proj_call(ctx, mod_c, None, w_in_r, gq, w_qb_r, gk, w_kvb_r, q_scale, t_c, "mla_proj_ctx")
    rows = ((0, None, 0), (1, None, 1))
    maps = dict(n_groups=A_HEADS // 2, q_map=lambda gi, j: 2 * gi + j, k_map=lambda gi, j: 2 * gi + j,
                v_map=lambda gi: gi, rows=rows, out_width=A_HEADS * A_V, lambda_init=0.0)
    o = _dense_attn_call("pair", q, k_c, v_c, k, v, (), tq=tq, tk=tk, name="mla_attn", **maps)
    o_c = _dense_attn_call("pair", q_c, k_c, v_c, None, None, (), tq=t_c, tk=tk, name="mla_attn_ctx", **maps) \
        if need_ctx else None
    return o, g, o_c, g_c


def _diff_layer(x, ctx, mod, mod_c, need_ctx, layer_idx, w_in, lam, g_sub):
    t_len, t_c = x.shape[1], ctx.shape[1]
    tm, tq, tk = _tiles(t_len, 2)
    n = 2 * B_HEADS * B_HEAD
    q_scale = B_HEAD ** -0.5 * LOG2E
    segs = ((0, 0, n, "rope", q_scale, None), (1, n, n, "rope", 1.0, None),
            (3, 3 * n, D_MODEL, "silu", 1.0, None), (2, 2 * n, n, "plain", 1.0, None))
    widths = (n, n, n, D_MODEL)
    wb = w_in.astype(BF16)
    tabs = _rope_tables(t_len, B_HEAD, "unit")
    q, k, v, g = _proj_call(x, mod, tabs, wb, None, segs, widths, B_HEAD // 2, tm, "diff_proj")
    q_c, k_c, v_c, g_c = _proj_call(ctx, mod_c, None, wb, None, segs, widths, B_HEAD // 2, t_c, "diff_proj_ctx")
    lambda_init = 0.8 - 0.6 * math.exp(-0.3 * layer_idx)
    rows = ((0, "lo", 0), (0, "hi", 0))
    extras = (lam.astype(F32), g_sub.reshape(1, -1).astype(F32))
    maps = dict(n_groups=B_HEADS, q_map=lambda gi, j: gi, k_map=lambda gi, j: gi, v_map=lambda gi: gi,
                rows=rows, out_width=n, lambda_init=lambda_init)
    o = _dense_attn_call("diff", q, k_c, v_c, k, v, extras, tq=tq, tk=tk, name="diff_attn", **maps)
    o_c = _dense_attn_call("diff", q_c, k_c, v_c, None, None, extras, tq=t_c, tk=tk, name="diff_attn_ctx", **maps) \
        if need_ctx else None
    return o, g, o_c, g_c


def _gqa_layer(x, ctx, mod, mod_c, need_ctx, w_in, g_q, g_k):
    t_len, t_c = x.shape[1], ctx.shape[1]
    tm, tq, tk = _tiles(t_len, C_HEADS // C_KV_HEADS)
    nq, nkv = C_HEADS * C_HEAD, C_KV_HEADS * C_HEAD
    q_scale = C_HEAD ** -0.5 * LOG2E
    segs = ((0, 0, nq, "rope", q_scale, 0), (1, nq, nkv, "rope", 1.0, 1),
            (3, nq + 2 * nkv, D_MODEL, "silu", 1.0, None), (2, nq + nkv, nkv, "plain", 1.0, None))
    widths = (nq, nkv, nkv, D_MODEL)
    wb = w_in.astype(BF16)
    gains = jnp.stack([g_q, g_k]).astype(F32)
    tabs = _rope_tables(t_len, C_HEAD, "unit")
    q, k, v, g = _proj_call(x, mod, tabs, wb, gains, segs, widths, C_HEAD // 2, tm, "gqa_proj")
    q_c, k_c, v_c, g_c = _proj_call(ctx, mod_c, None, wb, gains, segs, widths, C_HEAD // 2, t_c, "gqa_proj_ctx")
    per = C_HEADS // C_KV_HEADS
    rows = tuple((j, None, 0) for j in range(per))
    maps = dict(n_groups=C_KV_HEADS, q_map=lambda gi, j: per * gi + j, k_map=lambda gi, j: gi, v_map=lambda gi: gi,
                rows=rows, out_width=nq, lambda_init=0.0)
    o = _dense_attn_call("stack", q, k_c, v_c, k, v, (), tq=tq, tk=tk, name="gqa_attn", **maps)
    o_c = _dense_attn_call("stack", q_c, k_c, v_c, None, None, (), tq=t_c, tk=tk, name="gqa_attn_ctx", **maps) \
        if need_ctx else None
    return o, g, o_c, g_c


def _swa_layer(x, ctx, mod, mod_c, need_ctx, w_in, sinks):
    t_len, t_c = x.shape[1], ctx.shape[1]
    tm, _, _ = _tiles(t_len)
    nq, nkv = D_HEADS * D_HEAD, D_KV_HEADS * D_HEAD
    wk = w_in[:, nq:nq + nkv]
    wv = w_in[:, nq + nkv:nq + 2 * nkv]
    swap = lambda w: jnp.concatenate([w, w[:, D_HEAD:], w[:, :D_HEAD]], axis=1)
    wb = jnp.concatenate([w_in[:, :nq], swap(wk), swap(wv), w_in[:, nq + 2 * nkv:]], axis=1).astype(BF16)
    q_scale = D_HEAD ** -0.5 * LOG2E
    segs = ((0, 0, nq, "rope", q_scale, None), (1, nq, 2 * nkv, "rope", 1.0, None),
            (3, nq + 4 * nkv, D_MODEL, "silu", 1.0, None), (2, nq + 2 * nkv, 2 * nkv, "plain", 1.0, None))
    widths = (nq, 2 * nkv, 2 * nkv, D_MODEL)
    tabs = _rope_tables(t_len, D_HEAD, "unit")
    q, k, v, g = _proj_call(x, mod, tabs, wb, None, segs, widths, D_HEAD // 2, tm, "swa_proj")
    q_c, k_c, v_c, g_c = _proj_call(ctx, mod_c, None, wb, None, segs, widths, D_HEAD // 2, t_c, "swa_proj_ctx")
    sinks_log2 = sinks.astype(F32) * LOG2E
    o = _swa_call(q, k_c, v_c, k, v, sinks_log2, WINDOW, "swa_attn")
    o_c = _swa_call(q_c, k_c, v_c, None, None, sinks_log2, t_c, "swa_attn_ctx") if need_ctx else None
    return o, g, o_c, g_c


def kernel(x, c, ctx, c_ctx, ada_w, ada_b, out_w, ln_g, ln_b, mla_w_in, mla_g_qa, mla_w_qb, mla_g_kva, mla_w_kvb,
           diff_w_in, diff_lambda, diff_g_sub, gqa_w_in, gqa_g_q, gqa_g_k, swa_w_in, swa_sink):
    depth = ada_w.shape[0]
    bsz, t_len, d = x.shape
    t_c = ctx.shape[1]
    alpha = (2 * depth) ** 0.25
    n_rows = 8
    assert bsz + 1 <= n_rows
    cvec = jnp.concatenate([c, c_ctx[None, :], jnp.zeros((n_rows - bsz - 1, d), F32)], axis=0)
    mods = _adaln_call(cvec, ada_w, ada_b)
    tm = _tiles(t_len)[0]
    for i in range(depth):
        kind, j = i % 4, i // 4
        need_ctx = i < depth - 1
        mod = mods[i, 0:bsz][:, None, :]
        mod_c = jnp.broadcast_to(mods[i, bsz][None, None, :], (bsz, 1, 3 * d))
        if kind == 0:
            o, g, o_c, g_c = _mla_layer(x, ctx, mod, mod_c, need_ctx, mla_w_in[j], mla_g_qa[j], mla_w_qb[j],
                                        mla_g_kva[j], mla_w_kvb[j])
        elif kind == 1:
            o, g, o_c, g_c = _diff_layer(x, ctx, mod, mod_c, need_ctx, i, diff_w_in[j], diff_lambda[j], diff_g_sub[j])
        elif kind == 2:
            o, g, o_c, g_c = _gqa_layer(x, ctx, mod, mod_c, need_ctx, gqa_w_in[j], gqa_g_q[j], gqa_g_k[j])
        else:
            o, g, o_c, g_c = _swa_layer(x, ctx, mod, mod_c, need_ctx, swa_w_in[j], swa_sink[j])
        wo = out_w[i].astype(BF16)
        x = _out_call(x, o, g, mod, wo, ln_g[i], ln_b[i], alpha, tm, "out_proj")
        if need_ctx:
            ctx = _out_call(ctx, o_c, g_c, mod_c, wo, ln_g[i], ln_b[i], alpha, t_c, "out_proj_ctx")
    return x
```

```python
import functools
import math

import jax
import jax.numpy as jnp
from jax import lax
from jax.experimental import pallas as pl
from jax.experimental.pallas import tpu as pltpu

D_MODEL = 1024
GRID_W = 64
WINDOW = 128
ROPE_THETA = 10000.0
NORM_EPS = 1e-6
A_HEADS, A_Q_LORA, A_KV_LORA, A_NOPE, A_ROPE, A_V = 16, 256, 128, 64, 32, 64
B_HEADS, B_HEAD = 8, 64
C_HEADS, C_KV_HEADS, C_HEAD = 8, 2, 128
D_HEADS, D_KV_HEADS, D_HEAD = 16, 2, 64

LANE = 128
HALF = LANE // 2
LOG2E = math.log2(math.e)
NEG_BIG = -1e30
VMEM_LIMIT = 56 * 1024 * 1024
BF16 = jnp.bfloat16
F32 = jnp.float32


def _cparams(n_grid):
    return pltpu.CompilerParams(dimension_semantics=("arbitrary",) * n_grid, vmem_limit_bytes=VMEM_LIMIT)


def _silu(x):
    return x / (1.0 + jnp.exp(-x))


def _dot(a, b):
    return jnp.dot(a, b, preferred_element_type=F32)


def _dot_nt(a, b):
    return lax.dot_general(a, b, (((1,), (1,)), ((), ())), preferred_element_type=F32)


def _adaln_kernel(c_ref, w_ref, b_ref, o_ref):
    sc = _silu(c_ref[...])
    o_ref[0] = jnp.dot(sc, w_ref[0], preferred_element_type=F32, precision=lax.Precision.HIGHEST) + b_ref[0]


def _adaln_call(cvec, ada_w, ada_b):
    depth, d, n = ada_w.shape
    rows = cvec.shape[0]
    tn = 1024
    return pl.pallas_call(
        _adaln_kernel,
        grid=(depth, n // tn),
        in_specs=[
            pl.BlockSpec((rows, d), lambda i, j: (0, 0)),
            pl.BlockSpec((1, d, tn), lambda i, j: (i, 0, j)),
            pl.BlockSpec((1, 1, tn), lambda i, j: (i, 0, j)),
        ],
        out_specs=pl.BlockSpec((1, rows, tn), lambda i, j: (i, 0, j)),
        out_shape=jax.ShapeDtypeStruct((depth, rows, n), F32),
        compiler_params=_cparams(2),
        name="adaln",
    )(cvec, ada_w, ada_b.reshape(depth, 1, n))


def _rope_tables(t_len, rot_dim, layout):
    rows = t_len // GRID_W
    row = jnp.repeat(jnp.arange(rows, dtype=F32), GRID_W)
    col = jnp.tile(jnp.arange(GRID_W, dtype=F32), rows)
    n_freq = rot_dim // 4
    inv_freq = ROPE_THETA ** (-jnp.arange(n_freq, dtype=F32) / n_freq)
    ang = jnp.concatenate([row[:, None] * inv_freq, col[:, None] * inv_freq], axis=-1)
    cos, sin = jnp.cos(ang), jnp.sin(ang)
    zero = jnp.zeros_like(sin)
    if layout == "mla":
        one = jnp.ones((t_len, A_NOPE), F32)
        pad = jnp.zeros((t_len, LANE - A_NOPE - A_ROPE), F32)
        c = jnp.concatenate([one, cos, cos, pad], axis=-1)
        s_lo = jnp.concatenate([0 * one, -sin, zero, pad], axis=-1)
        s_hi = jnp.concatenate([0 * one, zero, sin, pad], axis=-1)
    else:
        reps = LANE // rot_dim
        c = jnp.tile(jnp.concatenate([cos, cos], axis=-1), (1, reps))
        s_lo = jnp.tile(jnp.concatenate([-sin, zero], axis=-1), (1, reps))
        s_hi = jnp.tile(jnp.concatenate([zero, sin], axis=-1), (1, reps))
    return c, s_lo, s_hi


def _rope_slab(x, c, s_lo, s_hi, half):
    if half == HALF:
        return x * c + pltpu.roll(x, HALF, 1) * (s_lo + s_hi)
    return x * c + pltpu.roll(x, LANE - half, 1) * s_lo + pltpu.roll(x, half, 1) * s_hi


def _modulate(x_ref, mod_ref):
    d = x_ref.shape[-1]
    shift = mod_ref[0, :, 0:d]
    scale = mod_ref[0, :, d:2 * d]
    return (x_ref[0] * (1.0 + scale) + shift).astype(BF16)


def _rms(x, g):
    return x * lax.rsqrt(jnp.mean(x * x, axis=-1, keepdims=True) + NORM_EPS) * g


PROJ_CHUNK = 512


def _proj_kernel(*refs, segs, rope_half, has_rope, has_norm):
    it = iter(refs)
    x_ref, mod_ref = next(it), next(it)
    if has_rope:
        c = next(it)[...]
        s_lo = next(it)[...]
        s_hi = next(it)[...]
    w_ref = next(it)
    gn_ref = next(it) if has_norm else None
    out_refs = list(it)
    hb = _modulate(x_ref, mod_ref)
    items = [(seg, cc, min(PROJ_CHUNK, seg[2] - cc)) for seg in segs for cc in range(0, seg[2], PROJ_CHUNK)]

    def matmul(item):
        (_, col0, _, _, _, _), cc, cw = item
        return _dot(hb, w_ref[:, col0 + cc:col0 + cc + cw])

    def finish(item, t):
        (oi, _, _, kind, q_scale, norm_row), cc, cw = item
        if kind == "plain":
            y = t
        else:
            slabs = []
            for s in range(cw // LANE):
                xs = t[:, s * LANE:(s + 1) * LANE]
                if norm_row is not None:
                    xs = _rms(xs, gn_ref[norm_row:norm_row + 1, :])
                if has_rope:
                    xs = _rope_slab(xs, c, s_lo, s_hi, rope_half)
                if q_scale != 1.0:
                    xs = xs * q_scale
                slabs.append(xs)
            y = jnp.concatenate(slabs, axis=1) if len(slabs) > 1 else slabs[0]
        out_refs[oi][0, :, cc:cc + cw] = y.astype(BF16)

    t_cur = matmul(items[0])
    for n, item in enumerate(items):
        t_next = matmul(items[n + 1]) if n + 1 < len(items) else None
        finish(item, t_cur)
        t_cur = t_next


def _proj_call(x, mod, tabs, w, gains, segs, out_widths, rope_half, tm, name):
    bsz, t_len, d = x.shape
    has_rope = tabs is not None
    has_norm = gains is not None
    args = [x, mod]
    in_specs = [
        pl.BlockSpec((1, tm, d), lambda b, i: (b, i, 0)),
        pl.BlockSpec((1, 1, mod.shape[-1]), lambda b, i: (b, 0, 0)),
    ]
    if has_rope:
        for tab in tabs:
            args.append(tab)
            in_specs.append(pl.BlockSpec((tm, LANE), lambda b, i: (i, 0)))
    args.append(w)
    in_specs.append(pl.BlockSpec(w.shape, lambda b, i: (0, 0)))
    if has_norm:
        args.append(gains)
        in_specs.append(pl.BlockSpec(gains.shape, lambda b, i: (0, 0)))
    return pl.pallas_call(
        functools.partial(_proj_kernel, segs=segs, rope_half=rope_half, has_rope=has_rope, has_norm=has_norm),
        grid=(bsz, t_len // tm),
        in_specs=in_specs,
        out_specs=[pl.BlockSpec((1, tm, ow), lambda b, i: (b, i, 0)) for ow in out_widths],
        out_shape=[jax.ShapeDtypeStruct((bsz, t_len, ow), BF16) for ow in out_widths],
        compiler_params=_cparams(2),
        name=name,
    )(*args)


def _mla_proj_kernel(*refs, has_rope, q_scale):
    it = iter(refs)
    x_ref, mod_ref = next(it), next(it)
    if has_rope:
        c = next(it)[...]
        s_lo = next(it)[...]
        s_hi = next(it)[...]
    w_in_ref, gqa_ref, w_qb_ref, gkva_ref, w_kvb_ref = next(it), next(it), next(it), next(it), next(it)
    q_ref, k_ref, v_ref = next(it), next(it), next(it)
    half = A_ROPE // 2
    n_lat = A_Q_LORA + A_KV_LORA + LANE
    hb = _modulate(x_ref, mod_ref)
    t = _dot(hb, w_in_ref[:, 0:n_lat])
    qn = _rms(t[:, 0:A_Q_LORA], gqa_ref[...]).astype(BF16)
    kn = _rms(t[:, A_Q_LORA:A_Q_LORA + A_KV_LORA], gkva_ref[...]).astype(BF16)
    kpe = t[:, A_Q_LORA + A_KV_LORA:n_lat]
    if has_rope:
        kpe = _rope_slab(kpe, c, s_lo, s_hi, half)
    n_qk = A_HEADS * LANE
    for cc in range(0, n_qk, PROJ_CHUNK):
        tq = _dot(qn, w_qb_ref[:, cc:cc + PROJ_CHUNK])
        tk = _dot(kn, w_kvb_ref[:, cc:cc + PROJ_CHUNK])
        qs, ks = [], []
        for s in range(PROJ_CHUNK // LANE):
            xq = tq[:, s * LANE:(s + 1) * LANE]
            if has_rope:
                xq = _rope_slab(xq, c, s_lo, s_hi, half)
            qs.append(xq * q_scale)
            ks.append(tk[:, s * LANE:(s + 1) * LANE] + kpe)
        q_ref[0, :, cc:cc + PROJ_CHUNK] = jnp.concatenate(qs, axis=1).astype(BF16)
        k_ref[0, :, cc:cc + PROJ_CHUNK] = jnp.concatenate(ks, axis=1).astype(BF16)
    n_v = A_HEADS * A_V
    for cc in range(0, n_v, PROJ_CHUNK):
        v_ref[0, :, cc:cc + PROJ_CHUNK] = _dot(kn, w_kvb_ref[:, n_qk + cc:n_qk + cc + PROJ_CHUNK]).astype(BF16)


def _mla_proj_call(x, mod, tabs, w_in, g_qa, w_qb, g_kva, w_kvb, q_scale, tm, name):
    bsz, t_len, d = x.shape
    has_rope = tabs is not None
    args = [x, mod]
    in_specs = [
        pl.BlockSpec((1, tm, d), lambda b, i: (b, i, 0)),
        pl.BlockSpec((1, 1, mod.shape[-1]), lambda b, i: (b, 0, 0)),
    ]
    if has_rope:
        for tab in tabs:
            args.append(tab)
            in_specs.append(pl.BlockSpec((tm, LANE), lambda b, i: (i, 0)))
    for a in (w_in, g_qa, w_qb, g_kva, w_kvb):
        args.append(a)
        in_specs.append(pl.BlockSpec(a.shape, lambda b, i: (0, 0)))
    out_widths = (A_HEADS * LANE, A_HEADS * LANE, A_HEADS * A_V)
    return pl.pallas_call(
        functools.partial(_mla_proj_kernel, has_rope=has_rope, q_scale=q_scale),
        grid=(bsz, t_len // tm),
        in_specs=in_specs,
        out_specs=[pl.BlockSpec((1, tm, ow), lambda b, i: (b, i, 0)) for ow in out_widths],
        out_shape=[jax.ShapeDtypeStruct((bsz, t_len, ow), BF16) for ow in out_widths],
        compiler_params=_cparams(2),
        name=name,
    )(*args)


def _lane_lo(shape):
    return lax.broadcasted_iota(jnp.int32, shape, len(shape) - 1) < HALF


def _dense_attn_kernel(*refs, mode, n_q, n_k, rows, tq, tk, n_chunks, n_parts, unroll, lambda_init):
    it = iter(refs)
    q_refs = [next(it) for _ in range(n_q)]
    kc_refs = [next(it) for _ in range(n_k)]
    vc_ref = next(it)
    if n_chunks:
        k_refs = [next(it) for _ in range(n_k)]
        v_ref = next(it)
    if mode == "diff":
        lam_ref, gsub_ref = next(it), next(it)
    o_ref = next(it)
    qs_scr, m_scr, acc_scr = next(it), next(it), next(it)
    buf_a, buf_b = next(it), next(it)

    n_rows = len(rows)
    for r, (qi, mask, _) in enumerate(rows):
        q = q_refs[qi][0]
        if mask is not None:
            lo = _lane_lo(q.shape)
            q = jnp.where(lo if mask == "lo" else jnp.logical_not(lo), q, jnp.zeros_like(q))
        qs_scr[r * tq:(r + 1) * tq, :] = q
    spans = []
    for ks in range(n_k):
        idx = [r for r, row in enumerate(rows) if row[2] == ks]
        spans.append((idx[0] * tq, (idx[-1] + 1) * tq))
    blocks_per_span = max(1, n_parts // n_k)
    row_blocks = []
    for (r0, r1) in spans:
        nb = (r1 - r0) // blocks_per_span
        row_blocks += [(r0 + i * nb, r0 + (i + 1) * nb) for i in range(blocks_per_span)]

    def qk_part(k_load, s_ref, c0, c1):
        for ks, (r0, r1) in enumerate(spans):
            s_ref[r0:r1, c0:c1] = _dot_nt(qs_scr[r0:r1, :], k_load(ks, c0, c1))

    def sm_part(v_ext, s_ref, a, b):
        n_cols = v_ext.shape[0] // LANE
        mp = s_ref[a:b, 0:LANE]
        for j in range(1, n_cols):
            mp = jnp.maximum(mp, s_ref[a:b, j * LANE:(j + 1) * LANE])
        m_prev = m_scr[a:b, :]
        m_new = jnp.maximum(m_prev, jnp.max(mp, axis=1, keepdims=True))
        ps = [jnp.exp2(s_ref[a:b, j * LANE:(j + 1) * LANE] - m_new).astype(BF16) for j in range(n_cols)]
        p = jnp.concatenate(ps, axis=1) if n_cols > 1 else ps[0]
        pv = _dot(p, v_ext)
        alpha = jnp.exp2(m_prev - m_new)
        acc_scr[a:b, :] = jnp.concatenate([alpha, alpha], axis=1) * acc_scr[a:b, :] + pv
        m_scr[a:b, :] = m_new

    def ext(v):
        return jnp.concatenate([v, jnp.ones(v.shape, BF16)], axis=1)

    def k_loader(off):
        return lambda ks, c0, c1: k_refs[ks][0, pl.ds(off + c0, c1 - c0), :]

    def v_chunk(off):
        return ext(v_ref[0, pl.ds(off, tk), :])

    def pipelined(v_ext, s_cur, k_next, s_next, next_width):
        kw = max(LANE, next_width // n_parts)
        for pi, (a, b) in enumerate(row_blocks):
            sm_part(v_ext, s_cur, a, b)
            if pi * kw < next_width:
                qk_part(k_next, s_next, pi * kw, (pi + 1) * kw)

    bufs = (buf_a, buf_b)
    t_c = vc_ref.shape[1]
    ctx_loader = lambda ks, c0, c1: kc_refs[ks][0, c0:c1, :]
    if n_chunks:
        qk_part(k_loader(0), bufs[0], 0, tk)
    else:
        qk_part(ctx_loader, bufs[0], 0, t_c)
    m_scr[...] = jnp.full(m_scr.shape, NEG_BIG, F32)
    acc_scr[...] = jnp.zeros(acc_scr.shape, F32)
    if n_chunks:
        n_trips = (n_chunks - 1) // unroll

        def step(ti, carry):
            base = ti * (unroll * tk)
            for u in range(unroll):
                off = pl.multiple_of(base + u * tk, tk)
                nxt = pl.multiple_of(base + (u + 1) * tk, tk)
                pipelined(v_chunk(off), bufs[u % 2], k_loader(nxt), bufs[(u + 1) % 2], tk)
            return carry
        if n_trips:
            lax.fori_loop(0, n_trips, step, 0)
        for c in range(unroll * n_trips, n_chunks):
            if c == n_chunks - 1:
                pipelined(v_chunk(c * tk), bufs[c % 2], ctx_loader, bufs[(c + 1) % 2], t_c)
            else:
                pipelined(v_chunk(c * tk), bufs[c % 2], k_loader((c + 1) * tk), bufs[(c + 1) % 2], tk)
    vc_ext = ext(vc_ref[0])
    for (a, b) in row_blocks:
        sm_part(vc_ext, bufs[n_chunks % 2], a, b)

    acc = acc_scr[...]
    o = acc[:, 0:LANE] / acc[:, LANE:2 * LANE]
    if mode == "pair":
        out = jnp.where(_lane_lo((tq, LANE)), o[0:tq], o[tq:2 * tq])
    elif mode == "diff":
        lam = lam_ref[...]
        a = jnp.sum(lam[0:1, :] * lam[1:2, :], axis=1, keepdims=True)
        b = jnp.sum(lam[2:3, :] * lam[3:4, :], axis=1, keepdims=True)
        lam_full = jnp.exp(a) - jnp.exp(b) + lambda_init
        dlt = o[0:tq] - lam_full * o[tq:2 * tq]
        out = _rms(dlt, gsub_ref[...]) * (1.0 - lambda_init)
    else:
        out = jnp.concatenate([o[r * tq:(r + 1) * tq] for r in range(n_rows)], axis=1)
    o_ref[0] = out.astype(o_ref.dtype)


def _dense_attn_call(mode, q, kc, vc, k, v, extras, n_groups, q_map, k_map, v_map, rows, out_width, tq, tk,
                     lambda_init, name):
    bsz, t_q, _ = q.shape
    t_c = kc.shape[1]
    n_q = 1 + max(r[0] for r in rows)
    n_k = 1 + max(r[2] for r in rows)
    n_rows = len(rows)
    has_latent = k is not None
    n_chunks = (k.shape[1] // tk) if has_latent else 0
    args, in_specs = [], []
    for j in range(n_q):
        args.append(q)
        in_specs.append(pl.BlockSpec((1, tq, LANE), lambda b, g, i, j=j: (b, i, q_map(g, j))))
    for j in range(n_k):
        args.append(kc)
        in_specs.append(pl.BlockSpec((1, t_c, LANE), lambda b, g, i, j=j: (b, 0, k_map(g, j))))
    args.append(vc)
    in_specs.append(pl.BlockSpec((1, t_c, LANE), lambda b, g, i: (b, 0, v_map(g))))
    if has_latent:
        t_k = k.shape[1]
        for j in range(n_k):
            args.append(k)
            in_specs.append(pl.BlockSpec((1, t_k, LANE), lambda b, g, i, j=j: (b, 0, k_map(g, j))))
        args.append(v)
        in_specs.append(pl.BlockSpec((1, t_k, LANE), lambda b, g, i: (b, 0, v_map(g))))
    for e in extras:
        args.append(e)
        in_specs.append(pl.BlockSpec(e.shape, lambda b, g, i: (0, 0)))
    ow = out_width // n_groups
    s_width = max(tk, t_c) if has_latent else t_c
    return pl.pallas_call(
        functools.partial(_dense_attn_kernel, mode=mode, n_q=n_q, n_k=n_k, rows=rows, tq=tq, tk=tk,
                          n_chunks=n_chunks, n_parts=ATTN_PARTS, unroll=ATTN_UNROLL, lambda_init=lambda_init),
        grid=(bsz, n_groups, t_q // tq),
        in_specs=in_specs,
        out_specs=pl.BlockSpec((1, tq, ow), lambda b, g, i: (b, i, g)),
        out_shape=jax.ShapeDtypeStruct((bsz, t_q, out_width), BF16),
        scratch_shapes=[
            pltpu.VMEM((n_rows * tq, LANE), BF16),
            pltpu.VMEM((n_rows * tq, LANE), F32),
            pltpu.VMEM((n_rows * tq, 2 * LANE), F32),
            pltpu.VMEM((n_rows * tq, s_width), F32),
            pltpu.VMEM((n_rows * tq, s_width), F32),
        ],
        compiler_params=_cparams(3),
        name=name,
    )(*args)


def _swa_kernel(*refs, latent, tq, n_blocks):
    it = iter(refs)
    sink_ref = next(it)
    q_ref = next(it)
    kc_ref, vc_ref = next(it), next(it)
    if latent:
        kp_ref, k0_ref, kn_ref = next(it), next(it), next(it)
        vp_ref, v0_ref, vn_ref = next(it), next(it), next(it)
    o_ref = next(it)
    i = pl.program_id(1)
    per_g = D_HEADS // D_KV_HEADS
    n_pairs = per_g // 2
    lo = _lane_lo((tq, LANE))
    if latent:
        r = lax.broadcasted_iota(jnp.int32, (tq, tq), 0)
        cidx = lax.broadcasted_iota(jnp.int32, (tq, tq), 1)
        ok_prev = jnp.logical_and(cidx >= r, i > 0)
        ok_next = jnp.logical_and(cidx <= r, i < n_blocks - 1)
        ok_pc = jnp.concatenate([ok_prev, jnp.full((tq, tq), True)], axis=1)
        ok_pc = jnp.concatenate([ok_pc] * n_pairs, axis=0)
        ok_next = jnp.concatenate([ok_next] * n_pairs, axis=0)

    def scores(g, e):
        slab = g ^ e
        sl = slice(slab * LANE, (slab + 1) * LANE)
        qrows, sinks = [], []
        for pi in range(n_pairs):
            ps = g * n_pairs + pi
            qv = q_ref[0, :, ps * LANE:(ps + 1) * LANE]
            qrows.append(jnp.where(lo if e == 0 else jnp.logical_not(lo), qv, jnp.zeros_like(qv)))
            sinks.append(jnp.full((tq, LANE), sink_ref[g * per_g + 2 * pi + e], F32))
        qst = jnp.concatenate(qrows, axis=0)
        sink = jnp.concatenate(sinks, axis=0)
        s_list = [_dot_nt(qst, kc_ref[0, :, sl])]
        if latent:
            k_pc = jnp.concatenate([kp_ref[0, :, sl], k0_ref[0, :, sl]], axis=0)
            s_list.append(jnp.where(ok_pc, _dot_nt(qst, k_pc), NEG_BIG))
            s_list.append(jnp.where(ok_next, _dot_nt(qst, kn_ref[0, :, sl]), NEG_BIG))
        return jnp.concatenate(s_list, axis=1), sink, sl

    def softmax_pv(s, sink, sl):
        v_list = [vc_ref[0, :, sl]]
        if latent:
            v_list += [vp_ref[0, :, sl], v0_ref[0, :, sl], vn_ref[0, :, sl]]
        vv = jnp.concatenate(v_list, axis=0)
        vv = jnp.concatenate([vv, jnp.ones(vv.shape, BF16)], axis=1)
        m = jnp.maximum(jnp.max(s, axis=1, keepdims=True), sink)
        ps = [jnp.exp2(s[:, j * LANE:(j + 1) * LANE] - m).astype(BF16) for j in range(s.shape[1] // LANE)]
        pv = _dot(jnp.concatenate(ps, axis=1), vv)
        return pv[:, 0:LANE] / (pv[:, LANE:2 * LANE] + jnp.exp2(sink - m))

    groups = [(g, e) for g in range(D_KV_HEADS) for e in range(2)]
    outs = []
    pending = scores(*groups[0])
    for gi in range(len(groups)):
        nxt = scores(*groups[gi + 1]) if gi + 1 < len(groups) else None
        outs.append(softmax_pv(*pending))
        pending = nxt
    for g in range(D_KV_HEADS):
        for pi in range(n_pairs):
            ps = g * n_pairs + pi
            out = jnp.where(lo, outs[2 * g][pi * tq:(pi + 1) * tq], outs[2 * g + 1][pi * tq:(pi + 1) * tq])
            o_ref[0, :, ps * LANE:(ps + 1) * LANE] = out.astype(o_ref.dtype)


def _swa_call(q, kc, vc, k, v, sinks_log2, tq, name):
    bsz, t_q, width = q.shape
    t_c = kc.shape[1]
    latent = k is not None
    n_blocks = t_q // tq
    args = [sinks_log2, q, kc, vc]
    in_specs = [
        pl.BlockSpec(memory_space=pltpu.SMEM),
        pl.BlockSpec((1, tq, width), lambda b, i: (b, i, 0)),
        pl.BlockSpec((1, t_c, 2 * LANE), lambda b, i: (b, 0, 0)),
        pl.BlockSpec((1, t_c, 2 * LANE), lambda b, i: (b, 0, 0)),
    ]
    if latent:
        assert tq == WINDOW
        prev_map = lambda b, i: (b, jnp.maximum(i - 1, 0), 0)
        cur_map = lambda b, i: (b, i, 0)
        next_map = lambda b, i: (b, jnp.minimum(i + 1, n_blocks - 1), 0)
        for arr in (k, v):
            for mp in (prev_map, cur_map, next_map):
                args.append(arr)
                in_specs.append(pl.BlockSpec((1, tq, 2 * LANE), mp))
    return pl.pallas_call(
        functools.partial(_swa_kernel, latent=latent, tq=tq, n_blocks=n_blocks),
        grid=(bsz, n_blocks),
        in_specs=in_specs,
        out_specs=pl.BlockSpec((1, tq, width), lambda b, i: (b, i, 0)),
        out_shape=jax.ShapeDtypeStruct((bsz, t_q, width), BF16),
        compiler_params=_cparams(2),
        name=name,
    )(*args)


def _out_kernel(x_ref, o_ref, mod_ref, wg_ref, w_ref, lng_ref, lnb_ref, y_ref, *, alpha):
    d = x_ref.shape[-1]
    g = _silu(_dot(_modulate(x_ref, mod_ref), wg_ref[...]))
    u = (o_ref[0].astype(F32) * g).astype(BF16)
    br = _dot(u, w_ref[...])
    gate = mod_ref[0, :, 2 * d:3 * d]
    z = alpha * x_ref[0] + gate * br
    zc = z - jnp.mean(z, axis=-1, keepdims=True)
    var = jnp.mean(zc * zc, axis=-1, keepdims=True)
    y_ref[0] = zc * lax.rsqrt(var + NORM_EPS) * lng_ref[...] + lnb_ref[...]


def _out_call(x, o, w_gate, mod, w, ln_g, ln_b, alpha, tm, name):
    bsz, t_len, d = x.shape
    width = o.shape[-1]
    return pl.pallas_call(
        functools.partial(_out_kernel, alpha=alpha),
        grid=(bsz, t_len // tm),
        in_specs=[
            pl.BlockSpec((1, tm, d), lambda b, i: (b, i, 0)),
            pl.BlockSpec((1, tm, width), lambda b, i: (b, i, 0)),
            pl.BlockSpec((1, 1, mod.shape[-1]), lambda b, i: (b, 0, 0)),
            pl.BlockSpec(w_gate.shape, lambda b, i: (0, 0)),
            pl.BlockSpec(w.shape, lambda b, i: (0, 0)),
            pl.BlockSpec((1, d), lambda b, i: (0, 0)),
            pl.BlockSpec((1, d), lambda b, i: (0, 0)),
        ],
        out_specs=pl.BlockSpec((1, tm, d), lambda b, i: (b, i, 0)),
        out_shape=jax.ShapeDtypeStruct((bsz, t_len, d), F32),
        compiler_params=_cparams(2),
        name=name,
    )(x, o, mod, w_gate, w, ln_g.reshape(1, d), ln_b.reshape(1, d))


ATTN_ROWS = 1024
ATTN_TK = 1024
ATTN_UNROLL = 16
ATTN_PARTS = 4


def _tiles(t_len, n_row_groups=1):
    tm = min(512, t_len)
    tq = min(ATTN_ROWS // n_row_groups, t_len)
    tk = min(ATTN_TK, t_len)
    return tm, tq, tk


def _mla_layer(x, ctx, mod, mod_c, need_ctx, w_in, g_qa, w_qb, g_kva, w_kvb):
    t_len, t_c = x.shape[1], ctx.shape[1]
    tm, tq, tk = _tiles(t_len, 2)
    d = w_in.shape[0]
    o0, o1, o2 = A_Q_LORA, A_Q_LORA + A_KV_LORA, A_Q_LORA + A_KV_LORA + A_ROPE
    kpe_pad = jnp.concatenate([jnp.zeros((d, A_NOPE), F32), w_in[:, o1:o2],
                               jnp.zeros((d, LANE - A_NOPE - A_ROPE), F32)], axis=1)
    w_in_r = jnp.concatenate([w_in[:, :o1], kpe_pad], axis=1).astype(BF16)
    w_gate = w_in[:, o2:].astype(BF16)
    w_qb_r = jnp.pad(w_qb.reshape(A_Q_LORA, A_HEADS, A_NOPE + A_ROPE),
                     ((0, 0), (0, 0), (0, LANE - A_NOPE - A_ROPE))).reshape(A_Q_LORA, A_HEADS * LANE).astype(BF16)
    kvb = w_kvb.reshape(A_KV_LORA, A_HEADS, A_NOPE + A_V)
    k_pad = jnp.pad(kvb[:, :, :A_NOPE], ((0, 0), (0, 0), (0, LANE - A_NOPE))).reshape(A_KV_LORA, A_HEADS * LANE)
    w_kvb_r = jnp.concatenate([k_pad, kvb[:, :, A_NOPE:].reshape(A_KV_LORA, A_HEADS * A_V)], axis=1).astype(BF16)
    q_scale = (A_NOPE + A_ROPE) ** -0.5 * LOG2E
    gq, gk = g_qa.reshape(1, -1), g_kva.reshape(1, -1)
    tabs = _rope_tables(t_len, A_ROPE, "mla")
    q, k, v = _mla_proj_call(x, mod, tabs, w_in_r, gq, w_qb_r, gk, w_kvb_r, q_scale, tm, "mla_proj")
    q_c, k_c, v_c = _mla_proj_call(ctx, mod_c, None, w_in_r, gq, w_qb_r, gk, w_kvb_r, q_scale, t_c, "mla_proj_ctx")
    rows = ((0, None, 0), (1, None, 1))
    maps = dict(n_groups=A_HEADS // 2, q_map=lambda gi, j: 2 * gi + j, k_map=lambda gi, j: 2 * gi + j,
                v_map=lambda gi: gi, rows=rows, out_width=A_HEADS * A_V, lambda_init=0.0)
    o = _dense_attn_call("pair", q, k_c, v_c, k, v, (), tq=tq, tk=tk, name="mla_attn", **maps)
    o_c = _dense_attn_call("pair", q_c, k_c, v_c, None, None, (), tq=t_c, tk=tk, name="mla_attn_ctx", **maps) \
        if need_ctx else None
    return o, w_gate, o_c


def _diff_layer(x, ctx, mod, mod_c, need_ctx, layer_idx, w_in, lam, g_sub):
    t_len, t_c = x.shape[1], ctx.shape[1]
    tm, tq, tk = _tiles(t_len, 2)
    n = 2 * B_HEADS * B_HEAD
    q_scale = B_HEAD ** -0.5 * LOG2E
    segs = ((0, 0, n, "rope", q_scale, None), (1, n, n, "rope", 1.0, None), (2, 2 * n, n, "plain", 1.0, None))
    widths = (n, n, n)
    wb = w_in[:, :3 * n].astype(BF16)
    w_gate = w_in[:, 3 * n:].astype(BF16)
    tabs = _rope_tables(t_len, B_HEAD, "unit")
    q, k, v = _proj_call(x, mod, tabs, wb, None, segs, widths, B_HEAD // 2, tm, "diff_proj")
    q_c, k_c, v_c = _proj_call(ctx, mod_c, None, wb, None, segs, widths, B_HEAD // 2, t_c, "diff_proj_ctx")
    lambda_init = 0.8 - 0.6 * math.exp(-0.3 * layer_idx)
    rows = ((0, "lo", 0), (0, "hi", 0))
    extras = (lam.astype(F32), g_sub.reshape(1, -1).astype(F32))
    maps = dict(n_groups=B_HEADS, q_map=lambda gi, j: gi, k_map=lambda gi, j: gi, v_map=lambda gi: gi,
                rows=rows, out_width=n, lambda_init=lambda_init)
    o = _dense_attn_call("diff", q, k_c, v_c, k, v, extras, tq=tq, tk=tk, name="diff_attn", **maps)
    o_c = _dense_attn_call("diff", q_c, k_c, v_c, None, None, extras, tq=t_c, tk=tk, name="diff_attn_ctx", **maps) \
        if need_ctx else None
    return o, w_gate, o_c


def _gqa_layer(x, ctx, mod, mod_c, need_ctx, w_in, g_q, g_k):
    t_len, t_c = x.shape[1], ctx.shape[1]
    tm, tq, tk = _tiles(t_len, C_HEADS // C_KV_HEADS)
    nq, nkv = C_HEADS * C_HEAD, C_KV_HEADS * C_HEAD
    q_scale = C_HEAD ** -0.5 * LOG2E
    segs = ((0, 0, nq, "rope", q_scale, 0), (1, nq, nkv, "rope", 1.0, 1), (2, nq + nkv, nkv, "plain", 1.0, None))
    widths = (nq, nkv, nkv)
    wb = w_in[:, :nq + 2 * nkv].astype(BF16)
    w_gate = w_in[:, nq + 2 * nkv:].astype(BF16)
    gains = jnp.stack([g_q, g_k]).astype(F32)
    tabs = _rope_tables(t_len, C_HEAD, "unit")
    q, k, v = _proj_call(x, mod, tabs, wb, gains, segs, widths, C_HEAD // 2, tm, "gqa_proj")
    q_c, k_c, v_c = _proj_call(ctx, mod_c, None, wb, gains, segs, widths, C_HEAD // 2, t_c, "gqa_proj_ctx")
    per = C_HEADS // C_KV_HEADS
    rows = tuple((j, None, 0) for j in range(per))
    maps = dict(n_groups=C_KV_HEADS, q_map=lambda gi, j: per * gi + j, k_map=lambda gi, j: gi, v_map=lambda gi: gi,
                rows=rows, out_width=nq, lambda_init=0.0)
    o = _dense_attn_call("stack", q, k_c, v_c, k, v, (), tq=tq, tk=tk, name="gqa_attn", **maps)
    o_c = _dense_attn_call("stack", q_c, k_c, v_c, None, None, (), tq=t_c, tk=tk, name="gqa_attn_ctx", **maps) \
        if need_ctx else None
    return o, w_gate, o_c


def _swa_layer(x, ctx, mod, mod_c, need_ctx, w_in, sinks):
    t_len, t_c = x.shape[1], ctx.shape[1]
    tm, _, _ = _tiles(t_len)
    nq, nkv = D_HEADS * D_HEAD, D_KV_HEADS * D_HEAD
    wk = w_in[:, nq:nq + nkv]
    wv = w_in[:, nq + nkv:nq + 2 * nkv]
    swap = lambda w: jnp.concatenate([w, w[:, D_HEAD:], w[:, :D_HEAD]], axis=1)
    wb = jnp.concatenate([w_in[:, :nq], swap(wk), swap(wv)], axis=1).astype(BF16)
    w_gate = w_in[:, nq + 2 * nkv:].astype(BF16)
    q_scale = D_HEAD ** -0.5 * LOG2E
    segs = ((0, 0, nq, "rope", q_scale, None), (1, nq, 2 * nkv, "rope", 1.0, None),
            (2, nq + 2 * nkv, 2 * nkv, "plain", 1.0, None))
    widths = (nq, 2 * nkv, 2 * nkv)
    tabs = _rope_tables(t_len, D_HEAD, "unit")
    q, k, v = _proj_call(x, mod, tabs, wb, None, segs, widths, D_HEAD // 2, tm, "swa_proj")
    q_c, k_c, v_c = _proj_call(ctx, mod_c, None, wb, None, segs, widths, D_HEAD // 2, t_c, "swa_proj_ctx")
    sinks_log2 = sinks.astype(F32) * LOG2E
    o = _swa_call(q, k_c, v_c, k, v, sinks_log2, WINDOW, "swa_attn")
    o_c = _swa_call(q_c, k_c, v_c, None, None, sinks_log2, t_c, "swa_attn_ctx") if need_ctx else None
    return o, w_gate, o_c


def kernel(x, c, ctx, c_ctx, ada_w, ada_b, out_w, ln_g, ln_b, mla_w_in, mla_g_qa, mla_w_qb, mla_g_kva, mla_w_kvb,
           diff_w_in, diff_lambda, diff_g_sub, gqa_w_in, gqa_g_q, gqa_g_k, swa_w_in, swa_sink):
    depth = ada_w.shape[0]
    bsz, t_len, d = x.shape
    t_c = ctx.shape[1]
    alpha = (2 * depth) ** 0.25
    n_rows = 8
    assert bsz + 1 <= n_rows
    cvec = jnp.concatenate([c, c_ctx[None, :], jnp.zeros((n_rows - bsz - 1, d), F32)], axis=0)
    mods = _adaln_call(cvec, ada_w, ada_b)
    tm = _tiles(t_len)[0]
    for i in range(depth):
        kind, j = i % 4, i // 4
        need_ctx = i < depth - 1
        mod = mods[i, 0:bsz][:, None, :]
        mod_c = jnp.broadcast_to(mods[i, bsz][None, None, :], (bsz, 1, 3 * d))
        if kind == 0:
            o, w_gate, o_c = _mla_layer(x, ctx, mod, mod_c, need_ctx, mla_w_in[j], mla_g_qa[j], mla_w_qb[j],
                                        mla_g_kva[j], mla_w_kvb[j])
        elif kind == 1:
            o, w_gate, o_c = _diff_layer(x, ctx, mod, mod_c, need_ctx, i, diff_w_in[j], diff_lambda[j], diff_g_sub[j])
        elif kind == 2:
            o, w_gate, o_c = _gqa_layer(x, ctx, mod, mod_c, need_ctx, gqa_w_in[j], gqa_g_q[j], gqa_g_k[j])
        else:
            o, w_gate, o_c = _swa_layer(x, ctx, mod, mod_c, need_ctx, swa_w_in[j], swa_sink[j])
        wo = out_w[i].astype(BF16)
        x = _out_call(x, o, w_gate, mod, wo, ln_g[i], ln_b[i], alpha, tm, "out_proj")
        if need_ctx:
            ctx = _out_call(ctx, o_c, w_gate, mod_c, wo, ln_g[i], ln_b[i], alpha, t_c, "out_proj_ctx")
    return x
```

```python
import functools
import math

import jax
import jax.numpy as jnp
from jax import lax
from jax.experimental import pallas as pl
from jax.experimental.pallas import tpu as pltpu

D_MODEL = 1024
GRID_W = 64
WINDOW = 128
ROPE_THETA = 10000.0
NORM_EPS = 1e-6
A_HEADS, A_Q_LORA, A_KV_LORA, A_NOPE, A_ROPE, A_V = 16, 256, 128, 64, 32, 64
B_HEADS, B_HEAD = 8, 64
C_HEADS, C_KV_HEADS, C_HEAD = 8, 2, 128
D_HEADS, D_KV_HEADS, D_HEAD = 16, 2, 64

LANE = 128
HALF = LANE // 2
LOG2E = math.log2(math.e)
NEG_BIG = -1e30
VMEM_LIMIT = 56 * 1024 * 1024
BF16 = jnp.bfloat16
F32 = jnp.float32


def _cparams(n_grid):
    return pltpu.CompilerParams(dimension_semantics=("arbitrary",) * n_grid, vmem_limit_bytes=VMEM_LIMIT)


def _silu(x):
    return x / (1.0 + jnp.exp(-x))


def _dot(a, b):
    return jnp.dot(a, b, preferred_element_type=F32)


def _dot_nt(a, b):
    return lax.dot_general(a, b, (((1,), (1,)), ((), ())), preferred_element_type=F32)


def _adaln_kernel(c_ref, w_ref, b_ref, o_ref):
    sc = _silu(c_ref[...])
    o_ref[0] = jnp.dot(sc, w_ref[0], preferred_element_type=F32, precision=lax.Precision.HIGHEST) + b_ref[0]


def _adaln_call(cvec, ada_w, ada_b):
    depth, d, n = ada_w.shape
    rows = cvec.shape[0]
    tn = 1024
    return pl.pallas_call(
        _adaln_kernel,
        grid=(depth, n // tn),
        in_specs=[
            pl.BlockSpec((rows, d), lambda i, j: (0, 0)),
            pl.BlockSpec((1, d, tn), lambda i, j: (i, 0, j)),
            pl.BlockSpec((1, 1, tn), lambda i, j: (i, 0, j)),
        ],
        out_specs=pl.BlockSpec((1, rows, tn), lambda i, j: (i, 0, j)),
        out_shape=jax.ShapeDtypeStruct((depth, rows, n), F32),
        compiler_params=_cparams(2),
        name="adaln",
    )(cvec, ada_w, ada_b.reshape(depth, 1, n))


def _mla_slab(pe_lo, nope_a, pe_hi, nope_b):
    used = pe_lo.shape[-1] + nope_a.shape[-1] + pe_hi.shape[-1] + nope_b.shape[-1]
    pad = jnp.zeros(pe_lo.shape[:-1] + (LANE - used,), pe_lo.dtype)
    return jnp.concatenate([pe_lo, nope_a, pe_hi, nope_b, pad], axis=-1)


def _rope_tables(t_len, rot_dim, layout):
    rows = t_len // GRID_W
    row = jnp.repeat(jnp.arange(rows, dtype=F32), GRID_W)
    col = jnp.tile(jnp.arange(GRID_W, dtype=F32), rows)
    n_freq = rot_dim // 4
    inv_freq = ROPE_THETA ** (-jnp.arange(n_freq, dtype=F32) / n_freq)
    ang = jnp.concatenate([row[:, None] * inv_freq, col[:, None] * inv_freq], axis=-1)
    cos, sin = jnp.cos(ang), jnp.sin(ang)
    zero = jnp.zeros_like(sin)
    if layout == "mla":
        pieces = (cos, jnp.ones((t_len, HALF - A_ROPE // 2), F32), cos,
                  jnp.ones((t_len, A_NOPE - (HALF - A_ROPE // 2)), F32))
        c = _mla_slab(*pieces)
        s_lo = _mla_slab(-sin, 0 * pieces[1], zero, 0 * pieces[3])
        s_hi = _mla_slab(zero, 0 * pieces[1], sin, 0 * pieces[3])
    else:
        reps = LANE // rot_dim
        c = jnp.tile(jnp.concatenate([cos, cos], axis=-1), (1, reps))
        s_lo = jnp.tile(jnp.concatenate([-sin, zero], axis=-1), (1, reps))
        s_hi = jnp.tile(jnp.concatenate([zero, sin], axis=-1), (1, reps))
    return c, s_lo, s_hi


def _rope_slab(x, c, s_lo, s_hi, half):
    if half == HALF:
        return x * c + pltpu.roll(x, HALF, 1) * (s_lo + s_hi)
    return x * c + pltpu.roll(x, LANE - half, 1) * s_lo + pltpu.roll(x, half, 1) * s_hi


def _rms(x, g):
    return x * lax.rsqrt(jnp.mean(x * x, axis=-1, keepdims=True) + NORM_EPS) * g


PROJ_CHUNK = 512
PROJ_SUB_ROWS = 256


def _proj_kernel(*refs, segs, rope_half, has_rope, has_norm):
    it = iter(refs)
    x_ref, mod_ref = next(it), next(it)
    if has_rope:
        c = next(it)[...]
        s_lo = next(it)[...]
        s_hi = next(it)[...]
    w_ref = next(it)
    gn_ref = next(it) if has_norm else None
    out_refs = list(it)
    tm, d = x_ref.shape[1], x_ref.shape[2]
    shift, scale = mod_ref[0, :, 0:d], mod_ref[0, :, d:2 * d]
    rs = min(tm, PROJ_SUB_ROWS)
    hbs = [(x_ref[0, r0:r0 + rs, :] * (1.0 + scale) + shift).astype(BF16) for r0 in range(0, tm, rs)]
    items = [(ri, seg, cc, min(PROJ_CHUNK, seg[2] - cc))
             for ri in range(tm // rs) for seg in segs for cc in range(0, seg[2], PROJ_CHUNK)]

    def matmul(item):
        ri, (_, col0, _, _, _, _), cc, cw = item
        return _dot(hbs[ri], w_ref[:, col0 + cc:col0 + cc + cw])

    def finish(item, t):
        ri, (oi, _, _, kind, q_scale, norm_row), cc, cw = item
        rows = slice(ri * rs, (ri + 1) * rs)
        if kind == "plain":
            y = t
        else:
            slabs = []
            for s in range(cw // LANE):
                xs = t[:, s * LANE:(s + 1) * LANE]
                if norm_row is not None:
                    xs = _rms(xs, gn_ref[norm_row:norm_row + 1, :])
                if has_rope:
                    xs = _rope_slab(xs, c[rows], s_lo[rows], s_hi[rows], rope_half)
                if q_scale != 1.0:
                    xs = xs * q_scale
                slabs.append(xs)
            y = jnp.concatenate(slabs, axis=1) if len(slabs) > 1 else slabs[0]
        out_refs[oi][0, rows, cc:cc + cw] = y.astype(BF16)

    t_cur = matmul(items[0])
    for n, item in enumerate(items):
        t_next = matmul(items[n + 1]) if n + 1 < len(items) else None
        finish(item, t_cur)
        t_cur = t_next


def _proj_call(x, mod, tabs, w, gains, segs, out_widths, rope_half, tm, name):
    bsz, t_len, d = x.shape
    has_rope = tabs is not None
    has_norm = gains is not None
    args = [x, mod]
    in_specs = [
        pl.BlockSpec((1, tm, d), lambda b, i: (b, i, 0)),
        pl.BlockSpec((1, 1, mod.shape[-1]), lambda b, i: (b, 0, 0)),
    ]
    if has_rope:
        for tab in tabs:
            args.append(tab)
            in_specs.append(pl.BlockSpec((tm, LANE), lambda b, i: (i, 0)))
    args.append(w)
    in_specs.append(pl.BlockSpec(w.shape, lambda b, i: (0, 0)))
    if has_norm:
        args.append(gains)
        in_specs.append(pl.BlockSpec(gains.shape, lambda b, i: (0, 0)))
    return pl.pallas_call(
        functools.partial(_proj_kernel, segs=segs, rope_half=rope_half, has_rope=has_rope, has_norm=has_norm),
        grid=(bsz, t_len // tm),
        in_specs=in_specs,
        out_specs=[pl.BlockSpec((1, tm, ow), lambda b, i: (b, i, 0)) for ow in out_widths],
        out_shape=[jax.ShapeDtypeStruct((bsz, t_len, ow), BF16) for ow in out_widths],
        compiler_params=_cparams(2),
        name=name,
    )(*args)


def _mla_proj_kernel(*refs, has_rope, q_scale):
    it = iter(refs)
    x_ref, mod_ref = next(it), next(it)
    if has_rope:
        c = next(it)[...]
        s_lo = next(it)[...]
        s_hi = next(it)[...]
    w_in_ref, gqa_ref, w_qb_ref, gkva_ref, w_kvb_ref = next(it), next(it), next(it), next(it), next(it)
    q_ref, k_ref, v_ref = next(it), next(it), next(it)
    half = HALF
    n_lat =A_Q_LORA + A_KV_LORA + LANE
    n_qk = A_HEADS * LANE
    n_v = A_HEADS * A_V
    tm, d = x_ref.shape[1], x_ref.shape[2]
    shift, scale = mod_ref[0, :, 0:d], mod_ref[0, :, d:2 * d]
    rs = min(tm, PROJ_SUB_ROWS)
    for r0 in range(0, tm, rs):
        rows = slice(r0, r0 + rs)
        hb = (x_ref[0, rows, :] * (1.0 + scale) + shift).astype(BF16)
        t = _dot(hb, w_in_ref[...])
        qn = _rms(t[:, 0:A_Q_LORA], gqa_ref[...]).astype(BF16)
        kn = _rms(t[:, A_Q_LORA:A_Q_LORA + A_KV_LORA], gkva_ref[...]).astype(BF16)
        kpe = t[:, A_Q_LORA + A_KV_LORA:n_lat]
        if has_rope:
            cr, slr, shr = c[rows], s_lo[rows], s_hi[rows]
            kpe = _rope_slab(kpe, cr, slr, shr, half)
        for cc in range(0, n_qk, PROJ_CHUNK):
            tq = _dot(qn, w_qb_ref[:, cc:cc + PROJ_CHUNK])
            tk = _dot(kn, w_kvb_ref[:, cc:cc + PROJ_CHUNK])
            qs, ks = [], []
            for s in range(PROJ_CHUNK // LANE):
                xq = tq[:, s * LANE:(s + 1) * LANE]
                if has_rope:
                    xq = _rope_slab(xq, cr, slr, shr, half)
                qs.append(xq * q_scale)
                ks.append(tk[:, s * LANE:(s + 1) * LANE] + kpe)
            q_ref[0, rows, cc:cc + PROJ_CHUNK] = jnp.concatenate(qs, axis=1).astype(BF16)
            k_ref[0, rows, cc:cc + PROJ_CHUNK] = jnp.concatenate(ks, axis=1).astype(BF16)
        for cc in range(0, n_v, PROJ_CHUNK):
            v_ref[0, rows, cc:cc + PROJ_CHUNK] = _dot(kn, w_kvb_ref[:, n_qk + cc:n_qk + cc + PROJ_CHUNK]).astype(BF16)


def _mla_proj_call(x, mod, tabs, w_in, g_qa, w_qb, g_kva, w_kvb, q_scale, tm, name):
    bsz, t_len, d = x.shape
    has_rope = tabs is not None
    args = [x, mod]
    in_specs = [
        pl.BlockSpec((1, tm, d), lambda b, i: (b, i, 0)),
        pl.BlockSpec((1, 1, mod.shape[-1]), lambda b, i: (b, 0, 0)),
    ]
    if has_rope:
        for tab in tabs:
            args.append(tab)
            in_specs.append(pl.BlockSpec((tm, LANE), lambda b, i: (i, 0)))
    for a in (w_in, g_qa, w_qb, g_kva, w_kvb):
        args.append(a)
        in_specs.append(pl.BlockSpec(a.shape, lambda b, i: (0, 0)))
    out_widths = (A_HEADS * LANE, A_HEADS * LANE, A_HEADS * A_V)
    return pl.pallas_call(
        functools.partial(_mla_proj_kernel, has_rope=has_rope, q_scale=q_scale),
        grid=(bsz, t_len // tm),
        in_specs=in_specs,
        out_specs=[pl.BlockSpec((1, tm, ow), lambda b, i: (b, i, 0)) for ow in out_widths],
        out_shape=[jax.ShapeDtypeStruct((bsz, t_len, ow), BF16) for ow in out_widths],
        compiler_params=_cparams(2),
        name=name,
    )(*args)


def _lane_lo(shape):
    return lax.broadcasted_iota(jnp.int32, shape, len(shape) - 1) < HALF


def _dense_attn_kernel(*refs, mode, n_q, n_k, rows, tq, tk, n_chunks, n_parts, unroll, lambda_init):
    it = iter(refs)
    q_refs = [next(it) for _ in range(n_q)]
    kc_refs = [next(it) for _ in range(n_k)]
    vc_ref = next(it)
    if n_chunks:
        k_refs = [next(it) for _ in range(n_k)]
        v_ref = next(it)
    if mode == "diff":
        lam_ref, gsub_ref = next(it), next(it)
    o_ref = next(it)
    qs_scr, m_scr, acc_scr = next(it), next(it), next(it)
    buf_a, buf_b = next(it), next(it)

    n_rows = len(rows)
    for r, (qi, mask, _) in enumerate(rows):
        q = q_refs[qi][0]
        if mask is not None:
            lo = _lane_lo(q.shape)
            q = jnp.where(lo if mask == "lo" else jnp.logical_not(lo), q, jnp.zeros_like(q))
        qs_scr[r * tq:(r + 1) * tq, :] = q
    spans = []
    for ks in range(n_k):
        idx = [r for r, row in enumerate(rows) if row[2] == ks]
        spans.append((idx[0] * tq, (idx[-1] + 1) * tq))
    blocks_per_span = max(1, n_parts // n_k)
    row_blocks = []
    for (r0, r1) in spans:
        nb = (r1 - r0) // blocks_per_span
        row_blocks += [(r0 + i * nb, r0 + (i + 1) * nb) for i in range(blocks_per_span)]

    def qk_part(k_load, s_ref, c0, c1):
        for ks, (r0, r1) in enumerate(spans):
            s_ref[r0:r1, c0:c1] = _dot_nt(qs_scr[r0:r1, :], k_load(ks, c0, c1))

    def sm_part(v_ext, s_ref, a, b):
        n_cols = v_ext.shape[0] // LANE
        mp = s_ref[a:b, 0:LANE]
        for j in range(1, n_cols):
            mp = jnp.maximum(mp, s_ref[a:b, j * LANE:(j + 1) * LANE])
        m_prev = m_scr[a:b, :]
        m_new = jnp.maximum(m_prev, jnp.max(mp, axis=1, keepdims=True))
        ps = [jnp.exp2(s_ref[a:b, j * LANE:(j + 1) * LANE] - m_new).astype(BF16) for j in range(n_cols)]
        p = jnp.concatenate(ps, axis=1) if n_cols > 1 else ps[0]
        pv = _dot(p, v_ext)
        alpha = jnp.exp2(m_prev - m_new)
        acc_scr[a:b, :] = jnp.concatenate([alpha, alpha], axis=1) * acc_scr[a:b, :] + pv
        m_scr[a:b, :] = m_new

    def ext(v):
        return jnp.concatenate([v, jnp.ones(v.shape, BF16)], axis=1)

    def k_loader(off):
        return lambda ks, c0, c1: k_refs[ks][0, pl.ds(off + c0, c1 - c0), :]

    def v_chunk(off):
        return ext(v_ref[0, pl.ds(off, tk), :])

    def pipelined(v_ext, s_cur, k_next, s_next, next_width):
        kw = max(LANE, next_width // n_parts)
        for pi, (a, b) in enumerate(row_blocks):
            sm_part(v_ext, s_cur, a, b)
            if pi * kw < next_width:
                qk_part(k_next, s_next, pi * kw, (pi + 1) * kw)

    bufs = (buf_a, buf_b)
    t_c = vc_ref.shape[1]
    ctx_loader = lambda ks, c0, c1: kc_refs[ks][0, c0:c1, :]
    if n_chunks:
        qk_part(k_loader(0), bufs[0], 0, tk)
    else:
        qk_part(ctx_loader, bufs[0], 0, t_c)
    m_scr[...] = jnp.full(m_scr.shape, NEG_BIG, F32)
    acc_scr[...] = jnp.zeros(acc_scr.shape, F32)
    if n_chunks:
        n_trips = (n_chunks - 1) // unroll

        def step(ti, carry):
            base = ti * (unroll * tk)
            for u in range(unroll):
                off = pl.multiple_of(base + u * tk, tk)
                nxt = pl.multiple_of(base + (u + 1) * tk, tk)
                pipelined(v_chunk(off), bufs[u % 2], k_loader(nxt), bufs[(u + 1) % 2], tk)
            return carry
        if n_trips:
            lax.fori_loop(0, n_trips, step, 0)
        for c in range(unroll * n_trips, n_chunks):
            if c == n_chunks - 1:
                pipelined(v_chunk(c * tk), bufs[c % 2], ctx_loader, bufs[(c + 1) % 2], t_c)
            else:
                pipelined(v_chunk(c * tk), bufs[c % 2], k_loader((c + 1) * tk), bufs[(c + 1) % 2], tk)
    vc_ext = ext(vc_ref[0])
    for (a, b) in row_blocks:
        sm_part(vc_ext, bufs[n_chunks % 2], a, b)

    acc = acc_scr[...]
    o = acc[:, 0:LANE] / acc[:, LANE:2 * LANE]
    if mode == "pair":
        out = jnp.where(_lane_lo((tq, LANE)), o[0:tq], o[tq:2 * tq])
    elif mode == "diff":
        lam = lam_ref[...]
        a = jnp.sum(lam[0:1, :] * lam[1:2, :], axis=1, keepdims=True)
        b = jnp.sum(lam[2:3, :] * lam[3:4, :], axis=1, keepdims=True)
        lam_full = jnp.exp(a) - jnp.exp(b) + lambda_init
        dlt = o[0:tq] - lam_full * o[tq:2 * tq]
        out = _rms(dlt, gsub_ref[...]) * (1.0 - lambda_init)
    else:
        out = jnp.concatenate([o[r * tq:(r + 1) * tq] for r in range(n_rows)], axis=1)
    o_ref[0] = out.astype(o_ref.dtype)


def _dense_attn_call(mode, q, kc, vc, k, v, extras, n_groups, q_map, k_map, v_map, rows, out_width, tq, tk,
                     lambda_init, name):
    bsz, t_q, _ = q.shape
    t_c = kc.shape[1]
    n_q = 1 + max(r[0] for r in rows)
    n_k = 1 + max(r[2] for r in rows)
    n_rows = len(rows)
    has_latent = k is not None
    n_chunks = (k.shape[1] // tk) if has_latent else 0
    args, in_specs = [], []
    for j in range(n_q):
        args.append(q)
        in_specs.append(pl.BlockSpec((1, tq, LANE), lambda b, g, i, j=j: (b, i, q_map(g, j))))
    for j in range(n_k):
        args.append(kc)
        in_specs.append(pl.BlockSpec((1, t_c, LANE), lambda b, g, i, j=j: (b, 0, k_map(g, j))))
    args.append(vc)
    in_specs.append(pl.BlockSpec((1, t_c, LANE), lambda b, g, i: (b, 0, v_map(g))))
    if has_latent:
        t_k = k.shape[1]
        for j in range(n_k):
            args.append(k)
            in_specs.append(pl.BlockSpec((1, t_k, LANE), lambda b, g, i, j=j: (b, 0, k_map(g, j))))
        args.append(v)
        in_specs.append(pl.BlockSpec((1, t_k, LANE), lambda b, g, i: (b, 0, v_map(g))))
    for e in extras:
        args.append(e)
        in_specs.append(pl.BlockSpec(e.shape, lambda b, g, i: (0, 0)))
    ow = out_width // n_groups
    s_width = max(tk, t_c) if has_latent else t_c
    return pl.pallas_call(
        functools.partial(_dense_attn_kernel, mode=mode, n_q=n_q, n_k=n_k, rows=rows, tq=tq, tk=tk,
                          n_chunks=n_chunks, n_parts=ATTN_PARTS, unroll=ATTN_UNROLL, lambda_init=lambda_init),
        grid=(bsz, n_groups, t_q // tq),
        in_specs=in_specs,
        out_specs=pl.BlockSpec((1, tq, ow), lambda b, g, i: (b, i, g)),
        out_shape=jax.ShapeDtypeStruct((bsz, t_q, out_width), BF16),
        scratch_shapes=[
            pltpu.VMEM((n_rows * tq, LANE), BF16),
            pltpu.VMEM((n_rows * tq, LANE), F32),
            pltpu.VMEM((n_rows * tq, 2 * LANE), F32),
            pltpu.VMEM((n_rows * tq, s_width), F32),
            pltpu.VMEM((n_rows * tq, s_width), F32),
        ],
        compiler_params=_cparams(3),
        name=name,
    )(*args)


def _swa_kernel(*refs, latent, tq, n_blocks):
    it = iter(refs)
    sink_ref = next(it)
    q_ref = next(it)
    kc_ref, vc_ref = next(it), next(it)
    if latent:
        kp_ref, k0_ref, kn_ref = next(it), next(it), next(it)
        vp_ref, v0_ref, vn_ref = next(it), next(it), next(it)
    o_ref = next(it)
    i = pl.program_id(1)
    per_g = D_HEADS // D_KV_HEADS
    n_pairs = per_g // 2
    lo = _lane_lo((tq, LANE))
    if latent:
        r = lax.broadcasted_iota(jnp.int32, (tq, tq), 0)
        cidx = lax.broadcasted_iota(jnp.int32, (tq, tq), 1)
        ok_prev = jnp.logical_and(cidx >= r, i > 0)
        ok_next = jnp.logical_and(cidx <= r, i < n_blocks - 1)
        ok_pc = jnp.concatenate([ok_prev, jnp.full((tq, tq), True)], axis=1)
        ok_pc = jnp.concatenate([ok_pc] * n_pairs, axis=0)
        ok_next = jnp.concatenate([ok_next] * n_pairs, axis=0)

    def scores(g, e):
        slab = g ^ e
        sl = slice(slab * LANE, (slab + 1) * LANE)
        qrows, sinks = [], []
        for pi in range(n_pairs):
            ps = g * n_pairs + pi
            qv = q_ref[0, :, ps * LANE:(ps + 1) * LANE]
            qrows.append(jnp.where(lo if e == 0 else jnp.logical_not(lo), qv, jnp.zeros_like(qv)))
            sinks.append(jnp.full((tq, LANE), sink_ref[g * per_g + 2 * pi + e], F32))
        qst = jnp.concatenate(qrows, axis=0)
        sink = jnp.concatenate(sinks, axis=0)
        s_list = [_dot_nt(qst, kc_ref[0, :, sl])]
        if latent:
            k_pc = jnp.concatenate([kp_ref[0, :, sl], k0_ref[0, :, sl]], axis=0)
            s_list.append(jnp.where(ok_pc, _dot_nt(qst, k_pc), NEG_BIG))
            s_list.append(jnp.where(ok_next, _dot_nt(qst, kn_ref[0, :, sl]), NEG_BIG))
        return jnp.concatenate(s_list, axis=1), sink, sl

    def softmax_pv(s, sink, sl):
        v_list = [vc_ref[0, :, sl]]
        if latent:
            v_list += [vp_ref[0, :, sl], v0_ref[0, :, sl], vn_ref[0, :, sl]]
        vv = jnp.concatenate(v_list, axis=0)
        vv = jnp.concatenate([vv, jnp.ones(vv.shape, BF16)], axis=1)
        m = jnp.maximum(jnp.max(s, axis=1, keepdims=True), sink)
        ps = [jnp.exp2(s[:, j * LANE:(j + 1) * LANE] - m).astype(BF16) for j in range(s.shape[1] // LANE)]
        pv = _dot(jnp.concatenate(ps, axis=1), vv)
        return pv[:, 0:LANE] / (pv[:, LANE:2 * LANE] + jnp.exp2(sink - m))

    groups = [(g, e) for g in range(D_KV_HEADS) for e in range(2)]
    outs = []
    pending = scores(*groups[0])
    for gi in range(len(groups)):
        nxt = scores(*groups[gi + 1]) if gi + 1 < len(groups) else None
        outs.append(softmax_pv(*pending))
        pending = nxt
    for g in range(D_KV_HEADS):
        for pi in range(n_pairs):
            ps = g * n_pairs + pi
            out = jnp.where(lo, outs[2 * g][pi * tq:(pi + 1) * tq], outs[2 * g + 1][pi * tq:(pi + 1) * tq])
            o_ref[0, :, ps * LANE:(ps + 1) * LANE] = out.astype(o_ref.dtype)


def _swa_call(q, kc, vc, k, v, sinks_log2, tq, name):
    bsz, t_q, width = q.shape
    t_c = kc.shape[1]
    latent = k is not None
    n_blocks = t_q // tq
    args = [sinks_log2, q, kc, vc]
    in_specs = [
        pl.BlockSpec(memory_space=pltpu.SMEM),
        pl.BlockSpec((1, tq, width), lambda b, i: (b, i, 0)),
        pl.BlockSpec((1, t_c, 2 * LANE), lambda b, i: (b, 0, 0)),
        pl.BlockSpec((1, t_c, 2 * LANE), lambda b, i: (b, 0, 0)),
    ]
    if latent:
        assert tq == WINDOW
        prev_map = lambda b, i: (b, jnp.maximum(i - 1, 0), 0)
        cur_map = lambda b, i: (b, i, 0)
        next_map = lambda b, i: (b, jnp.minimum(i + 1, n_blocks - 1), 0)
        for arr in (k, v):
            for mp in (prev_map, cur_map, next_map):
                args.append(arr)
                in_specs.append(pl.BlockSpec((1, tq, 2 * LANE), mp))
    return pl.pallas_call(
        functools.partial(_swa_kernel, latent=latent, tq=tq, n_blocks=n_blocks),
        grid=(bsz, n_blocks),
        in_specs=in_specs,
        out_specs=pl.BlockSpec((1, tq, width), lambda b, i: (b, i, 0)),
        out_shape=jax.ShapeDtypeStruct((bsz, t_q, width), BF16),
        compiler_params=_cparams(2),
        name=name,
    )(*args)


def _out_kernel(x_ref, o_ref, mod_ref, wg_ref, w_ref, lng_ref, lnb_ref, y_ref, *, alpha):
    tm, d = x_ref.shape[1], x_ref.shape[2]
    shift, scale, gate = mod_ref[0, :, 0:d], mod_ref[0, :, d:2 * d], mod_ref[0, :, 2 * d:3 * d]
    rs = min(tm, OUT_SUB_ROWS)
    n_sub = tm // rs

    def branch(r0):
        xs = x_ref[0, r0:r0 + rs, :]
        hb = (xs * (1.0 + scale) + shift).astype(BF16)
        g = _silu(_dot(hb, wg_ref[...]))
        u = (o_ref[0, r0:r0 + rs, :].astype(F32) * g).astype(BF16)
        return alpha * xs + gate * _dot(u, w_ref[...])

    def norm_store(r0, z):
        zc = z - jnp.mean(z, axis=-1, keepdims=True)
        var = jnp.mean(zc * zc, axis=-1, keepdims=True)
        y_ref[0, r0:r0 + rs, :] = zc * lax.rsqrt(var + NORM_EPS) * lng_ref[...] + lnb_ref[...]

    zs = [branch(i * rs) for i in range(n_sub)]
    for i in range(n_sub):
        norm_store(i * rs, zs[i])


def _out_call(x, o, w_gate, mod, w, ln_g, ln_b, alpha, tm, name):
    bsz, t_len, d = x.shape
    width = o.shape[-1]
    return pl.pallas_call(
        functools.partial(_out_kernel, alpha=alpha),
        grid=(bsz, t_len // tm),
        in_specs=[
            pl.BlockSpec((1, tm, d), lambda b, i: (b, i, 0)),
            pl.BlockSpec((1, tm, width), lambda b, i: (b, i, 0)),
            pl.BlockSpec((1, 1, mod.shape[-1]), lambda b, i: (b, 0, 0)),
            pl.BlockSpec(w_gate.shape, lambda b, i: (0, 0)),
            pl.BlockSpec(w.shape, lambda b, i: (0, 0)),
            pl.BlockSpec((1, d), lambda b, i: (0, 0)),
            pl.BlockSpec((1, d), lambda b, i: (0, 0)),
        ],
        out_specs=pl.BlockSpec((1, tm, d), lambda b, i: (b, i, 0)),
        out_shape=jax.ShapeDtypeStruct((bsz, t_len, d), F32),
        compiler_params=_cparams(2),
        name=name,
    )(x, o, mod, w_gate, w, ln_g.reshape(1, d), ln_b.reshape(1, d))


OUT_ROWS = 1024
OUT_SUB_ROWS = 256
ATTN_ROWS = 1024
ATTN_TK = 1024
ATTN_UNROLL = 16
ATTN_PARTS = 4


def _tiles(t_len, n_row_groups=1):
    tm = min(512, t_len)
    tq = min(ATTN_ROWS // n_row_groups, t_len)
    tk = min(ATTN_TK, t_len)
    return tm, tq, tk


def _mla_layer(x, ctx, mod, mod_c, need_ctx, w_in, g_qa, w_qb, g_kva, w_kvb):
    t_len, t_c = x.shape[1], ctx.shape[1]
    tm, tq, tk = _tiles(t_len, 2)
    d = w_in.shape[0]
    o0, o1, o2 = A_Q_LORA, A_Q_LORA + A_KV_LORA, A_Q_LORA + A_KV_LORA + A_ROPE
    hr = A_ROPE // 2
    na = HALF - hr
    w_kpe = w_in[:, o1:o2]
    kpe_pad = _mla_slab(w_kpe[:, :hr], jnp.zeros((d, na), F32), w_kpe[:, hr:], jnp.zeros((d, A_NOPE - na), F32))
    w_in_r = jnp.concatenate([w_in[:, :o1], kpe_pad], axis=1).astype(BF16)
    w_gate = w_in[:, o2:].astype(BF16)
    qb = w_qb.reshape(A_Q_LORA, A_HEADS, A_NOPE + A_ROPE)
    w_qb_r = _mla_slab(qb[:, :, A_NOPE:A_NOPE + hr], qb[:, :, :na], qb[:, :, A_NOPE + hr:], qb[:, :, na:A_NOPE])
    w_qb_r = w_qb_r.reshape(A_Q_LORA, A_HEADS * LANE).astype(BF16)
    kvb = w_kvb.reshape(A_KV_LORA, A_HEADS, A_NOPE + A_V)
    zk = jnp.zeros((A_KV_LORA, A_HEADS, hr), F32)
    k_pad = _mla_slab(zk, kvb[:, :, :na], zk, kvb[:, :, na:A_NOPE]).reshape(A_KV_LORA, A_HEADS * LANE)
    w_kvb_r = jnp.concatenate([k_pad, kvb[:, :, A_NOPE:].reshape(A_KV_LORA, A_HEADS * A_V)], axis=1).astype(BF16)
    q_scale = (A_NOPE + A_ROPE) ** -0.5 * LOG2E
    gq, gk = g_qa.reshape(1, -1), g_kva.reshape(1, -1)
    tabs = _rope_tables(t_len, A_ROPE, "mla")
    q, k, v = _mla_proj_call(x, mod, tabs, w_in_r, gq, w_qb_r, gk, w_kvb_r, q_scale, tm, "mla_proj")
    q_c, k_c, v_c = _mla_proj_call(ctx, mod_c, None, w_in_r, gq, w_qb_r, gk, w_kvb_r, q_scale, t_c, "mla_proj_ctx")
    rows = ((0, None, 0), (1, None, 1))
    maps = dict(n_groups=A_HEADS // 2, q_map=lambda gi, j: 2 * gi + j, k_map=lambda gi, j: 2 * gi + j,
                v_map=lambda gi: gi, rows=rows, out_width=A_HEADS * A_V, lambda_init=0.0)
    o = _dense_attn_call("pair", q, k_c, v_c, k, v, (), tq=tq, tk=tk, name="mla_attn", **maps)
    o_c = _dense_attn_call("pair", q_c, k_c, v_c, None, None, (), tq=t_c, tk=tk, name="mla_attn_ctx", **maps) \
        if need_ctx else None
    return o, w_gate, o_c


def _diff_layer(x, ctx, mod, mod_c, need_ctx, layer_idx, w_in, lam, g_sub):
    t_len, t_c = x.shape[1], ctx.shape[1]
    tm, tq, tk = _tiles(t_len, 2)
    n = 2 * B_HEADS * B_HEAD
    q_scale = B_HEAD ** -0.5 * LOG2E
    segs = ((0, 0, n, "rope", q_scale, None), (1, n, n, "rope", 1.0, None), (2, 2 * n, n, "plain", 1.0, None))
    widths = (n, n, n)
    wb = w_in[:, :3 * n].astype(BF16)
    w_gate = w_in[:, 3 * n:].astype(BF16)
    tabs = _rope_tables(t_len, B_HEAD, "unit")
    q, k, v = _proj_call(x, mod, tabs, wb, None, segs, widths, B_HEAD // 2, tm, "diff_proj")
    q_c, k_c, v_c = _proj_call(ctx, mod_c, None, wb, None, segs, widths, B_HEAD // 2, t_c, "diff_proj_ctx")
    lambda_init = 0.8 - 0.6 * math.exp(-0.3 * layer_idx)
    rows = ((0, "lo", 0), (0, "hi", 0))
    extras = (lam.astype(F32), g_sub.reshape(1, -1).astype(F32))
    maps = dict(n_groups=B_HEADS, q_map=lambda gi, j: gi, k_map=lambda gi, j: gi, v_map=lambda gi: gi,
                rows=rows, out_width=n, lambda_init=lambda_init)
    o = _dense_attn_call("diff", q, k_c, v_c, k, v, extras, tq=tq, tk=tk, name="diff_attn", **maps)
    o_c = _dense_attn_call("diff", q_c, k_c, v_c, None, None, extras, tq=t_c, tk=tk, name="diff_attn_ctx", **maps) \
        if need_ctx else None
    return o, w_gate, o_c


def _gqa_layer(x, ctx, mod, mod_c, need_ctx, w_in, g_q, g_k):
    t_len, t_c = x.shape[1], ctx.shape[1]
    tm, tq, tk = _tiles(t_len, C_HEADS // C_KV_HEADS)
    nq, nkv = C_HEADS * C_HEAD, C_KV_HEADS * C_HEAD
    q_scale = C_HEAD ** -0.5 * LOG2E
    segs = ((0, 0, nq, "rope", q_scale, 0), (1, nq, nkv, "rope", 1.0, 1), (2, nq + nkv, nkv, "plain", 1.0, None))
    widths = (nq, nkv, nkv)
    wb = w_in[:, :nq + 2 * nkv].astype(BF16)
    w_gate = w_in[:, nq + 2 * nkv:].astype(BF16)
    gains = jnp.stack([g_q, g_k]).astype(F32)
    tabs = _rope_tables(t_len, C_HEAD, "unit")
    q, k, v = _proj_call(x, mod, tabs, wb, gains, segs, widths, C_HEAD // 2, tm, "gqa_proj")
    q_c, k_c, v_c = _proj_call(ctx, mod_c, None, wb, gains, segs, widths, C_HEAD // 2, t_c, "gqa_proj_ctx")
    per = C_HEADS // C_KV_HEADS
    rows = tuple((j, None, 0) for j in range(per))
    maps = dict(n_groups=C_KV_HEADS, q_map=lambda gi, j: per * gi + j, k_map=lambda gi, j: gi, v_map=lambda gi: gi,
                rows=rows, out_width=nq, lambda_init=0.0)
    o = _dense_attn_call("stack", q, k_c, v_c, k, v, (), tq=tq, tk=tk, name="gqa_attn", **maps)
    o_c = _dense_attn_call("stack", q_c, k_c, v_c, None, None, (), tq=t_c, tk=tk, name="gqa_attn_ctx", **maps) \
        if need_ctx else None
    return o, w_gate, o_c


def _swa_layer(x, ctx, mod, mod_c, need_ctx, w_in, sinks):
    t_len, t_c = x.shape[1], ctx.shape[1]
    tm, _, _ = _tiles(t_len)
    nq, nkv = D_HEADS * D_HEAD, D_KV_HEADS * D_HEAD
    wk = w_in[:, nq:nq + nkv]
    wv = w_in[:, nq + nkv:nq + 2 * nkv]
    swap = lambda w: jnp.concatenate([w, w[:, D_HEAD:], w[:, :D_HEAD]], axis=1)
    wb = jnp.concatenate([w_in[:, :nq], swap(wk), swap(wv)], axis=1).astype(BF16)
    w_gate = w_in[:, nq + 2 * nkv:].astype(BF16)
    q_scale = D_HEAD ** -0.5 * LOG2E
    segs = ((0, 0, nq, "rope", q_scale, None), (1, nq, 2 * nkv, "rope", 1.0, None),
            (2, nq + 2 * nkv, 2 * nkv, "plain", 1.0, None))
    widths = (nq, 2 * nkv, 2 * nkv)
    tabs = _rope_tables(t_len, D_HEAD, "unit")
    q, k, v = _proj_call(x, mod, tabs, wb, None, segs, widths, D_HEAD // 2, tm, "swa_proj")
    q_c, k_c, v_c = _proj_call(ctx, mod_c, None, wb, None, segs, widths, D_HEAD // 2, t_c, "swa_proj_ctx")
    sinks_log2 = sinks.astype(F32) * LOG2E
    o = _swa_call(q, k_c, v_c, k, v, sinks_log2, WINDOW, "swa_attn")
    o_c = _swa_call(q_c, k_c, v_c, None, None, sinks_log2, t_c, "swa_attn_ctx") if need_ctx else None
    return o, w_gate, o_c


def kernel(x, c, ctx, c_ctx, ada_w, ada_b, out_w, ln_g, ln_b, mla_w_in, mla_g_qa, mla_w_qb, mla_g_kva, mla_w_kvb,
           diff_w_in, diff_lambda, diff_g_sub, gqa_w_in, gqa_g_q, gqa_g_k, swa_w_in, swa_sink):
    depth = ada_w.shape[0]
    bsz, t_len, d = x.shape
    t_c = ctx.shape[1]
    alpha = (2 * depth) ** 0.25
    n_rows = 8
    assert bsz + 1 <= n_rows
    cvec = jnp.concatenate([c, c_ctx[None, :], jnp.zeros((n_rows - bsz - 1, d), F32)], axis=0)
    mods = _adaln_call(cvec, ada_w, ada_b)
    tm_out = min(OUT_ROWS, t_len)
    for i in range(depth):
        kind, j = i % 4, i // 4
        need_ctx = i < depth - 1
        mod = mods[i, 0:bsz][:, None, :]
        mod_c = jnp.broadcast_to(mods[i, bsz][None, None, :], (bsz, 1, 3 * d))
        if kind == 0:
            o, w_gate, o_c = _mla_layer(x, ctx, mod, mod_c, need_ctx, mla_w_in[j], mla_g_qa[j], mla_w_qb[j],
                                        mla_g_kva[j], mla_w_kvb[j])
        elif kind == 1:
            o, w_gate, o_c = _diff_layer(x, ctx, mod, mod_c, need_ctx, i, diff_w_in[j], diff_lambda[j], diff_g_sub[j])
        elif kind == 2:
            o, w_gate, o_c = _gqa_layer(x, ctx, mod, mod_c, need_ctx, gqa_w_in[j], gqa_g_q[j], gqa_g_k[j])
        else:
            o, w_gate, o_c = _swa_layer(x, ctx, mod, mod_c, need_ctx, swa_w_in[j], swa_sink[j])
        wo = out_w[i].astype(BF16)
        x = _out_call(x, o, w_gate, mod, wo, ln_g[i], ln_b[i], alpha, tm_out, "out_proj")
        if need_ctx:
            ctx = _out_call(ctx, o_c, w_gate, mod_c, wo, ln_g[i], ln_b[i], alpha, t_c, "out_proj_ctx")
    return x
```

```python
import functools
import math

import jax
import jax.numpy as jnp
from jax import lax
from jax.experimental import pallas as pl
from jax.experimental.pallas import tpu as pltpu

D_MODEL = 1024
GRID_W = 64
WINDOW = 128
ROPE_THETA = 10000.0
NORM_EPS = 1e-6
A_HEADS, A_Q_LORA, A_KV_LORA, A_NOPE, A_ROPE, A_V = 16, 256, 128, 64, 32, 64
B_HEADS, B_HEAD = 8, 64
C_HEADS, C_KV_HEADS, C_HEAD = 8, 2, 128
D_HEADS, D_KV_HEADS, D_HEAD = 16, 2, 64

LANE = 128
HALF = LANE // 2
LOG2E = math.log2(math.e)
NEG_BIG = -1e30
VMEM_LIMIT = 56 * 1024 * 1024
BF16 = jnp.bfloat16
F32 = jnp.float32


def _cparams(n_grid):
    return pltpu.CompilerParams(dimension_semantics=("arbitrary",) * n_grid, vmem_limit_bytes=VMEM_LIMIT)


def _silu(x):
    return x / (1.0 + jnp.exp(-x))


def _dot(a, b):
    return jnp.dot(a, b, preferred_element_type=F32)


def _dot_nt(a, b):
    return lax.dot_general(a, b, (((1,), (1,)), ((), ())), preferred_element_type=F32)


def _adaln_kernel(c_ref, w_ref, b_ref, o_ref):
    sc = _silu(c_ref[...])
    o_ref[0] = jnp.dot(sc, w_ref[0], preferred_element_type=F32, precision=lax.Precision.HIGHEST) + b_ref[0]


def _adaln_call(cvec, ada_w, ada_b):
    depth, d, n = ada_w.shape
    rows = cvec.shape[0]
    tn = 1024
    return pl.pallas_call(
        _adaln_kernel,
        grid=(depth, n // tn),
        in_specs=[
            pl.BlockSpec((rows, d), lambda i, j: (0, 0)),
            pl.BlockSpec((1, d, tn), lambda i, j: (i, 0, j)),
            pl.BlockSpec((1, 1, tn), lambda i, j: (i, 0, j)),
        ],
        out_specs=pl.BlockSpec((1, rows, tn), lambda i, j: (i, 0, j)),
        out_shape=jax.ShapeDtypeStruct((depth, rows, n), F32),
        compiler_params=_cparams(2),
        name="adaln",
    )(cvec, ada_w, ada_b.reshape(depth, 1, n))


def _mla_slab(pe_lo, nope_a, pe_hi, nope_b):
    used = pe_lo.shape[-1] + nope_a.shape[-1] + pe_hi.shape[-1] + nope_b.shape[-1]
    pad = jnp.zeros(pe_lo.shape[:-1] + (LANE - used,), pe_lo.dtype)
    return jnp.concatenate([pe_lo, nope_a, pe_hi, nope_b, pad], axis=-1)


def _rope_tables(t_len, rot_dim, layout):
    n_freq = rot_dim // 4
    inv_freq = ROPE_THETA ** (-jnp.arange(n_freq, dtype=F32) / n_freq)

    def build(n_pos, is_row):
        ang = jnp.arange(n_pos, dtype=F32)[:, None] * inv_freq
        z = jnp.zeros((n_pos, n_freq), F32)
        cos = jnp.concatenate([jnp.cos(ang), z] if is_row else [z, jnp.cos(ang)], axis=-1)
        sin = jnp.concatenate([jnp.sin(ang), z] if is_row else [z, jnp.sin(ang)], axis=-1)
        one = 0.0 if is_row else 1.0
        if layout == "mla":
            na = HALF - A_ROPE // 2
            ones_a, ones_b = jnp.full((n_pos, na), one, F32), jnp.full((n_pos, A_NOPE - na), one, F32)
            return (_mla_slab(cos, ones_a, cos, ones_b), _mla_slab(-sin, 0 * ones_a, sin, 0 * ones_b))
        reps = LANE // rot_dim
        c = jnp.tile(jnp.concatenate([cos, cos], axis=-1), (1, reps))
        if rot_dim == LANE:
            return (c, jnp.concatenate([-sin, sin], axis=-1))
        zero = jnp.zeros_like(sin)
        return (c, jnp.tile(jnp.concatenate([-sin, zero], axis=-1), (1, reps)),
                jnp.tile(jnp.concatenate([zero, sin], axis=-1), (1, reps)))

    return build(t_len // GRID_W, True), build(GRID_W, False)


def _rope_block(row_refs, col_refs, g0, n_groups):
    tabs = []
    for rr, cr in zip(row_refs, col_refs):
        col_part = cr[...]
        tabs.append(jnp.concatenate(
            [jnp.broadcast_to(rr[g:g + 1, :], (GRID_W, LANE)) + col_part for g in range(g0, g0 + n_groups)], axis=0))
    return tabs


def _rope_slab(x, tabs, half):
    if half == HALF:
        c, sn = tabs
        return x * c + pltpu.roll(x, HALF, 1) * sn
    c, s_lo, s_hi = tabs
    return x * c + pltpu.roll(x, LANE - half, 1) * s_lo + pltpu.roll(x, half, 1) * s_hi


def _rms(x, g):
    return x * lax.rsqrt(jnp.mean(x * x, axis=-1, keepdims=True) + NORM_EPS) * g


PROJ_CHUNK = 512
PROJ_SUB_ROWS = 256


def _proj_kernel(*refs, segs, rope_half, has_rope, has_norm):
    it = iter(refs)
    x_ref, mod_ref = next(it), next(it)
    if has_rope:
        n_tabs = 2 if rope_half == HALF else 3
        row_refs = [next(it) for _ in range(n_tabs)]
        col_refs = [next(it) for _ in range(n_tabs)]
    w_ref = next(it)
    gn_ref = next(it) if has_norm else None
    out_refs = list(it)
    tm, d = x_ref.shape[1], x_ref.shape[2]
    shift, scale = mod_ref[0, :, 0:d], mod_ref[0, :, d:2 * d]
    rs = min(tm, PROJ_SUB_ROWS)
    hbs = [(x_ref[0, r0:r0 + rs, :] * (1.0 + scale) + shift).astype(BF16) for r0 in range(0, tm, rs)]
    if has_rope:
        tabs = [_rope_block(row_refs, col_refs, ri * (rs // GRID_W), rs // GRID_W) for ri in range(tm // rs)]
    items = [(ri, seg, cc, min(PROJ_CHUNK, seg[2] - cc))
             for ri in range(tm // rs) for seg in segs for cc in range(0, seg[2], PROJ_CHUNK)]

    def matmul(item):
        ri, (_, col0, _, _, _, _), cc, cw = item
        return _dot(hbs[ri], w_ref[:, col0 + cc:col0 + cc + cw])

    def finish(item, t):
        ri, (oi, _, _, kind, q_scale, norm_row), cc, cw = item
        rows = slice(ri * rs, (ri + 1) * rs)
        if kind == "plain":
            y = t
        else:
            slabs = []
            for s in range(cw // LANE):
                xs = t[:, s * LANE:(s + 1) * LANE]
                if norm_row is not None:
                    xs = _rms(xs, gn_ref[norm_row:norm_row + 1, :])
                if has_rope:
                    xs = _rope_slab(xs, tabs[ri], rope_half)
                if q_scale != 1.0:
                    xs = xs * q_scale
                slabs.append(xs)
            y = jnp.concatenate(slabs, axis=1) if len(slabs) > 1 else slabs[0]
        out_refs[oi][0, rows, cc:cc + cw] = y.astype(BF16)

    t_cur = matmul(items[0])
    for n, item in enumerate(items):
        t_next = matmul(items[n + 1]) if n + 1 < len(items) else None
        finish(item, t_cur)
        t_cur = t_next


def _proj_call(x, mod, tabs, w, gains, segs, out_widths, rope_half, tm, name):
    bsz, t_len, d = x.shape
    has_rope = tabs is not None
    has_norm = gains is not None
    args = [x, mod]
    in_specs = [
        pl.BlockSpec((1, tm, d), lambda b, i: (b, i, 0)),
        pl.BlockSpec((1, 1, mod.shape[-1]), lambda b, i: (b, 0, 0)),
    ]
    if has_rope:
        row_tabs, col_tabs = tabs
        for tab in row_tabs:
            args.append(tab)
            in_specs.append(pl.BlockSpec((tm // GRID_W, LANE), lambda b, i: (i, 0)))
        for tab in col_tabs:
            args.append(tab)
            in_specs.append(pl.BlockSpec((GRID_W, LANE), lambda b, i: (0, 0)))
    args.append(w)
    in_specs.append(pl.BlockSpec(w.shape, lambda b, i: (0, 0)))
    if has_norm:
        args.append(gains)
        in_specs.append(pl.BlockSpec(gains.shape, lambda b, i: (0, 0)))
    return pl.pallas_call(
        functools.partial(_proj_kernel, segs=segs, rope_half=rope_half, has_rope=has_rope, has_norm=has_norm),
        grid=(bsz, t_len // tm),
        in_specs=in_specs,
        out_specs=[pl.BlockSpec((1, tm, ow), lambda b, i: (b, i, 0)) for ow in out_widths],
        out_shape=[jax.ShapeDtypeStruct((bsz, t_len, ow), BF16) for ow in out_widths],
        compiler_params=_cparams(2),
        name=name,
    )(*args)


def _mla_proj_kernel(*refs, has_rope, q_scale):
    it = iter(refs)
    x_ref, mod_ref = next(it), next(it)
    if has_rope:
        row_refs = [next(it), next(it)]
        col_refs = [next(it), next(it)]
    w_in_ref, gqa_ref, w_qb_ref, gkva_ref, w_kvb_ref = next(it), next(it), next(it), next(it), next(it)
    q_ref, k_ref, v_ref = next(it), next(it), next(it)
    half = HALF
    n_lat =A_Q_LORA + A_KV_LORA + LANE
    n_qk = A_HEADS * LANE
    n_v = A_HEADS * A_V
    tm, d = x_ref.shape[1], x_ref.shape[2]
    shift, scale = mod_ref[0, :, 0:d], mod_ref[0, :, d:2 * d]
    rs = min(tm, PROJ_SUB_ROWS)
    for r0 in range(0, tm, rs):
        rows = slice(r0, r0 + rs)
        hb = (x_ref[0, rows, :] * (1.0 + scale) + shift).astype(BF16)
        t = _dot(hb, w_in_ref[...])
        qn = _rms(t[:, 0:A_Q_LORA], gqa_ref[...]).astype(BF16)
        kn = _rms(t[:, A_Q_LORA:A_Q_LORA + A_KV_LORA], gkva_ref[...]).astype(BF16)
        kpe = t[:, A_Q_LORA + A_KV_LORA:n_lat]
        if has_rope:
            tabs = _rope_block(row_refs, col_refs, r0 // GRID_W, rs // GRID_W)
            kpe = _rope_slab(kpe, tabs, half)
        for cc in range(0, n_qk, PROJ_CHUNK):
            tq = _dot(qn, w_qb_ref[:, cc:cc + PROJ_CHUNK])
            tk = _dot(kn, w_kvb_ref[:, cc:cc + PROJ_CHUNK])
            qs, ks = [], []
            for s in range(PROJ_CHUNK // LANE):
                xq = tq[:, s * LANE:(s + 1) * LANE]
                if has_rope:
                    xq = _rope_slab(xq, tabs, half)
                qs.append(xq * q_scale)
                ks.append(tk[:, s * LANE:(s + 1) * LANE] + kpe)
            q_ref[0, rows, cc:cc + PROJ_CHUNK] = jnp.concatenate(qs, axis=1).astype(BF16)
            k_ref[0, rows, cc:cc + PROJ_CHUNK] = jnp.concatenate(ks, axis=1).astype(BF16)
        for cc in range(0, n_v, PROJ_CHUNK):
            v_ref[0, rows, cc:cc + PROJ_CHUNK] = _dot(kn, w_kvb_ref[:, n_qk + cc:n_qk + cc + PROJ_CHUNK]).astype(BF16)


def _mla_proj_call(x, mod, tabs, w_in, g_qa, w_qb, g_kva, w_kvb, q_scale, tm, name):
    bsz, t_len, d = x.shape
    has_rope = tabs is not None
    args = [x, mod]
    in_specs = [
        pl.BlockSpec((1, tm, d), lambda b, i: (b, i, 0)),
        pl.BlockSpec((1, 1, mod.shape[-1]), lambda b, i: (b, 0, 0)),
    ]
    if has_rope:
        row_tabs, col_tabs = tabs
        for tab in row_tabs:
            args.append(tab)
            in_specs.append(pl.BlockSpec((tm // GRID_W, LANE), lambda b, i: (i, 0)))
        for tab in col_tabs:
            args.append(tab)
            in_specs.append(pl.BlockSpec((GRID_W, LANE), lambda b, i: (0, 0)))
    for a in (w_in, g_qa, w_qb, g_kva, w_kvb):
        args.append(a)
        in_specs.append(pl.BlockSpec(a.shape, lambda b, i: (0, 0)))
    out_widths = (A_HEADS * LANE, A_HEADS * LANE, A_HEADS * A_V)
    return pl.pallas_call(
        functools.partial(_mla_proj_kernel, has_rope=has_rope, q_scale=q_scale),
        grid=(bsz, t_len // tm),
        in_specs=in_specs,
        out_specs=[pl.BlockSpec((1, tm, ow), lambda b, i: (b, i, 0)) for ow in out_widths],
        out_shape=[jax.ShapeDtypeStruct((bsz, t_len, ow), BF16) for ow in out_widths],
        compiler_params=_cparams(2),
        name=name,
    )(*args)


def _lane_lo(shape):
    return lax.broadcasted_iota(jnp.int32, shape, len(shape) - 1) < HALF


def _dense_attn_kernel(*refs, mode, n_q, n_k, rows, tq, tk, n_chunks, n_parts, unroll, lambda_init):
    it = iter(refs)
    q_refs = [next(it) for _ in range(n_q)]
    kc_refs = [next(it) for _ in range(n_k)]
    vc_ref = next(it)
    if n_chunks:
        k_refs = [next(it) for _ in range(n_k)]
        v_ref = next(it)
    if mode == "diff":
        lam_ref, gsub_ref = next(it), next(it)
    o_ref = next(it)
    qs_scr, m_scr, acc_scr = next(it), next(it), next(it)
    buf_a, buf_b = next(it), next(it)

    n_rows = len(rows)
    for r, (qi, mask, _) in enumerate(rows):
        q = q_refs[qi][0]
        if mask is not None:
            lo = _lane_lo(q.shape)
            q = jnp.where(lo if mask == "lo" else jnp.logical_not(lo), q, jnp.zeros_like(q))
        qs_scr[r * tq:(r + 1) * tq, :] = q
    spans = []
    for ks in range(n_k):
        idx = [r for r, row in enumerate(rows) if row[2] == ks]
        spans.append((idx[0] * tq, (idx[-1] + 1) * tq))
    blocks_per_span = max(1, n_parts // n_k)
    row_blocks = []
    for (r0, r1) in spans:
        nb = (r1 - r0) // blocks_per_span
        row_blocks += [(r0 + i * nb, r0 + (i + 1) * nb) for i in range(blocks_per_span)]

    def qk_part(k_load, s_ref, c0, c1):
        for ks, (r0, r1) in enumerate(spans):
            s_ref[r0:r1, c0:c1] = _dot_nt(qs_scr[r0:r1, :], k_load(ks, c0, c1))

    def sm_part(v_ext, s_ref, a, b):
        n_cols = v_ext.shape[0] // LANE
        mp = s_ref[a:b, 0:LANE]
        for j in range(1, n_cols):
            mp = jnp.maximum(mp, s_ref[a:b, j * LANE:(j + 1) * LANE])
        m_prev = m_scr[a:b, :]
        m_new = jnp.maximum(m_prev, jnp.max(mp, axis=1, keepdims=True))
        ps = [jnp.exp2(s_ref[a:b, j * LANE:(j + 1) * LANE] - m_new).astype(BF16) for j in range(n_cols)]
        p = jnp.concatenate(ps, axis=1) if n_cols > 1 else ps[0]
        pv = _dot(p, v_ext)
        alpha = jnp.exp2(m_prev - m_new)
        acc_scr[a:b, :] = jnp.concatenate([alpha, alpha], axis=1) * acc_scr[a:b, :] + pv
        m_scr[a:b, :] = m_new

    def ext(v):
        return jnp.concatenate([v, jnp.ones(v.shape, BF16)], axis=1)

    def k_loader(off):
        return lambda ks, c0, c1: k_refs[ks][0, pl.ds(off + c0, c1 - c0), :]

    def v_chunk(off):
        return ext(v_ref[0, pl.ds(off, tk), :])

    def pipelined(v_ext, s_cur, k_next, s_next, next_width):
        kw = max(LANE, next_width // n_parts)
        for pi, (a, b) in enumerate(row_blocks):
            sm_part(v_ext, s_cur, a, b)
            if pi * kw < next_width:
                qk_part(k_next, s_next, pi * kw, (pi + 1) * kw)

    bufs = (buf_a, buf_b)
    t_c = vc_ref.shape[1]
    ctx_loader = lambda ks, c0, c1: kc_refs[ks][0, c0:c1, :]
    if n_chunks:
        qk_part(k_loader(0), bufs[0], 0, tk)
    else:
        qk_part(ctx_loader, bufs[0], 0, t_c)
    m_scr[...] = jnp.full(m_scr.shape, NEG_BIG, F32)
    acc_scr[...] = jnp.zeros(acc_scr.shape, F32)
    if n_chunks:
        n_trips = (n_chunks - 1) // unroll

        def step(ti, carry):
            base = ti * (unroll * tk)
            for u in range(unroll):
                off = pl.multiple_of(base + u * tk, tk)
                nxt = pl.multiple_of(base + (u + 1) * tk, tk)
                pipelined(v_chunk(off), bufs[u % 2], k_loader(nxt), bufs[(u + 1) % 2], tk)
            return carry
        if n_trips:
            lax.fori_loop(0, n_trips, step, 0)
        for c in range(unroll * n_trips, n_chunks):
            if c == n_chunks - 1:
                pipelined(v_chunk(c * tk), bufs[c % 2], ctx_loader, bufs[(c + 1) % 2], t_c)
            else:
                pipelined(v_chunk(c * tk), bufs[c % 2], k_loader((c + 1) * tk), bufs[(c + 1) % 2], tk)
    vc_ext = ext(vc_ref[0])
    for (a, b) in row_blocks:
        sm_part(vc_ext, bufs[n_chunks % 2], a, b)

    acc = acc_scr[...]
    o = acc[:, 0:LANE] / acc[:, LANE:2 * LANE]
    if mode == "pair":
        out = jnp.where(_lane_lo((tq, LANE)), o[0:tq], o[tq:2 * tq])
    elif mode == "diff":
        lam = lam_ref[...]
        a = jnp.sum(lam[0:1, :] * lam[1:2, :], axis=1, keepdims=True)
        b = jnp.sum(lam[2:3, :] * lam[3:4, :], axis=1, keepdims=True)
        lam_full = jnp.exp(a) - jnp.exp(b) + lambda_init
        dlt = o[0:tq] - lam_full * o[tq:2 * tq]
        out = _rms(dlt, gsub_ref[...]) * (1.0 - lambda_init)
    else:
        out = jnp.concatenate([o[r * tq:(r + 1) * tq] for r in range(n_rows)], axis=1)
    o_ref[0] = out.astype(o_ref.dtype)


def _dense_attn_call(mode, q, kc, vc, k, v, extras, n_groups, q_map, k_map, v_map, rows, out_width, tq, tk,
                     lambda_init, name):
    bsz, t_q, _ = q.shape
    t_c = kc.shape[1]
    n_q = 1 + max(r[0] for r in rows)
    n_k = 1 + max(r[2] for r in rows)
    n_rows = len(rows)
    has_latent = k is not None
    n_chunks = (k.shape[1] // tk) if has_latent else 0
    args, in_specs = [], []
    for j in range(n_q):
        args.append(q)
        in_specs.append(pl.BlockSpec((1, tq, LANE), lambda b, g, i, j=j: (b, i, q_map(g, j))))
    for j in range(n_k):
        args.append(kc)
        in_specs.append(pl.BlockSpec((1, t_c, LANE), lambda b, g, i, j=j: (b, 0, k_map(g, j))))
    args.append(vc)
    in_specs.append(pl.BlockSpec((1, t_c, LANE), lambda b, g, i: (b, 0, v_map(g))))
    if has_latent:
        t_k = k.shape[1]
        for j in range(n_k):
            args.append(k)
            in_specs.append(pl.BlockSpec((1, t_k, LANE), lambda b, g, i, j=j: (b, 0, k_map(g, j))))
        args.append(v)
        in_specs.append(pl.BlockSpec((1, t_k, LANE), lambda b, g, i: (b, 0, v_map(g))))
    for e in extras:
        args.append(e)
        in_specs.append(pl.BlockSpec(e.shape, lambda b, g, i: (0, 0)))
    ow = out_width // n_groups
    s_width = max(tk, t_c) if has_latent else t_c
    return pl.pallas_call(
        functools.partial(_dense_attn_kernel, mode=mode, n_q=n_q, n_k=n_k, rows=rows, tq=tq, tk=tk,
                          n_chunks=n_chunks, n_parts=ATTN_PARTS, unroll=ATTN_UNROLL, lambda_init=lambda_init),
        grid=(bsz, n_groups, t_q // tq),
        in_specs=in_specs,
        out_specs=pl.BlockSpec((1, tq, ow), lambda b, g, i: (b, i, g)),
        out_shape=jax.ShapeDtypeStruct((bsz, t_q, out_width), BF16),
        scratch_shapes=[
            pltpu.VMEM((n_rows * tq, LANE), BF16),
            pltpu.VMEM((n_rows * tq, LANE), F32),
            pltpu.VMEM((n_rows * tq, 2 * LANE), F32),
            pltpu.VMEM((n_rows * tq, s_width), F32),
            pltpu.VMEM((n_rows * tq, s_width), F32),
        ],
        compiler_params=_cparams(3),
        name=name,
    )(*args)


def _swa_kernel(*refs, latent, tq, n_blocks):
    it = iter(refs)
    sink_ref = next(it)
    q_ref = next(it)
    kc_ref, vc_ref = next(it), next(it)
    if latent:
        kp_ref, k0_ref, kn_ref = next(it), next(it), next(it)
        vp_ref, v0_ref, vn_ref = next(it), next(it), next(it)
    o_ref = next(it)
    i = pl.program_id(1)
    per_g = D_HEADS // D_KV_HEADS
    n_pairs = per_g // 2
    lo = _lane_lo((tq, LANE))
    if latent:
        r = lax.broadcasted_iota(jnp.int32, (tq, tq), 0)
        cidx = lax.broadcasted_iota(jnp.int32, (tq, tq), 1)
        ok_prev = jnp.logical_and(cidx >= r, i > 0)
        ok_next = jnp.logical_and(cidx <= r, i < n_blocks - 1)
        ok_pc = jnp.concatenate([ok_prev, jnp.full((tq, tq), True)], axis=1)
        ok_pc = jnp.concatenate([ok_pc] * n_pairs, axis=0)
        ok_next = jnp.concatenate([ok_next] * n_pairs, axis=0)

    def scores(g, e):
        slab = g ^ e
        sl = slice(slab * LANE, (slab + 1) * LANE)
        qrows, sinks = [], []
        for pi in range(n_pairs):
            ps = g * n_pairs + pi
            qv = q_ref[0, :, ps * LANE:(ps + 1) * LANE]
            qrows.append(jnp.where(lo if e == 0 else jnp.logical_not(lo), qv, jnp.zeros_like(qv)))
            sinks.append(jnp.full((tq, LANE), sink_ref[g * per_g + 2 * pi + e], F32))
        qst = jnp.concatenate(qrows, axis=0)
        sink = jnp.concatenate(sinks, axis=0)
        s_list = [_dot_nt(qst, kc_ref[0, :, sl])]
        if latent:
            k_pc = jnp.concatenate([kp_ref[0, :, sl], k0_ref[0, :, sl]], axis=0)
            s_list.append(jnp.where(ok_pc, _dot_nt(qst, k_pc), NEG_BIG))
            s_list.append(jnp.where(ok_next, _dot_nt(qst, kn_ref[0, :, sl]), NEG_BIG))
        return jnp.concatenate(s_list, axis=1), sink, sl

    def softmax_pv(s, sink, sl):
        v_list = [vc_ref[0, :, sl]]
        if latent:
            v_list += [vp_ref[0, :, sl], v0_ref[0, :, sl], vn_ref[0, :, sl]]
        vv = jnp.concatenate(v_list, axis=0)
        vv = jnp.concatenate([vv, jnp.ones(vv.shape, BF16)], axis=1)
        m = jnp.maximum(jnp.max(s, axis=1, keepdims=True), sink)
        ps = [jnp.exp2(s[:, j * LANE:(j + 1) * LANE] - m).astype(BF16) for j in range(s.shape[1] // LANE)]
        pv = _dot(jnp.concatenate(ps, axis=1), vv)
        return pv[:, 0:LANE] / (pv[:, LANE:2 * LANE] + jnp.exp2(sink - m))

    groups = [(g, e) for g in range(D_KV_HEADS) for e in range(2)]
    outs = []
    pending = scores(*groups[0])
    for gi in range(len(groups)):
        nxt = scores(*groups[gi + 1]) if gi + 1 < len(groups) else None
        outs.append(softmax_pv(*pending))
        pending = nxt
    for g in range(D_KV_HEADS):
        for pi in range(n_pairs):
            ps = g * n_pairs + pi
            out = jnp.where(lo, outs[2 * g][pi * tq:(pi + 1) * tq], outs[2 * g + 1][pi * tq:(pi + 1) * tq])
            o_ref[0, :, ps * LANE:(ps + 1) * LANE] = out.astype(o_ref.dtype)


def _swa_call(q, kc, vc, k, v, sinks_log2, tq, name):
    bsz, t_q, width = q.shape
    t_c = kc.shape[1]
    latent = k is not None
    n_blocks = t_q // tq
    args = [sinks_log2, q, kc, vc]
    in_specs = [
        pl.BlockSpec(memory_space=pltpu.SMEM),
        pl.BlockSpec((1, tq, width), lambda b, i: (b, i, 0)),
        pl.BlockSpec((1, t_c, 2 * LANE), lambda b, i: (b, 0, 0)),
        pl.BlockSpec((1, t_c, 2 * LANE), lambda b, i: (b, 0, 0)),
    ]
    if latent:
        assert tq == WINDOW
        prev_map = lambda b, i: (b, jnp.maximum(i - 1, 0), 0)
        cur_map = lambda b, i: (b, i, 0)
        next_map = lambda b, i: (b, jnp.minimum(i + 1, n_blocks - 1), 0)
        for arr in (k, v):
            for mp in (prev_map, cur_map, next_map):
                args.append(arr)
                in_specs.append(pl.BlockSpec((1, tq, 2 * LANE), mp))
    return pl.pallas_call(
        functools.partial(_swa_kernel, latent=latent, tq=tq, n_blocks=n_blocks),
        grid=(bsz, n_blocks),
        in_specs=in_specs,
        out_specs=pl.BlockSpec((1, tq, width), lambda b, i: (b, i, 0)),
        out_shape=jax.ShapeDtypeStruct((bsz, t_q, width), BF16),
        compiler_params=_cparams(2),
        name=name,
    )(*args)


def _out_kernel(x_ref, o_ref, mod_ref, wg_ref, w_ref, lng_ref, lnb_ref, y_ref, *, alpha):
    tm, d = x_ref.shape[1], x_ref.shape[2]
    shift, scale, gate = mod_ref[0, :, 0:d], mod_ref[0, :, d:2 * d], mod_ref[0, :, 2 * d:3 * d]
    rs = min(tm, OUT_SUB_ROWS)
    n_sub = tm // rs

    def branch(r0):
        xs = x_ref[0, r0:r0 + rs, :]
        hb = (xs * (1.0 + scale) + shift).astype(BF16)
        g = _silu(_dot(hb, wg_ref[...]))
        u = (o_ref[0, r0:r0 + rs, :].astype(F32) * g).astype(BF16)
        return alpha * xs + gate * _dot(u, w_ref[...])

    def norm_store(r0, z):
        zc = z - jnp.mean(z, axis=-1, keepdims=True)
        var = jnp.mean(zc * zc, axis=-1, keepdims=True)
        y_ref[0, r0:r0 + rs, :] = zc * lax.rsqrt(var + NORM_EPS) * lng_ref[...] + lnb_ref[...]

    zs = [branch(i * rs) for i in range(n_sub)]
    for i in range(n_sub):
        norm_store(i * rs, zs[i])


def _out_call(x, o, w_gate, mod, w, ln_g, ln_b, alpha, tm, name):
    bsz, t_len, d = x.shape
    width = o.shape[-1]
    return pl.pallas_call(
        functools.partial(_out_kernel, alpha=alpha),
        grid=(bsz, t_len // tm),
        in_specs=[
            pl.BlockSpec((1, tm, d), lambda b, i: (b, i, 0)),
            pl.BlockSpec((1, tm, width), lambda b, i: (b, i, 0)),
            pl.BlockSpec((1, 1, mod.shape[-1]), lambda b, i: (b, 0, 0)),
            pl.BlockSpec(w_gate.shape, lambda b, i: (0, 0)),
            pl.BlockSpec(w.shape, lambda b, i: (0, 0)),
            pl.BlockSpec((1, d), lambda b, i: (0, 0)),
            pl.BlockSpec((1, d), lambda b, i: (0, 0)),
        ],
        out_specs=pl.BlockSpec((1, tm, d), lambda b, i: (b, i, 0)),
        out_shape=jax.ShapeDtypeStruct((bsz, t_len, d), F32),
        compiler_params=_cparams(2),
        name=name,
    )(x, o, mod, w_gate, w, ln_g.reshape(1, d), ln_b.reshape(1, d))


OUT_ROWS = 1024
OUT_SUB_ROWS = 256
ATTN_ROWS = 1024
ATTN_TK = 1024
ATTN_UNROLL = 16
ATTN_PARTS = 4


def _tiles(t_len, n_row_groups=1):
    tm = min(512, t_len)
    tq = min(ATTN_ROWS // n_row_groups, t_len)
    tk = min(ATTN_TK, t_len)
    return tm, tq, tk


def _mla_layer(x, ctx, mod, mod_c, need_ctx, w_in, g_qa, w_qb, g_kva, w_kvb):
    t_len, t_c = x.shape[1], ctx.shape[1]
    tm, tq, tk = _tiles(t_len, 2)
    d = w_in.shape[0]
    o0, o1, o2 = A_Q_LORA, A_Q_LORA + A_KV_LORA, A_Q_LORA + A_KV_LORA + A_ROPE
    hr = A_ROPE // 2
    na = HALF - hr
    w_kpe = w_in[:, o1:o2]
    kpe_pad = _mla_slab(w_kpe[:, :hr], jnp.zeros((d, na), F32), w_kpe[:, hr:], jnp.zeros((d, A_NOPE - na), F32))
    w_in_r = jnp.concatenate([w_in[:, :o1], kpe_pad], axis=1).astype(BF16)
    w_gate = w_in[:, o2:].astype(BF16)
    qb = w_qb.reshape(A_Q_LORA, A_HEADS, A_NOPE + A_ROPE)
    w_qb_r = _mla_slab(qb[:, :, A_NOPE:A_NOPE + hr], qb[:, :, :na], qb[:, :, A_NOPE + hr:], qb[:, :, na:A_NOPE])
    w_qb_r = w_qb_r.reshape(A_Q_LORA, A_HEADS * LANE).astype(BF16)
    kvb = w_kvb.reshape(A_KV_LORA, A_HEADS, A_NOPE + A_V)
    zk = jnp.zeros((A_KV_LORA, A_HEADS, hr), F32)
    k_pad = _mla_slab(zk, kvb[:, :, :na], zk, kvb[:, :, na:A_NOPE]).reshape(A_KV_LORA, A_HEADS * LANE)
    w_kvb_r = jnp.concatenate([k_pad, kvb[:, :, A_NOPE:].reshape(A_KV_LORA, A_HEADS * A_V)], axis=1).astype(BF16)
    q_scale = (A_NOPE + A_ROPE) ** -0.5 * LOG2E
    gq, gk = g_qa.reshape(1, -1), g_kva.reshape(1, -1)
    tabs = _rope_tables(t_len, A_ROPE, "mla")
    q, k, v = _mla_proj_call(x, mod, tabs, w_in_r, gq, w_qb_r, gk, w_kvb_r, q_scale, tm, "mla_proj")
    q_c, k_c, v_c = _mla_proj_call(ctx, mod_c, None, w_in_r, gq, w_qb_r, gk, w_kvb_r, q_scale, t_c, "mla_proj_ctx")
    rows = ((0, None, 0), (1, None, 1))
    maps = dict(n_groups=A_HEADS // 2, q_map=lambda gi, j: 2 * gi + j, k_map=lambda gi, j: 2 * gi + j,
                v_map=lambda gi: gi, rows=rows, out_width=A_HEADS * A_V, lambda_init=0.0)
    o = _dense_attn_call("pair", q, k_c, v_c, k, v, (), tq=tq, tk=tk, name="mla_attn", **maps)
    o_c = _dense_attn_call("pair", q_c, k_c, v_c, None, None, (), tq=t_c, tk=tk, name="mla_attn_ctx", **maps) \
        if need_ctx else None
    return o, w_gate, o_c


def _diff_layer(x, ctx, mod, mod_c, need_ctx, layer_idx, w_in, lam, g_sub):
    t_len, t_c = x.shape[1], ctx.shape[1]
    tm, tq, tk = _tiles(t_len, 2)
    n = 2 * B_HEADS * B_HEAD
    q_scale = B_HEAD ** -0.5 * LOG2E
    segs = ((0, 0, n, "rope", q_scale, None), (1, n, n, "rope", 1.0, None), (2, 2 * n, n, "plain", 1.0, None))
    widths = (n, n, n)
    wb = w_in[:, :3 * n].astype(BF16)
    w_gate = w_in[:, 3 * n:].astype(BF16)
    tabs = _rope_tables(t_len, B_HEAD, "unit")
    q, k, v = _proj_call(x, mod, tabs, wb, None, segs, widths, B_HEAD // 2, tm, "diff_proj")
    q_c, k_c, v_c = _proj_call(ctx, mod_c, None, wb, None, segs, widths, B_HEAD // 2, t_c, "diff_proj_ctx")
    lambda_init = 0.8 - 0.6 * math.exp(-0.3 * layer_idx)
    rows = ((0, "lo", 0), (0, "hi", 0))
    extras = (lam.astype(F32), g_sub.reshape(1, -1).astype(F32))
    maps = dict(n_groups=B_HEADS, q_map=lambda gi, j: gi, k_map=lambda gi, j: gi, v_map=lambda gi: gi,
                rows=rows, out_width=n, lambda_init=lambda_init)
    o = _dense_attn_call("diff", q, k_c, v_c, k, v, extras, tq=tq, tk=tk, name="diff_attn", **maps)
    o_c = _dense_attn_call("diff", q_c, k_c, v_c, None, None, extras, tq=t_c, tk=tk, name="diff_attn_ctx", **maps) \
        if need_ctx else None
    return o, w_gate, o_c


def _gqa_layer(x, ctx, mod, mod_c, need_ctx, w_in, g_q, g_k):
    t_len, t_c = x.shape[1], ctx.shape[1]
    tm, tq, tk = _tiles(t_len, C_HEADS // C_KV_HEADS)
    nq, nkv = C_HEADS * C_HEAD, C_KV_HEADS * C_HEAD
    q_scale = C_HEAD ** -0.5 * LOG2E
    segs = ((0, 0, nq, "rope", q_scale, 0), (1, nq, nkv, "rope", 1.0, 1), (2, nq + nkv, nkv, "plain", 1.0, None))
    widths = (nq, nkv, nkv)
    wb = w_in[:, :nq + 2 * nkv].astype(BF16)
    w_gate = w_in[:, nq + 2 * nkv:].astype(BF16)
    gains = jnp.stack([g_q, g_k]).astype(F32)
    tabs = _rope_tables(t_len, C_HEAD, "unit")
    q, k, v = _proj_call(x, mod, tabs, wb, gains, segs, widths, C_HEAD // 2, tm, "gqa_proj")
    q_c, k_c, v_c = _proj_call(ctx, mod_c, None, wb, gains, segs, widths, C_HEAD // 2, t_c, "gqa_proj_ctx")
    per = C_HEADS // C_KV_HEADS
    rows = tuple((j, None, 0) for j in range(per))
    maps = dict(n_groups=C_KV_HEADS, q_map=lambda gi, j: per * gi + j, k_map=lambda gi, j: gi, v_map=lambda gi: gi,
                rows=rows, out_width=nq, lambda_init=0.0)
    o = _dense_attn_call("stack", q, k_c, v_c, k, v, (), tq=tq, tk=tk, name="gqa_attn", **maps)
    o_c = _dense_attn_call("stack", q_c, k_c, v_c, None, None, (), tq=t_c, tk=tk, name="gqa_attn_ctx", **maps) \
        if need_ctx else None
    return o, w_gate, o_c


def _swa_layer(x, ctx, mod, mod_c, need_ctx, w_in, sinks):
    t_len, t_c = x.shape[1], ctx.shape[1]
    tm, _, _ = _tiles(t_len)
    nq, nkv = D_HEADS * D_HEAD, D_KV_HEADS * D_HEAD
    wk = w_in[:, nq:nq + nkv]
    wv = w_in[:, nq + nkv:nq + 2 * nkv]
    swap = lambda w: jnp.concatenate([w, w[:, D_HEAD:], w[:, :D_HEAD]], axis=1)
    wb = jnp.concatenate([w_in[:, :nq], swap(wk), swap(wv)], axis=1).astype(BF16)
    w_gate = w_in[:, nq + 2 * nkv:].astype(BF16)
    q_scale = D_HEAD ** -0.5 * LOG2E
    segs = ((0, 0, nq, "rope", q_scale, None), (1, nq, 2 * nkv, "rope", 1.0, None),
            (2, nq + 2 * nkv, 2 * nkv, "plain", 1.0, None))
    widths = (nq, 2 * nkv, 2 * nkv)
    tabs = _rope_tables(t_len, D_HEAD, "unit")
    q, k, v = _proj_call(x, mod, tabs, wb, None, segs, widths, D_HEAD // 2, tm, "swa_proj")
    q_c, k_c, v_c = _proj_call(ctx, mod_c, None, wb, None, segs, widths, D_HEAD // 2, t_c, "swa_proj_ctx")
    sinks_log2 = sinks.astype(F32) * LOG2E
    o = _swa_call(q, k_c, v_c, k, v, sinks_log2, WINDOW, "swa_attn")
    o_c = _swa_call(q_c, k_c, v_c, None, None, sinks_log2, t_c, "swa_attn_ctx") if need_ctx else None
    return o, w_gate, o_c


def kernel(x, c, ctx, c_ctx, ada_w, ada_b, out_w, ln_g, ln_b, mla_w_in, mla_g_qa, mla_w_qb, mla_g_kva, mla_w_kvb,
           diff_w_in, diff_lambda, diff_g_sub, gqa_w_in, gqa_g_q, gqa_g_k, swa_w_in, swa_sink):
    depth = ada_w.shape[0]
    bsz, t_len, d = x.shape
    t_c = ctx.shape[1]
    alpha = (2 * depth) ** 0.25
    n_rows = 8
    assert bsz + 1 <= n_rows
    cvec = jnp.concatenate([c, c_ctx[None, :], jnp.zeros((n_rows - bsz - 1, d), F32)], axis=0)
    mods = _adaln_call(cvec, ada_w, ada_b)
    tm_out = min(OUT_ROWS, t_len)
    for i in range(depth):
        kind, j = i % 4, i // 4
        need_ctx = i < depth - 1
        mod = mods[i, 0:bsz][:, None, :]
        mod_c = jnp.broadcast_to(mods[i, bsz][None, None, :], (bsz, 1, 3 * d))
        if kind == 0:
            o, w_gate, o_c = _mla_layer(x, ctx, mod, mod_c, need_ctx, mla_w_in[j], mla_g_qa[j], mla_w_qb[j],
                                        mla_g_kva[j], mla_w_kvb[j])
        elif kind == 1:
            o, w_gate, o_c = _diff_layer(x, ctx, mod, mod_c, need_ctx, i, diff_w_in[j], diff_lambda[j], diff_g_sub[j])
        elif kind == 2:
            o, w_gate, o_c = _gqa_layer(x, ctx, mod, mod_c, need_ctx, gqa_w_in[j], gqa_g_q[j], gqa_g_k[j])
        else:
            o, w_gate, o_c = _swa_layer(x, ctx, mod, mod_c, need_ctx, swa_w_in[j], swa_sink[j])
        wo = out_w[i].astype(BF16)
        x = _out_call(x, o, w_gate, mod, wo, ln_g[i], ln_b[i], alpha, tm_out, "out_proj")
        if need_ctx:
            ctx = _out_call(ctx, o_c, w_gate, mod_c, wo, ln_g[i], ln_b[i], alpha, t_c, "out_proj_ctx")
    return x
```

```python
import functools
import math

import jax
import jax.numpy as jnp
from jax import lax
from jax.experimental import pallas as pl
from jax.experimental.pallas import tpu as pltpu

D_MODEL = 1024
GRID_W = 64
WINDOW = 128
ROPE_THETA = 10000.0
NORM_EPS = 1e-6
A_HEADS, A_Q_LORA, A_KV_LORA, A_NOPE, A_ROPE, A_V = 16, 256, 128, 64, 32, 64
B_HEADS, B_HEAD = 8, 64
C_HEADS, C_KV_HEADS, C_HEAD = 8, 2, 128
D_HEADS, D_KV_HEADS, D_HEAD = 16, 2, 64

LANE = 128
HALF = LANE // 2
MXU_WIDTH = 256
LOG2E = math.log2(math.e)
NEG_BIG = -1e30
VMEM_LIMIT = 56 * 1024 * 1024
BF16 = jnp.bfloat16
F32 = jnp.float32


def _cparams(n_grid):
    return pltpu.CompilerParams(dimension_semantics=("arbitrary",) * n_grid, vmem_limit_bytes=VMEM_LIMIT)


def _silu(x):
    return x / (1.0 + jnp.exp(-x))


def _dot(a, b):
    return jnp.dot(a, b, preferred_element_type=F32)


def _dot_nt(a, b):
    return lax.dot_general(a, b, (((1,), (1,)), ((), ())), preferred_element_type=F32)


def _adaln_kernel(c_ref, w_ref, b_ref, o_ref):
    sc = _silu(c_ref[...])
    o_ref[0] = jnp.dot(sc, w_ref[0], preferred_element_type=F32, precision=lax.Precision.HIGHEST) + b_ref[0]


def _adaln_call(cvec, ada_w, ada_b):
    depth, d, n = ada_w.shape
    rows = cvec.shape[0]
    tn = 1024
    return pl.pallas_call(
        _adaln_kernel,
        grid=(depth, n // tn),
        in_specs=[
            pl.BlockSpec((rows, d), lambda i, j: (0, 0)),
            pl.BlockSpec((1, d, tn), lambda i, j: (i, 0, j)),
            pl.BlockSpec((1, 1, tn), lambda i, j: (i, 0, j)),
        ],
        out_specs=pl.BlockSpec((1, rows, tn), lambda i, j: (i, 0, j)),
        out_shape=jax.ShapeDtypeStruct((depth, rows, n), F32),
        compiler_params=_cparams(2),
        name="adaln",
    )(cvec, ada_w, ada_b.reshape(depth, 1, n))


def _mla_slab(pe_lo, nope_a, pe_hi, nope_b):
    used = pe_lo.shape[-1] + nope_a.shape[-1] + pe_hi.shape[-1] + nope_b.shape[-1]
    pad = jnp.zeros(pe_lo.shape[:-1] + (LANE - used,), pe_lo.dtype)
    return jnp.concatenate([pe_lo, nope_a, pe_hi, nope_b, pad], axis=-1)


def _rope_tables(t_len, rot_dim, layout):
    n_freq = rot_dim // 4
    inv_freq = ROPE_THETA ** (-jnp.arange(n_freq, dtype=F32) / n_freq)

    def build(n_pos, is_row):
        ang = jnp.arange(n_pos, dtype=F32)[:, None] * inv_freq
        z = jnp.zeros((n_pos, n_freq), F32)
        cos = jnp.concatenate([jnp.cos(ang), z] if is_row else [z, jnp.cos(ang)], axis=-1)
        sin = jnp.concatenate([jnp.sin(ang), z] if is_row else [z, jnp.sin(ang)], axis=-1)
        one = 0.0 if is_row else 1.0
        if layout == "mla":
            na = HALF - A_ROPE // 2
            ones_a, ones_b = jnp.full((n_pos, na), one, F32), jnp.full((n_pos, A_NOPE - na), one, F32)
            return (_mla_slab(cos, ones_a, cos, ones_b), _mla_slab(-sin, 0 * ones_a, sin, 0 * ones_b))
        reps = LANE // rot_dim
        c = jnp.tile(jnp.concatenate([cos, cos], axis=-1), (1, reps))
        if rot_dim == LANE:
            return (c, jnp.concatenate([-sin, sin], axis=-1))
        zero = jnp.zeros_like(sin)
        return (c, jnp.tile(jnp.concatenate([-sin, zero], axis=-1), (1, reps)),
                jnp.tile(jnp.concatenate([zero, sin], axis=-1), (1, reps)))

    return build(t_len // GRID_W, True), build(GRID_W, False)


def _rope_block(row_refs, col_refs, g0, n_groups):
    tabs = []
    for rr, cr in zip(row_refs, col_refs):
        col_part = cr[...]
        tabs.append(jnp.concatenate(
            [jnp.broadcast_to(rr[g:g + 1, :], (GRID_W, LANE)) + col_part for g in range(g0, g0 + n_groups)], axis=0))
    return tabs


def _rope_slab(x, tabs, half):
    if half == HALF:
        c, sn = tabs
        return x * c + pltpu.roll(x, HALF, 1) * sn
    c, s_lo, s_hi = tabs
    return x * c + pltpu.roll(x, LANE - half, 1) * s_lo + pltpu.roll(x, half, 1) * s_hi


def _rms(x, g):
    return x * lax.rsqrt(jnp.mean(x * x, axis=-1, keepdims=True) + NORM_EPS) * g


PROJ_CHUNK = 512
PROJ_SUB_ROWS = 256


def _proj_kernel(*refs, segs, rope_half, has_rope, has_norm):
    it = iter(refs)
    x_ref, mod_ref = next(it), next(it)
    if has_rope:
        n_tabs = 2 if rope_half == HALF else 3
        row_refs = [next(it) for _ in range(n_tabs)]
        col_refs = [next(it) for _ in range(n_tabs)]
    w_ref = next(it)
    gn_ref = next(it) if has_norm else None
    out_refs = list(it)
    tm, d = x_ref.shape[1], x_ref.shape[2]
    shift, scale = mod_ref[0, :, 0:d], mod_ref[0, :, d:2 * d]
    rs = min(tm, PROJ_SUB_ROWS)
    hbs = [(x_ref[0, r0:r0 + rs, :] * (1.0 + scale) + shift).astype(BF16) for r0 in range(0, tm, rs)]
    if has_rope:
        tabs = [_rope_block(row_refs, col_refs, ri * (rs // GRID_W), rs // GRID_W) for ri in range(tm // rs)]
    items = [(ri, seg, cc, min(PROJ_CHUNK, seg[2] - cc))
             for ri in range(tm // rs) for seg in segs for cc in range(0, seg[2], PROJ_CHUNK)]

    def matmul(item):
        ri, (_, col0, _, _, _, _), cc, cw = item
        return _dot(hbs[ri], w_ref[:, col0 + cc:col0 + cc + cw])

    def finish(item, t):
        ri, (oi, _, _, kind, q_scale, norm_row), cc, cw = item
        rows = slice(ri * rs, (ri + 1) * rs)
        if kind == "plain":
            y = t
        else:
            slabs = []
            for s in range(cw // LANE):
                xs = t[:, s * LANE:(s + 1) * LANE]
                if norm_row is not None:
                    xs = _rms(xs, gn_ref[norm_row:norm_row + 1, :])
                if has_rope:
                    xs = _rope_slab(xs, tabs[ri], rope_half)
                if q_scale != 1.0:
                    xs = xs * q_scale
                slabs.append(xs)
            y = jnp.concatenate(slabs, axis=1) if len(slabs) > 1 else slabs[0]
        out_refs[oi][0, rows, cc:cc + cw] = y.astype(BF16)

    t_cur = matmul(items[0])
    for n, item in enumerate(items):
        t_next = matmul(items[n + 1]) if n + 1 < len(items) else None
        finish(item, t_cur)
        t_cur = t_next


def _proj_call(x, mod, tabs, w, gains, segs, out_widths, rope_half, tm, name):
    bsz, t_len, d = x.shape
    has_rope = tabs is not None
    has_norm = gains is not None
    args = [x, mod]
    in_specs = [
        pl.BlockSpec((1, tm, d), lambda b, i: (b, i, 0)),
        pl.BlockSpec((1, 1, mod.shape[-1]), lambda b, i: (b, 0, 0)),
    ]
    if has_rope:
        row_tabs, col_tabs = tabs
        for tab in row_tabs:
            args.append(tab)
            in_specs.append(pl.BlockSpec((tm // GRID_W, LANE), lambda b, i: (i, 0)))
        for tab in col_tabs:
            args.append(tab)
            in_specs.append(pl.BlockSpec((GRID_W, LANE), lambda b, i: (0, 0)))
    args.append(w)
    in_specs.append(pl.BlockSpec(w.shape, lambda b, i: (0, 0)))
    if has_norm:
        args.append(gains)
        in_specs.append(pl.BlockSpec(gains.shape, lambda b, i: (0, 0)))
    return pl.pallas_call(
        functools.partial(_proj_kernel, segs=segs, rope_half=rope_half, has_rope=has_rope, has_norm=has_norm),
        grid=(bsz, t_len // tm),
        in_specs=in_specs,
        out_specs=[pl.BlockSpec((1, tm, ow), lambda b, i: (b, i, 0)) for ow in out_widths],
        out_shape=[jax.ShapeDtypeStruct((bsz, t_len, ow), BF16) for ow in out_widths],
        compiler_params=_cparams(2),
        name=name,
    )(*args)


def _mla_proj_kernel(*refs, has_rope, q_scale):
    it = iter(refs)
    x_ref, mod_ref = next(it), next(it)
    if has_rope:
        row_refs = [next(it), next(it)]
        col_refs = [next(it), next(it)]
    w_in_ref, gqa_ref, w_qb_ref, gkva_ref, w_kvb_ref = next(it), next(it), next(it), next(it), next(it)
    q_ref, k_ref, v_ref = next(it), next(it), next(it)
    half = HALF
    n_lat =A_Q_LORA + A_KV_LORA + LANE
    n_qk = A_HEADS * LANE
    n_v = A_HEADS * A_V
    tm, d = x_ref.shape[1], x_ref.shape[2]
    shift, scale = mod_ref[0, :, 0:d], mod_ref[0, :, d:2 * d]
    rs = min(tm, PROJ_SUB_ROWS)
    for r0 in range(0, tm, rs):
        rows = slice(r0, r0 + rs)
        hb = (x_ref[0, rows, :] * (1.0 + scale) + shift).astype(BF16)
        t = _dot(hb, w_in_ref[...])
        qn = _rms(t[:, 0:A_Q_LORA], gqa_ref[...]).astype(BF16)
        kn = _rms(t[:, A_Q_LORA:A_Q_LORA + A_KV_LORA], gkva_ref[...]).astype(BF16)
        kpe = t[:, A_Q_LORA + A_KV_LORA:n_lat]
        if has_rope:
            tabs = _rope_block(row_refs, col_refs, r0 // GRID_W, rs // GRID_W)
            kpe = _rope_slab(kpe, tabs, half)
        for cc in range(0, n_qk, PROJ_CHUNK):
            tq = _dot(qn, w_qb_ref[:, cc:cc + PROJ_CHUNK])
            tk = _dot(kn, w_kvb_ref[:, cc:cc + PROJ_CHUNK])
            qs, ks = [], []
            for s in range(PROJ_CHUNK // LANE):
                xq = tq[:, s * LANE:(s + 1) * LANE]
                if has_rope:
                    xq = _rope_slab(xq, tabs, half)
                qs.append(xq * q_scale)
                ks.append(tk[:, s * LANE:(s + 1) * LANE] + kpe)
            q_ref[0, rows, cc:cc + PROJ_CHUNK] = jnp.concatenate(qs, axis=1).astype(BF16)
            k_ref[0, rows, cc:cc + PROJ_CHUNK] = jnp.concatenate(ks, axis=1).astype(BF16)
        for cc in range(0, n_v, PROJ_CHUNK):
            v_ref[0, rows, cc:cc + PROJ_CHUNK] = _dot(kn, w_kvb_ref[:, n_qk + cc:n_qk + cc + PROJ_CHUNK]).astype(BF16)


def _mla_proj_call(x, mod, tabs, w_in, g_qa, w_qb, g_kva, w_kvb, q_scale, tm, name):
    bsz, t_len, d = x.shape
    has_rope = tabs is not None
    args = [x, mod]
    in_specs = [
        pl.BlockSpec((1, tm, d), lambda b, i: (b, i, 0)),
        pl.BlockSpec((1, 1, mod.shape[-1]), lambda b, i: (b, 0, 0)),
    ]
    if has_rope:
        row_tabs, col_tabs = tabs
        for tab in row_tabs:
            args.append(tab)
            in_specs.append(pl.BlockSpec((tm // GRID_W, LANE), lambda b, i: (i, 0)))
        for tab in col_tabs:
            args.append(tab)
            in_specs.append(pl.BlockSpec((GRID_W, LANE), lambda b, i: (0, 0)))
    for a in (w_in, g_qa, w_qb, g_kva, w_kvb):
        args.append(a)
        in_specs.append(pl.BlockSpec(a.shape, lambda b, i: (0, 0)))
    out_widths = (A_HEADS * LANE, A_HEADS * LANE, A_HEADS * A_V)
    return pl.pallas_call(
        functools.partial(_mla_proj_kernel, has_rope=has_rope, q_scale=q_scale),
        grid=(bsz, t_len // tm),
        in_specs=in_specs,
        out_specs=[pl.BlockSpec((1, tm, ow), lambda b, i: (b, i, 0)) for ow in out_widths],
        out_shape=[jax.ShapeDtypeStruct((bsz, t_len, ow), BF16) for ow in out_widths],
        compiler_params=_cparams(2),
        name=name,
    )(*args)


def _lane_lo(shape):
    return lax.broadcasted_iota(jnp.int32, shape, len(shape) - 1) < HALF


def _dense_attn_kernel(*refs, mode, n_q, n_k, rows, tq, tk, n_chunks, n_parts, lambda_init):
    it = iter(refs)
    q_refs = [next(it) for _ in range(n_q)]
    kc_refs = [next(it) for _ in range(n_k)]
    vc_ref = next(it)
    if n_chunks:
        k_refs = [next(it) for _ in range(n_k)]
        v_ref = next(it)
    if mode == "diff":
        lam_ref, gsub_ref = next(it), next(it)
    o_ref = next(it)
    qs_scr, m_scr, acc_scr = next(it), next(it), next(it)
    buf_a, buf_b = next(it), next(it)

    n_rows = len(rows)
    staged = any(mask is not None for (_, mask, _) in rows)
    if staged:
        for r, (qi, mask, _) in enumerate(rows):
            q = q_refs[qi][0]
            lo = _lane_lo(q.shape)
            qs_scr[r * tq:(r + 1) * tq, :] = jnp.where(lo if mask == "lo" else jnp.logical_not(lo), q,
                                                        jnp.zeros_like(q))
    spans = []
    for ks in range(n_k):
        idx = [r for r, row in enumerate(rows) if row[2] == ks]
        spans.append((idx[0] * tq, (idx[-1] + 1) * tq))
    blocks_per_span = max(1, n_parts // n_k)
    row_blocks = []
    for (r0, r1) in spans:
        nb = (r1 - r0) // blocks_per_span
        row_blocks += [(r0 + i * nb, r0 + (i + 1) * nb) for i in range(blocks_per_span)]

    def q_rows(r0, r1):
        if staged:
            return qs_scr[r0:r1, :]
        parts = [q_refs[rows[r][0]][0] for r in range(r0 // tq, r1 // tq)]
        return jnp.concatenate(parts, axis=0) if len(parts) > 1 else parts[0]

    def qk_part(k_load, s_ref, c0, c1):
        for ks, (r0, r1) in enumerate(spans):
            s_ref[r0:r1, c0:c1] = _dot_nt(q_rows(r0, r1), k_load(ks, c0, c1))

    def sm_part(v_ext, s_ref, a, b, first):
        n_cols = v_ext.shape[0] // LANE
        mp = s_ref[a:b, 0:LANE]
        for j in range(1, n_cols):
            mp = jnp.maximum(mp, s_ref[a:b, j * LANE:(j + 1) * LANE])
        smax = jnp.max(mp, axis=1, keepdims=True)
        if first:
            m_new = jnp.broadcast_to(smax, (b - a, LANE))
        else:
            m_prev = m_scr[a:b, :]
            m_new = jnp.maximum(m_prev, smax)
        ps = [jnp.exp2(s_ref[a:b, j * LANE:(j + 1) * LANE] - m_new).astype(BF16) for j in range(n_cols)]
        p = jnp.concatenate(ps, axis=1) if n_cols > 1 else ps[0]
        pv = _dot(p, v_ext)
        if first:
            acc_scr[a:b, :] = pv
        else:
            alpha = jnp.exp2(m_prev - m_new)
            acc_scr[a:b, :] = jnp.concatenate([alpha, alpha], axis=1) * acc_scr[a:b, :] + pv
        m_scr[a:b, :] = m_new

    def ext(v):
        return jnp.concatenate([v, jnp.ones(v.shape, BF16)], axis=1)

    def k_loader(off):
        return lambda ks, c0, c1: k_refs[ks][0, off + c0:off + c1, :]

    def pipelined(v_ext, s_cur, first, k_next, s_next, next_width):
        kw = max(MXU_WIDTH, next_width // n_parts)
        for pi, (a, b) in enumerate(row_blocks):
            sm_part(v_ext, s_cur, a, b, first)
            if pi * kw < next_width:
                qk_part(k_next, s_next, pi * kw, (pi + 1) * kw)

    bufs = (buf_a, buf_b)
    t_c = vc_ref.shape[1]
    ctx_loader = lambda ks, c0, c1: kc_refs[ks][0, c0:c1, :]
    if n_chunks:
        qk_part(k_loader(0), bufs[0], 0, tk)
    else:
        qk_part(ctx_loader, bufs[0], 0, t_c)
    for c in range(n_chunks):
        v_ext = ext(v_ref[0, c * tk:(c + 1) * tk, :])
        if c == n_chunks - 1:
            pipelined(v_ext, bufs[c % 2], c == 0, ctx_loader, bufs[(c + 1) % 2], t_c)
        else:
            pipelined(v_ext, bufs[c % 2], c == 0, k_loader((c + 1) * tk), bufs[(c + 1) % 2], tk)
    vc_ext = ext(vc_ref[0])
    for (a, b) in row_blocks:
        sm_part(vc_ext, bufs[n_chunks % 2], a, b, n_chunks == 0)

    acc = acc_scr[...]
    o = acc[:, 0:LANE] / acc[:, LANE:2 * LANE]
    if mode == "pair":
        out = jnp.where(_lane_lo((tq, LANE)), o[0:tq], o[tq:2 * tq])
    elif mode == "diff":
        lam = lam_ref[...]
        a = jnp.sum(lam[0:1, :] * lam[1:2, :], axis=1, keepdims=True)
        b = jnp.sum(lam[2:3, :] * lam[3:4, :], axis=1, keepdims=True)
        lam_full = jnp.exp(a) - jnp.exp(b) + lambda_init
        dlt = o[0:tq] - lam_full * o[tq:2 * tq]
        out = _rms(dlt, gsub_ref[...]) * (1.0 - lambda_init)
    else:
        out = jnp.concatenate([o[r * tq:(r + 1) * tq] for r in range(n_rows)], axis=1)
    o_ref[0] = out.astype(o_ref.dtype)


def _dense_attn_call(mode, q, kc, vc, k, v, extras, n_groups, q_map, k_map, v_map, rows, out_width, tq, tk,
                     lambda_init, name):
    bsz, t_q, _ = q.shape
    t_c = kc.shape[1]
    n_q = 1 + max(r[0] for r in rows)
    n_k = 1 + max(r[2] for r in rows)
    n_rows = len(rows)
    has_latent = k is not None
    n_chunks = (k.shape[1] // tk) if has_latent else 0
    args, in_specs = [], []
    for j in range(n_q):
        args.append(q)
        in_specs.append(pl.BlockSpec((1, tq, LANE), lambda b, g, i, j=j: (b, i, q_map(g, j))))
    for j in range(n_k):
        args.append(kc)
        in_specs.append(pl.BlockSpec((1, t_c, LANE), lambda b, g, i, j=j: (b, 0, k_map(g, j))))
    args.append(vc)
    in_specs.append(pl.BlockSpec((1, t_c, LANE), lambda b, g, i: (b, 0, v_map(g))))
    if has_latent:
        t_k = k.shape[1]
        for j in range(n_k):
            args.append(k)
            in_specs.append(pl.BlockSpec((1, t_k, LANE), lambda b, g, i, j=j: (b, 0, k_map(g, j))))
        args.append(v)
        in_specs.append(pl.BlockSpec((1, t_k, LANE), lambda b, g, i: (b, 0, v_map(g))))
    for e in extras:
        args.append(e)
        in_specs.append(pl.BlockSpec(e.shape, lambda b, g, i: (0, 0)))
    ow = out_width // n_groups
    s_width = max(tk, t_c) if has_latent else t_c
    return pl.pallas_call(
        functools.partial(_dense_attn_kernel, mode=mode, n_q=n_q, n_k=n_k, rows=rows, tq=tq, tk=tk,
                          n_chunks=n_chunks, n_parts=ATTN_PARTS, lambda_init=lambda_init),
        grid=(bsz, n_groups, t_q // tq),
        in_specs=in_specs,
        out_specs=pl.BlockSpec((1, tq, ow), lambda b, g, i: (b, i, g)),
        out_shape=jax.ShapeDtypeStruct((bsz, t_q, out_width), BF16),
        scratch_shapes=[
            pltpu.VMEM((n_rows * tq, LANE), BF16),
            pltpu.VMEM((n_rows * tq, LANE), F32),
            pltpu.VMEM((n_rows * tq, 2 * LANE), F32),
            pltpu.VMEM((n_rows * tq, s_width), F32),
            pltpu.VMEM((n_rows * tq, s_width), F32),
        ],
        compiler_params=_cparams(3),
        name=name,
    )(*args)


def _swa_kernel(*refs, latent, tq, n_blocks):
    it = iter(refs)
    sink_ref = next(it)
    q_ref = next(it)
    kc_ref, vc_ref = next(it), next(it)
    if latent:
        kp_ref, k0_ref, kn_ref = next(it), next(it), next(it)
        vp_ref, v0_ref, vn_ref = next(it), next(it), next(it)
    o_ref = next(it)
    i = pl.program_id(1)
    per_g = D_HEADS // D_KV_HEADS
    n_pairs = per_g // 2
    lo = _lane_lo((tq, LANE))
    if latent:
        r = lax.broadcasted_iota(jnp.int32, (tq, tq), 0)
        cidx = lax.broadcasted_iota(jnp.int32, (tq, tq), 1)
        ok_prev = jnp.logical_and(cidx >= r, i > 0)
        ok_next = jnp.logical_and(cidx <= r, i < n_blocks - 1)
        ok_pc = jnp.concatenate([ok_prev, jnp.full((tq, tq), True)], axis=1)
        ok_pc = jnp.concatenate([ok_pc] * n_pairs, axis=0)
        ok_next = jnp.concatenate([ok_next] * n_pairs, axis=0)

    def scores(g, e):
        slab = g ^ e
        sl = slice(slab * LANE, (slab + 1) * LANE)
        qrows, sinks = [], []
        for pi in range(n_pairs):
            ps = g * n_pairs + pi
            qv = q_ref[0, :, ps * LANE:(ps + 1) * LANE]
            qrows.append(jnp.where(lo if e == 0 else jnp.logical_not(lo), qv, jnp.zeros_like(qv)))
            sinks.append(jnp.full((tq, LANE), sink_ref[g * per_g + 2 * pi + e], F32))
        qst = jnp.concatenate(qrows, axis=0)
        sink = jnp.concatenate(sinks, axis=0)
        s_list = [_dot_nt(qst, kc_ref[0, :, sl])]
        if latent:
            k_pc = jnp.concatenate([kp_ref[0, :, sl], k0_ref[0, :, sl]], axis=0)
            s_list.append(jnp.where(ok_pc, _dot_nt(qst, k_pc), NEG_BIG))
            s_list.append(jnp.where(ok_next, _dot_nt(qst, kn_ref[0, :, sl]), NEG_BIG))
        return jnp.concatenate(s_list, axis=1), sink, sl

    def softmax_pv(s, sink, sl):
        v_list = [vc_ref[0, :, sl]]
        if latent:
            v_list += [vp_ref[0, :, sl], v0_ref[0, :, sl], vn_ref[0, :, sl]]
        vv = jnp.concatenate(v_list, axis=0)
        vv = jnp.concatenate([vv, jnp.ones(vv.shape, BF16)], axis=1)
        m = jnp.maximum(jnp.max(s, axis=1, keepdims=True), sink)
        ps = [jnp.exp2(s[:, j * LANE:(j + 1) * LANE] - m).astype(BF16) for j in range(s.shape[1] // LANE)]
        pv = _dot(jnp.concatenate(ps, axis=1), vv)
        return pv[:, 0:LANE] / (pv[:, LANE:2 * LANE] + jnp.exp2(sink - m))

    groups = [(g, e) for g in range(D_KV_HEADS) for e in range(2)]
    outs = []
    pending = scores(*groups[0])
    for gi in range(len(groups)):
        nxt = scores(*groups[gi + 1]) if gi + 1 < len(groups) else None
        outs.append(softmax_pv(*pending))
        pending = nxt
    for g in range(D_KV_HEADS):
        for pi in range(n_pairs):
            ps = g * n_pairs + pi
            out = jnp.where(lo, outs[2 * g][pi * tq:(pi + 1) * tq], outs[2 * g + 1][pi * tq:(pi + 1) * tq])
            o_ref[0, :, ps * LANE:(ps + 1) * LANE] = out.astype(o_ref.dtype)


def _swa_call(q, kc, vc, k, v, sinks_log2, tq, name):
    bsz, t_q, width = q.shape
    t_c = kc.shape[1]
    latent = k is not None
    n_blocks = t_q // tq
    args = [sinks_log2, q, kc, vc]
    in_specs = [
        pl.BlockSpec(memory_space=pltpu.SMEM),
        pl.BlockSpec((1, tq, width), lambda b, i: (b, i, 0)),
        pl.BlockSpec((1, t_c, 2 * LANE), lambda b, i: (b, 0, 0)),
        pl.BlockSpec((1, t_c, 2 * LANE), lambda b, i: (b, 0, 0)),
    ]
    if latent:
        assert tq == WINDOW
        prev_map = lambda b, i: (b, jnp.maximum(i - 1, 0), 0)
        cur_map = lambda b, i: (b, i, 0)
        next_map = lambda b, i: (b, jnp.minimum(i + 1, n_blocks - 1), 0)
        for arr in (k, v):
            for mp in (prev_map, cur_map, next_map):
                args.append(arr)
                in_specs.append(pl.BlockSpec((1, tq, 2 * LANE), mp))
    return pl.pallas_call(
        functools.partial(_swa_kernel, latent=latent, tq=tq, n_blocks=n_blocks),
        grid=(bsz, n_blocks),
        in_specs=in_specs,
        out_specs=pl.BlockSpec((1, tq, width), lambda b, i: (b, i, 0)),
        out_shape=jax.ShapeDtypeStruct((bsz, t_q, width), BF16),
        compiler_params=_cparams(2),
        name=name,
    )(*args)


def _out_kernel(x_ref, o_ref, mod_ref, wg_ref, w_ref, lng_ref, lnb_ref, y_ref, *, alpha):
    tm, d = x_ref.shape[1], x_ref.shape[2]
    shift, scale, gate = mod_ref[0, :, 0:d], mod_ref[0, :, d:2 * d], mod_ref[0, :, 2 * d:3 * d]
    rs = min(tm, OUT_SUB_ROWS)
    n_sub = tm // rs

    def branch(r0):
        xs = x_ref[0, r0:r0 + rs, :]
        hb = (xs * (1.0 + scale) + shift).astype(BF16)
        g = _silu(_dot(hb, wg_ref[...]))
        u = (o_ref[0, r0:r0 + rs, :].astype(F32) * g).astype(BF16)
        return alpha * xs + gate * _dot(u, w_ref[...])

    def norm_store(r0, z):
        zc = z - jnp.mean(z, axis=-1, keepdims=True)
        var = jnp.mean(zc * zc, axis=-1, keepdims=True)
        y_ref[0, r0:r0 + rs, :] = zc * lax.rsqrt(var + NORM_EPS) * lng_ref[...] + lnb_ref[...]

    zs = [branch(i * rs) for i in range(n_sub)]
    for i in range(n_sub):
        norm_store(i * rs, zs[i])


def _out_call(x, o, w_gate, mod, w, ln_g, ln_b, alpha, tm, name):
    bsz, t_len, d = x.shape
    width = o.shape[-1]
    return pl.pallas_call(
        functools.partial(_out_kernel, alpha=alpha),
        grid=(bsz, t_len // tm),
        in_specs=[
            pl.BlockSpec((1, tm, d), lambda b, i: (b, i, 0)),
            pl.BlockSpec((1, tm, width), lambda b, i: (b, i, 0)),
            pl.BlockSpec((1, 1, mod.shape[-1]), lambda b, i: (b, 0, 0)),
            pl.BlockSpec(w_gate.shape, lambda b, i: (0, 0)),
            pl.BlockSpec(w.shape, lambda b, i: (0, 0)),
            pl.BlockSpec((1, d), lambda b, i: (0, 0)),
            pl.BlockSpec((1, d), lambda b, i: (0, 0)),
        ],
        out_specs=pl.BlockSpec((1, tm, d), lambda b, i: (b, i, 0)),
        out_shape=jax.ShapeDtypeStruct((bsz, t_len, d), F32),
        compiler_params=_cparams(2),
        name=name,
    )(x, o, mod, w_gate, w, ln_g.reshape(1, d), ln_b.reshape(1, d))


PROJ_ROWS = 512
OUT_ROWS = 1024
OUT_SUB_ROWS = 256
ATTN_ROWS = 1024
ATTN_TK = 1024
ATTN_PARTS = 4


def _tiles(t_len, n_row_groups=1):
    tm = min(PROJ_ROWS, t_len)
    tq = min(ATTN_ROWS // n_row_groups, t_len)
    tk = min(ATTN_TK, t_len)
    return tm, tq, tk


def _mla_layer(x, ctx, mod, mod_c, need_ctx, w_in, g_qa, w_qb, g_kva, w_kvb):
    t_len, t_c = x.shape[1], ctx.shape[1]
    tm, tq, tk = _tiles(t_len, 2)
    d = w_in.shape[0]
    o0, o1, o2 = A_Q_LORA, A_Q_LORA + A_KV_LORA, A_Q_LORA + A_KV_LORA + A_ROPE
    hr = A_ROPE // 2
    na = HALF - hr
    w_kpe = w_in[:, o1:o2]
    kpe_pad = _mla_slab(w_kpe[:, :hr], jnp.zeros((d, na), F32), w_kpe[:, hr:], jnp.zeros((d, A_NOPE - na), F32))
    w_in_r = jnp.concatenate([w_in[:, :o1], kpe_pad], axis=1).astype(BF16)
    w_gate = w_in[:, o2:].astype(BF16)
    qb = w_qb.reshape(A_Q_LORA, A_HEADS, A_NOPE + A_ROPE)
    w_qb_r = _mla_slab(qb[:, :, A_NOPE:A_NOPE + hr], qb[:, :, :na], qb[:, :, A_NOPE + hr:], qb[:, :, na:A_NOPE])
    w_qb_r = w_qb_r.reshape(A_Q_LORA, A_HEADS * LANE).astype(BF16)
    kvb = w_kvb.reshape(A_KV_LORA, A_HEADS, A_NOPE + A_V)
    zk = jnp.zeros((A_KV_LORA, A_HEADS, hr), F32)
    k_pad = _mla_slab(zk, kvb[:, :, :na], zk, kvb[:, :, na:A_NOPE]).reshape(A_KV_LORA, A_HEADS * LANE)
    w_kvb_r = jnp.concatenate([k_pad, kvb[:, :, A_NOPE:].reshape(A_KV_LORA, A_HEADS * A_V)], axis=1).astype(BF16)
    q_scale = (A_NOPE + A_ROPE) ** -0.5 * LOG2E
    gq, gk = g_qa.reshape(1, -1), g_kva.reshape(1, -1)
    tabs = _rope_tables(t_len, A_ROPE, "mla")
    q, k, v = _mla_proj_call(x, mod, tabs, w_in_r, gq, w_qb_r, gk, w_kvb_r, q_scale, tm, "mla_proj")
    q_c, k_c, v_c = _mla_proj_call(ctx, mod_c, None, w_in_r, gq, w_qb_r, gk, w_kvb_r, q_scale, t_c, "mla_proj_ctx")
    rows = ((0, None, 0), (1, None, 1))
    maps = dict(n_groups=A_HEADS // 2, q_map=lambda gi, j: 2 * gi + j, k_map=lambda gi, j: 2 * gi + j,
                v_map=lambda gi: gi, rows=rows, out_width=A_HEADS * A_V, lambda_init=0.0)
    o = _dense_attn_call("pair", q, k_c, v_c, k, v, (), tq=tq, tk=tk, name="mla_attn", **maps)
    o_c = _dense_attn_call("pair", q_c, k_c, v_c, None, None, (), tq=t_c, tk=tk, name="mla_attn_ctx", **maps) \
        if need_ctx else None
    return o, w_gate, o_c


def _diff_layer(x, ctx, mod, mod_c, need_ctx, layer_idx, w_in, lam, g_sub):
    t_len, t_c = x.shape[1], ctx.shape[1]
    tm, tq, tk = _tiles(t_len, 2)
    n = 2 * B_HEADS * B_HEAD
    q_scale = B_HEAD ** -0.5 * LOG2E
    segs = ((0, 0, n, "rope", q_scale, None), (1, n, n, "rope", 1.0, None), (2, 2 * n, n, "plain", 1.0, None))
    widths = (n, n, n)
    wb = w_in[:, :3 * n].astype(BF16)
    w_gate = w_in[:, 3 * n:].astype(BF16)
    tabs = _rope_tables(t_len, B_HEAD, "unit")
    q, k, v = _proj_call(x, mod, tabs, wb, None, segs, widths, B_HEAD // 2, tm, "diff_proj")
    q_c, k_c, v_c = _proj_call(ctx, mod_c, None, wb, None, segs, widths, B_HEAD // 2, t_c, "diff_proj_ctx")
    lambda_init = 0.8 - 0.6 * math.exp(-0.3 * layer_idx)
    rows = ((0, "lo", 0), (0, "hi", 0))
    extras = (lam.astype(F32), g_sub.reshape(1, -1).astype(F32))
    maps = dict(n_groups=B_HEADS, q_map=lambda gi, j: gi, k_map=lambda gi, j: gi, v_map=lambda gi: gi,
                rows=rows, out_width=n, lambda_init=lambda_init)
    o = _dense_attn_call("diff", q, k_c, v_c, k, v, extras, tq=tq, tk=tk, name="diff_attn", **maps)
    o_c = _dense_attn_call("diff", q_c, k_c, v_c, None, None, extras, tq=t_c, tk=tk, name="diff_attn_ctx", **maps) \
        if need_ctx else None
    return o, w_gate, o_c


def _gqa_layer(x, ctx, mod, mod_c, need_ctx, w_in, g_q, g_k):
    t_len, t_c = x.shape[1], ctx.shape[1]
    tm, tq, tk = _tiles(t_len, C_HEADS // C_KV_HEADS)
    nq, nkv = C_HEADS * C_HEAD, C_KV_HEADS * C_HEAD
    q_scale = C_HEAD ** -0.5 * LOG2E
    segs = ((0, 0, nq, "rope", q_scale, 0), (1, nq, nkv, "rope", 1.0, 1), (2, nq + nkv, nkv, "plain", 1.0, None))
    widths = (nq, nkv, nkv)
    wb = w_in[:, :nq + 2 * nkv].astype(BF16)
    w_gate = w_in[:, nq + 2 * nkv:].astype(BF16)
    gains = jnp.stack([g_q, g_k]).astype(F32)
    tabs = _rope_tables(t_len, C_HEAD, "unit")
    q, k, v = _proj_call(x, mod, tabs, wb, gains, segs, widths, C_HEAD // 2, tm, "gqa_proj")
    q_c, k_c, v_c = _proj_call(ctx, mod_c, None, wb, gains, segs, widths, C_HEAD // 2, t_c, "gqa_proj_ctx")
    per = C_HEADS // C_KV_HEADS
    rows = tuple((j, None, 0) for j in range(per))
    maps = dict(n_groups=C_KV_HEADS, q_map=lambda gi, j: per * gi + j, k_map=lambda gi, j: gi, v_map=lambda gi: gi,
                rows=rows, out_width=nq, lambda_init=0.0)
    o = _dense_attn_call("stack", q, k_c, v_c, k, v, (), tq=tq, tk=tk, name="gqa_attn", **maps)
    o_c = _dense_attn_call("stack", q_c, k_c, v_c, None, None, (), tq=t_c, tk=tk, name="gqa_attn_ctx", **maps) \
        if need_ctx else None
    return o, w_gate, o_c


def _swa_layer(x, ctx, mod, mod_c, need_ctx, w_in, sinks):
    t_len, t_c = x.shape[1], ctx.shape[1]
    tm, _, _ = _tiles(t_len)
    nq, nkv = D_HEADS * D_HEAD, D_KV_HEADS * D_HEAD
    wk = w_in[:, nq:nq + nkv]
    wv = w_in[:, nq + nkv:nq + 2 * nkv]
    swap = lambda w: jnp.concatenate([w, w[:, D_HEAD:], w[:, :D_HEAD]], axis=1)
    wb = jnp.concatenate([w_in[:, :nq], swap(wk), swap(wv)], axis=1).astype(BF16)
    w_gate = w_in[:, nq + 2 * nkv:].astype(BF16)
    q_scale = D_HEAD ** -0.5 * LOG2E
    segs = ((0, 0, nq, "rope", q_scale, None), (1, nq, 2 * nkv, "rope", 1.0, None),
            (2, nq + 2 * nkv, 2 * nkv, "plain", 1.0, None))
    widths = (nq, 2 * nkv, 2 * nkv)
    tabs = _rope_tables(t_len, D_HEAD, "unit")
    q, k, v = _proj_call(x, mod, tabs, wb, None, segs, widths, D_HEAD // 2, tm, "swa_proj")
    q_c, k_c, v_c = _proj_call(ctx, mod_c, None, wb, None, segs, widths, D_HEAD // 2, t_c, "swa_proj_ctx")
    sinks_log2 = sinks.astype(F32) * LOG2E
    o = _swa_call(q, k_c, v_c, k, v, sinks_log2, WINDOW, "swa_attn")
    o_c = _swa_call(q_c, k_c, v_c, None, None, sinks_log2, t_c, "swa_attn_ctx") if need_ctx else None
    return o, w_gate, o_c


def kernel(x, c, ctx, c_ctx, ada_w, ada_b, out_w, ln_g, ln_b, mla_w_in, mla_g_qa, mla_w_qb, mla_g_kva, mla_w_kvb,
           diff_w_in, diff_lambda, diff_g_sub, gqa_w_in, gqa_g_q, gqa_g_k, swa_w_in, swa_sink):
    depth = ada_w.shape[0]
    bsz, t_len, d = x.shape
    t_c = ctx.shape[1]
    alpha = (2 * depth) ** 0.25
    n_rows = 8
    assert bsz + 1 <= n_rows
    cvec = jnp.concatenate([c, c_ctx[None, :], jnp.zeros((n_rows - bsz - 1, d), F32)], axis=0)
    mods = _adaln_call(cvec, ada_w, ada_b)
    tm_out = min(OUT_ROWS, t_len)
    for i in range(depth):
        kind, j = i % 4, i // 4
        need_ctx = i < depth - 1
        mod = mods[i, 0:bsz][:, None, :]
        mod_c = jnp.broadcast_to(mods[i, bsz][None, None, :], (bsz, 1, 3 * d))
        if kind == 0:
            o, w_gate, o_c = _mla_layer(x, ctx, mod, mod_c, need_ctx, mla_w_in[j], mla_g_qa[j], mla_w_qb[j],
                                        mla_g_kva[j], mla_w_kvb[j])
        elif kind == 1:
            o, w_gate, o_c = _diff_layer(x, ctx, mod, mod_c, need_ctx, i, diff_w_in[j], diff_lambda[j], diff_g_sub[j])
        elif kind == 2:
            o, w_gate, o_c = _gqa_layer(x, ctx, mod, mod_c, need_ctx, gqa_w_in[j], gqa_g_q[j], gqa_g_k[j])
        else:
            o, w_gate, o_c = _swa_layer(x, ctx, mod, mod_c, need_ctx, swa_w_in[j], swa_sink[j])
        wo = out_w[i].astype(BF16)
        x = _out_call(x, o, w_gate, mod, wo, ln_g[i], ln_b[i], alpha, tm_out, "out_proj")
        if need_ctx:
            ctx = _out_call(ctx, o_c, w_gate, mod_c, wo, ln_g[i], ln_b[i], alpha, t_c, "out_proj_ctx")
    return x
```

```python
import functools
import math

import jax
import jax.numpy as jnp
from jax import lax
from jax.experimental import pallas as pl
from jax.experimental.pallas import tpu as pltpu

D_MODEL = 1024
GRID_W = 64
WINDOW = 128
ROPE_THETA = 10000.0
NORM_EPS = 1e-6
A_HEADS, A_Q_LORA, A_KV_LORA, A_NOPE, A_ROPE, A_V = 16, 256, 128, 64, 32, 64
B_HEADS, B_HEAD = 8, 64
C_HEADS, C_KV_HEADS, C_HEAD = 8, 2, 128
D_HEADS, D_KV_HEADS, D_HEAD = 16, 2, 64

LANE = 128
HALF = LANE // 2
MXU_WIDTH = 256
LOG2E = math.log2(math.e)
NEG_BIG = -1e30
VMEM_LIMIT = 56 * 1024 * 1024
BF16 = jnp.bfloat16
F32 = jnp.float32


def _cparams(n_grid):
    return pltpu.CompilerParams(dimension_semantics=("arbitrary",) * n_grid, vmem_limit_bytes=VMEM_LIMIT)


def _silu(x):
    return x / (1.0 + jnp.exp(-x))


def _dot(a, b):
    return jnp.dot(a, b, preferred_element_type=F32)


def _dot_nt(a, b):
    return lax.dot_general(a, b, (((1,), (1,)), ((), ())), preferred_element_type=F32)


def _adaln_kernel(c_ref, w_ref, b_ref, o_ref):
    sc = _silu(c_ref[...])
    o_ref[0] = jnp.dot(sc, w_ref[0], preferred_element_type=F32, precision=lax.Precision.HIGHEST) + b_ref[0]


def _adaln_call(cvec, ada_w, ada_b):
    depth, d, n = ada_w.shape
    rows = cvec.shape[0]
    tn = 1024
    return pl.pallas_call(
        _adaln_kernel,
        grid=(depth, n // tn),
        in_specs=[
            pl.BlockSpec((rows, d), lambda i, j: (0, 0)),
            pl.BlockSpec((1, d, tn), lambda i, j: (i, 0, j)),
            pl.BlockSpec((1, 1, tn), lambda i, j: (i, 0, j)),
        ],
        out_specs=pl.BlockSpec((1, rows, tn), lambda i, j: (i, 0, j)),
        out_shape=jax.ShapeDtypeStruct((depth, rows, n), F32),
        compiler_params=_cparams(2),
        name="adaln",
    )(cvec, ada_w, ada_b.reshape(depth, 1, n))


def _mla_slab(pe_lo, nope_a, pe_hi, nope_b):
    used = pe_lo.shape[-1] + nope_a.shape[-1] + pe_hi.shape[-1] + nope_b.shape[-1]
    pad = jnp.zeros(pe_lo.shape[:-1] + (LANE - used,), pe_lo.dtype)
    return jnp.concatenate([pe_lo, nope_a, pe_hi, nope_b, pad], axis=-1)


def _rope_tables(t_len, rot_dim, layout):
    n_freq = rot_dim // 4
    inv_freq = ROPE_THETA ** (-jnp.arange(n_freq, dtype=F32) / n_freq)

    def build(n_pos, is_row):
        ang = jnp.arange(n_pos, dtype=F32)[:, None] * inv_freq
        z = jnp.zeros((n_pos, n_freq), F32)
        cos = jnp.concatenate([jnp.cos(ang), z] if is_row else [z, jnp.cos(ang)], axis=-1)
        sin = jnp.concatenate([jnp.sin(ang), z] if is_row else [z, jnp.sin(ang)], axis=-1)
        one = 0.0 if is_row else 1.0
        if layout == "mla":
            na = HALF - A_ROPE // 2
            ones_a, ones_b = jnp.full((n_pos, na), one, F32), jnp.full((n_pos, A_NOPE - na), one, F32)
            return (_mla_slab(cos, ones_a, cos, ones_b), _mla_slab(-sin, 0 * ones_a, sin, 0 * ones_b))
        reps = LANE // rot_dim
        c = jnp.tile(jnp.concatenate([cos, cos], axis=-1), (1, reps))
        if rot_dim == LANE:
            return (c, jnp.concatenate([-sin, sin], axis=-1))
        zero = jnp.zeros_like(sin)
        return (c, jnp.tile(jnp.concatenate([-sin, zero], axis=-1), (1, reps)),
                jnp.tile(jnp.concatenate([zero, sin], axis=-1), (1, reps)))

    return build(t_len // GRID_W, True), build(GRID_W, False)


def _rope_block(row_refs, col_refs, g0, n_groups):
    tabs = []
    for rr, cr in zip(row_refs, col_refs):
        col_part = cr[...]
        tabs.append(jnp.concatenate(
            [jnp.broadcast_to(rr[g:g + 1, :], (GRID_W, LANE)) + col_part for g in range(g0, g0 + n_groups)], axis=0))
    return tabs


def _rope_slab(x, tabs, half):
    if half == HALF:
        c, sn = tabs
        return x * c + pltpu.roll(x, HALF, 1) * sn
    c, s_lo, s_hi = tabs
    return x * c + pltpu.roll(x, LANE - half, 1) * s_lo + pltpu.roll(x, half, 1) * s_hi


def _rms(x, g):
    return x * lax.rsqrt(jnp.mean(x * x, axis=-1, keepdims=True) + NORM_EPS) * g


PROJ_CHUNK = 512
PROJ_SUB_ROWS = 256


def _proj_kernel(*refs, segs, rope_half, has_rope, has_norm):
    it = iter(refs)
    x_ref, mod_ref = next(it), next(it)
    if has_rope:
        n_tabs = 2 if rope_half == HALF else 3
        row_refs = [next(it) for _ in range(n_tabs)]
        col_refs = [next(it) for _ in range(n_tabs)]
    w_ref = next(it)
    gn_ref = next(it) if has_norm else None
    out_refs = list(it)
    tm, d = x_ref.shape[1], x_ref.shape[2]
    shift, scale = mod_ref[0, :, 0:d], mod_ref[0, :, d:2 * d]
    rs = min(tm, PROJ_SUB_ROWS)
    hbs = [(x_ref[0, r0:r0 + rs, :] * (1.0 + scale) + shift).astype(BF16) for r0 in range(0, tm, rs)]
    if has_rope:
        tabs = [_rope_block(row_refs, col_refs, ri * (rs // GRID_W), rs // GRID_W) for ri in range(tm // rs)]
    items = [(ri, seg, cc, min(PROJ_CHUNK, seg[2] - cc))
             for ri in range(tm // rs) for seg in segs for cc in range(0, seg[2], PROJ_CHUNK)]

    def matmul(item):
        ri, (_, col0, _, _, _, _), cc, cw = item
        return _dot(hbs[ri], w_ref[:, col0 + cc:col0 + cc + cw])

    def finish(item, t):
        ri, (oi, _, _, kind, q_scale, norm_row), cc, cw = item
        rows = slice(ri * rs, (ri + 1) * rs)
        if kind == "plain":
            y = t
        else:
            slabs = []
            for s in range(cw // LANE):
                xs = t[:, s * LANE:(s + 1) * LANE]
                if norm_row is not None:
                    xs = _rms(xs, gn_ref[norm_row:norm_row + 1, :])
                if has_rope:
                    xs = _rope_slab(xs, tabs[ri], rope_half)
                if q_scale != 1.0:
                    xs = xs * q_scale
                slabs.append(xs)
            y = jnp.concatenate(slabs, axis=1) if len(slabs) > 1 else slabs[0]
        out_refs[oi][0, rows, cc:cc + cw] = y.astype(BF16)

    t_cur = matmul(items[0])
    for n, item in enumerate(items):
        t_next = matmul(items[n + 1]) if n + 1 < len(items) else None
        finish(item, t_cur)
        t_cur = t_next


def _proj_call(x, mod, tabs, w, gains, segs, out_widths, rope_half, tm, name):
    bsz, t_len, d = x.shape
    has_rope = tabs is not None
    has_norm = gains is not None
    args = [x, mod]
    in_specs = [
        pl.BlockSpec((1, tm, d), lambda b, i: (b, i, 0)),
        pl.BlockSpec((1, 1, mod.shape[-1]), lambda b, i: (b, 0, 0)),
    ]
    if has_rope:
        row_tabs, col_tabs = tabs
        for tab in row_tabs:
            args.append(tab)
            in_specs.append(pl.BlockSpec((tm // GRID_W, LANE), lambda b, i: (i, 0)))
        for tab in col_tabs:
            args.append(tab)
            in_specs.append(pl.BlockSpec((GRID_W, LANE), lambda b, i: (0, 0)))
    args.append(w)
    in_specs.append(pl.BlockSpec(w.shape, lambda b, i: (0, 0)))
    if has_norm:
        args.append(gains)
        in_specs.append(pl.BlockSpec(gains.shape, lambda b, i: (0, 0)))
    return pl.pallas_call(
        functools.partial(_proj_kernel, segs=segs, rope_half=rope_half, has_rope=has_rope, has_norm=has_norm),
        grid=(bsz, t_len // tm),
        in_specs=in_specs,
        out_specs=[pl.BlockSpec((1, tm, ow), lambda b, i: (b, i, 0)) for ow in out_widths],
        out_shape=[jax.ShapeDtypeStruct((bsz, t_len, ow), BF16) for ow in out_widths],
        compiler_params=_cparams(2),
        name=name,
    )(*args)


def _mla_proj_kernel(*refs, has_rope, q_scale):
    it = iter(refs)
    x_ref, mod_ref = next(it), next(it)
    if has_rope:
        row_refs = [next(it), next(it)]
        col_refs = [next(it), next(it)]
    w_in_ref, gqa_ref, w_qb_ref, gkva_ref, w_kvb_ref = next(it), next(it), next(it), next(it), next(it)
    q_ref, k_ref, v_ref = next(it), next(it), next(it)
    half = HALF
    n_lat =A_Q_LORA + A_KV_LORA + LANE
    n_qk = A_HEADS * LANE
    n_v = A_HEADS * A_V
    tm, d = x_ref.shape[1], x_ref.shape[2]
    shift, scale = mod_ref[0, :, 0:d], mod_ref[0, :, d:2 * d]
    rs = min(tm, PROJ_SUB_ROWS)
    for r0 in range(0, tm, rs):
        rows = slice(r0, r0 + rs)
        hb = (x_ref[0, rows, :] * (1.0 + scale) + shift).astype(BF16)
        t = _dot(hb, w_in_ref[...])
        qn = _rms(t[:, 0:A_Q_LORA], gqa_ref[...]).astype(BF16)
        kn = _rms(t[:, A_Q_LORA:A_Q_LORA + A_KV_LORA], gkva_ref[...]).astype(BF16)
        kpe = t[:, A_Q_LORA + A_KV_LORA:n_lat]
        if has_rope:
            tabs = _rope_block(row_refs, col_refs, r0 // GRID_W, rs // GRID_W)
            kpe = _rope_slab(kpe, tabs, half)
        for cc in range(0, n_qk, PROJ_CHUNK):
            tq = _dot(qn, w_qb_ref[:, cc:cc + PROJ_CHUNK])
            tk = _dot(kn, w_kvb_ref[:, cc:cc + PROJ_CHUNK])
            qs, ks = [], []
            for s in range(PROJ_CHUNK // LANE):
                xq = tq[:, s * LANE:(s + 1) * LANE]
                if has_rope:
                    xq = _rope_slab(xq, tabs, half)
                qs.append(xq * q_scale)
                ks.append(tk[:, s * LANE:(s + 1) * LANE] + kpe)
            q_ref[0, rows, cc:cc + PROJ_CHUNK] = jnp.concatenate(qs, axis=1).astype(BF16)
            k_ref[0, rows, cc:cc + PROJ_CHUNK] = jnp.concatenate(ks, axis=1).astype(BF16)
        for cc in range(0, n_v, PROJ_CHUNK):
            v_ref[0, rows, cc:cc + PROJ_CHUNK] = _dot(kn, w_kvb_ref[:, n_qk + cc:n_qk + cc + PROJ_CHUNK]).astype(BF16)


def _mla_proj_call(x, mod, tabs, w_in, g_qa, w_qb, g_kva, w_kvb, q_scale, tm, name):
    bsz, t_len, d = x.shape
    has_rope = tabs is not None
    args = [x, mod]
    in_specs = [
        pl.BlockSpec((1, tm, d), lambda b, i: (b, i, 0)),
        pl.BlockSpec((1, 1, mod.shape[-1]), lambda b, i: (b, 0, 0)),
    ]
    if has_rope:
        row_tabs, col_tabs = tabs
        for tab in row_tabs:
            args.append(tab)
            in_specs.append(pl.BlockSpec((tm // GRID_W, LANE), lambda b, i: (i, 0)))
        for tab in col_tabs:
            args.append(tab)
            in_specs.append(pl.BlockSpec((GRID_W, LANE), lambda b, i: (0, 0)))
    for a in (w_in, g_qa, w_qb, g_kva, w_kvb):
        args.append(a)
        in_specs.append(pl.BlockSpec(a.shape, lambda b, i: (0, 0)))
    out_widths = (A_HEADS * LANE, A_HEADS * LANE, A_HEADS * A_V)
    return pl.pallas_call(
        functools.partial(_mla_proj_kernel, has_rope=has_rope, q_scale=q_scale),
        grid=(bsz, t_len // tm),
        in_specs=in_specs,
        out_specs=[pl.BlockSpec((1, tm, ow), lambda b, i: (b, i, 0)) for ow in out_widths],
        out_shape=[jax.ShapeDtypeStruct((bsz, t_len, ow), BF16) for ow in out_widths],
        compiler_params=_cparams(2),
        name=name,
    )(*args)


def _lane_lo(shape):
    return lax.broadcasted_iota(jnp.int32, shape, len(shape) - 1) < HALF


def _dense_attn_kernel(*refs, mode, n_q, n_k, rows, tq, tk, n_chunks, n_parts, lambda_init):
    it = iter(refs)
    q_refs = [next(it) for _ in range(n_q)]
    kc_refs = [next(it) for _ in range(n_k)]
    vc_ref = next(it)
    if n_chunks:
        k_refs = [next(it) for _ in range(n_k)]
        v_ref = next(it)
    if mode == "diff":
        lam_ref, gsub_ref = next(it), next(it)
    o_ref = next(it)
    qs_scr, m_scr, acc_scr = next(it), next(it), next(it)
    buf_a, buf_b = next(it), next(it)

    n_rows = len(rows)
    staged = any(mask is not None for (_, mask, _) in rows)
    if staged:
        for r, (qi, mask, _) in enumerate(rows):
            q = q_refs[qi][0]
            lo = _lane_lo(q.shape)
            qs_scr[r * tq:(r + 1) * tq, :] = jnp.where(lo if mask == "lo" else jnp.logical_not(lo), q,
                                                        jnp.zeros_like(q))
    spans = []
    for ks in range(n_k):
        idx = [r for r, row in enumerate(rows) if row[2] == ks]
        spans.append((idx[0] * tq, (idx[-1] + 1) * tq))
    blocks_per_span = max(1, n_parts // n_k)
    row_blocks = []
    for (r0, r1) in spans:
        nb = (r1 - r0) // blocks_per_span
        row_blocks += [(r0 + i * nb, r0 + (i + 1) * nb) for i in range(blocks_per_span)]

    def q_rows(r0, r1):
        if staged:
            return qs_scr[r0:r1, :]
        parts = [q_refs[rows[r][0]][0] for r in range(r0 // tq, r1 // tq)]
        return jnp.concatenate(parts, axis=0) if len(parts) > 1 else parts[0]

    def qk_part(k_load, s_ref, c0, c1):
        for ks, (r0, r1) in enumerate(spans):
            s_ref[r0:r1, c0:c1] = _dot_nt(q_rows(r0, r1), k_load(ks, c0, c1))

    def sm_part(v_ext, s_ref, a, b, first):
        n_cols = v_ext.shape[0] // LANE
        mp = s_ref[a:b, 0:LANE]
        for j in range(1, n_cols):
            mp = jnp.maximum(mp, s_ref[a:b, j * LANE:(j + 1) * LANE])
        smax = jnp.max(mp, axis=1, keepdims=True)
        if first:
            m_new = jnp.broadcast_to(smax, (b - a, LANE))
        else:
            m_prev = m_scr[a:b, :]
            m_new = jnp.maximum(m_prev, smax)
        ps = [jnp.exp2((s_ref[a:b, j * LANE:(j + 1) * LANE] - m_new).astype(BF16)) for j in range(n_cols)]
        p = jnp.concatenate(ps, axis=1) if n_cols > 1 else ps[0]
        pv = _dot(p, v_ext)
        if first:
            acc_scr[a:b, :] = pv
        else:
            alpha = jnp.exp2(m_prev - m_new)
            acc_scr[a:b, :] = jnp.concatenate([alpha, alpha], axis=1) * acc_scr[a:b, :] + pv
        m_scr[a:b, :] = m_new

    def ext(v):
        return jnp.concatenate([v, jnp.ones(v.shape, BF16)], axis=1)

    def k_loader(off):
        return lambda ks, c0, c1: k_refs[ks][0, off + c0:off + c1, :]

    def pipelined(v_ext, s_cur, first, k_next, s_next, next_width):
        kw = max(MXU_WIDTH, next_width // n_parts)
        for pi, (a, b) in enumerate(row_blocks):
            sm_part(v_ext, s_cur, a, b, first)
            if pi * kw < next_width:
                qk_part(k_next, s_next, pi * kw, (pi + 1) * kw)

    bufs = (buf_a, buf_b)
    t_c = vc_ref.shape[1]
    ctx_loader = lambda ks, c0, c1: kc_refs[ks][0, c0:c1, :]
    if n_chunks:
        qk_part(k_loader(0), bufs[0], 0, tk)
    else:
        qk_part(ctx_loader, bufs[0], 0, t_c)
    for c in range(n_chunks):
        v_ext = ext(v_ref[0, c * tk:(c + 1) * tk, :])
        if c == n_chunks - 1:
            pipelined(v_ext, bufs[c % 2], c == 0, ctx_loader, bufs[(c + 1) % 2], t_c)
        else:
            pipelined(v_ext, bufs[c % 2], c == 0, k_loader((c + 1) * tk), bufs[(c + 1) % 2], tk)
    vc_ext = ext(vc_ref[0])
    for (a, b) in row_blocks:
        sm_part(vc_ext, bufs[n_chunks % 2], a, b, n_chunks == 0)

    acc = acc_scr[...]
    o = acc[:, 0:LANE] / acc[:, LANE:2 * LANE]
    if mode == "pair":
        out = jnp.where(_lane_lo((tq, LANE)), o[0:tq], o[tq:2 * tq])
    elif mode == "diff":
        lam = lam_ref[...]
        a = jnp.sum(lam[0:1, :] * lam[1:2, :], axis=1, keepdims=True)
        b = jnp.sum(lam[2:3, :] * lam[3:4, :], axis=1, keepdims=True)
        lam_full = jnp.exp(a) - jnp.exp(b) + lambda_init
        dlt = o[0:tq] - lam_full * o[tq:2 * tq]
        out = _rms(dlt, gsub_ref[...]) * (1.0 - lambda_init)
    else:
        out = jnp.concatenate([o[r * tq:(r + 1) * tq] for r in range(n_rows)], axis=1)
    o_ref[0] = out.astype(o_ref.dtype)


def _dense_attn_call(mode, q, kc, vc, k, v, extras, n_groups, q_map, k_map, v_map, rows, out_width, tq, tk,
                     lambda_init, name):
    bsz, t_q, _ = q.shape
    t_c = kc.shape[1]
    n_q = 1 + max(r[0] for r in rows)
    n_k = 1 + max(r[2] for r in rows)
    n_rows = len(rows)
    has_latent = k is not None
    n_chunks = (k.shape[1] // tk) if has_latent else 0
    args, in_specs = [], []
    for j in range(n_q):
        args.append(q)
        in_specs.append(pl.BlockSpec((1, tq, LANE), lambda b, g, i, j=j: (b, i, q_map(g, j))))
    for j in range(n_k):
        args.append(kc)
        in_specs.append(pl.BlockSpec((1, t_c, LANE), lambda b, g, i, j=j: (b, 0, k_map(g, j))))
    args.append(vc)
    in_specs.append(pl.BlockSpec((1, t_c, LANE), lambda b, g, i: (b, 0, v_map(g))))
    if has_latent:
        t_k = k.shape[1]
        for j in range(n_k):
            args.append(k)
            in_specs.append(pl.BlockSpec((1, t_k, LANE), lambda b, g, i, j=j: (b, 0, k_map(g, j))))
        args.append(v)
        in_specs.append(pl.BlockSpec((1, t_k, LANE), lambda b, g, i: (b, 0, v_map(g))))
    for e in extras:
        args.append(e)
        in_specs.append(pl.BlockSpec(e.shape, lambda b, g, i: (0, 0)))
    ow = out_width // n_groups
    s_width = max(tk, t_c) if has_latent else t_c
    return pl.pallas_call(
        functools.partial(_dense_attn_kernel, mode=mode, n_q=n_q, n_k=n_k, rows=rows, tq=tq, tk=tk,
                          n_chunks=n_chunks, n_parts=ATTN_PARTS, lambda_init=lambda_init),
        grid=(bsz, n_groups, t_q // tq),
        in_specs=in_specs,
        out_specs=pl.BlockSpec((1, tq, ow), lambda b, g, i: (b, i, g)),
        out_shape=jax.ShapeDtypeStruct((bsz, t_q, out_width), BF16),
        scratch_shapes=[
            pltpu.VMEM((n_rows * tq, LANE), BF16),
            pltpu.VMEM((n_rows * tq, LANE), F32),
            pltpu.VMEM((n_rows * tq, 2 * LANE), F32),
            pltpu.VMEM((n_rows * tq, s_width), F32),
            pltpu.VMEM((n_rows * tq, s_width), F32),
        ],
        compiler_params=_cparams(3),
        name=name,
    )(*args)


def _swa_kernel(*refs, latent, tq, n_blocks):
    it = iter(refs)
    sink_ref = next(it)
    q_ref = next(it)
    kc_ref, vc_ref = next(it), next(it)
    if latent:
        kp_ref, k0_ref, kn_ref = next(it), next(it), next(it)
        vp_ref, v0_ref, vn_ref = next(it), next(it), next(it)
    o_ref = next(it)
    i = pl.program_id(1)
    per_g = D_HEADS // D_KV_HEADS
    n_pairs = per_g // 2
    lo = _lane_lo((tq, LANE))
    if latent:
        r = lax.broadcasted_iota(jnp.int32, (tq, tq), 0)
        cidx = lax.broadcasted_iota(jnp.int32, (tq, tq), 1)
        ok_prev = jnp.logical_and(cidx >= r, i > 0)
        ok_next = jnp.logical_and(cidx <= r, i < n_blocks - 1)
        ok_pc = jnp.concatenate([ok_prev, jnp.full((tq, tq), True)], axis=1)
        ok_pc = jnp.concatenate([ok_pc] * n_pairs, axis=0)
        ok_next = jnp.concatenate([ok_next] * n_pairs, axis=0)

    def scores(g, e):
        slab = g ^ e
        sl = slice(slab * LANE, (slab + 1) * LANE)
        qrows, sinks = [], []
        for pi in range(n_pairs):
            ps = g * n_pairs + pi
            qv = q_ref[0, :, ps * LANE:(ps + 1) * LANE]
            qrows.append(jnp.where(lo if e == 0 else jnp.logical_not(lo), qv, jnp.zeros_like(qv)))
            sinks.append(jnp.full((tq, LANE), sink_ref[g * per_g + 2 * pi + e], F32))
        qst = jnp.concatenate(qrows, axis=0)
        sink = jnp.concatenate(sinks, axis=0)
        s_list = [_dot_nt(qst, kc_ref[0, :, sl])]
        if latent:
            k_pc = jnp.concatenate([kp_ref[0, :, sl], k0_ref[0, :, sl]], axis=0)
            s_list.append(jnp.where(ok_pc, _dot_nt(qst, k_pc), NEG_BIG))
            s_list.append(jnp.where(ok_next, _dot_nt(qst, kn_ref[0, :, sl]), NEG_BIG))
        return jnp.concatenate(s_list, axis=1), sink, sl

    def softmax_pv(s, sink, sl):
        v_list = [vc_ref[0, :, sl]]
        if latent:
            v_list += [vp_ref[0, :, sl], v0_ref[0, :, sl], vn_ref[0, :, sl]]
        vv = jnp.concatenate(v_list, axis=0)
        vv = jnp.concatenate([vv, jnp.ones(vv.shape, BF16)], axis=1)
        m = jnp.maximum(jnp.max(s, axis=1, keepdims=True), sink)
        ps = [jnp.exp2(s[:, j * LANE:(j + 1) * LANE] - m).astype(BF16) for j in range(s.shape[1] // LANE)]
        pv = _dot(jnp.concatenate(ps, axis=1), vv)
        return pv[:, 0:LANE] / (pv[:, LANE:2 * LANE] + jnp.exp2(sink - m))

    groups = [(g, e) for g in range(D_KV_HEADS) for e in range(2)]
    outs = []
    pending = scores(*groups[0])
    for gi in range(len(groups)):
        nxt = scores(*groups[gi + 1]) if gi + 1 < len(groups) else None
        outs.append(softmax_pv(*pending))
        pending = nxt
    for g in range(D_KV_HEADS):
        for pi in range(n_pairs):
            ps = g * n_pairs + pi
            out = jnp.where(lo, outs[2 * g][pi * tq:(pi + 1) * tq], outs[2 * g + 1][pi * tq:(pi + 1) * tq])
            o_ref[0, :, ps * LANE:(ps + 1) * LANE] = out.astype(o_ref.dtype)


def _swa_call(q, kc, vc, k, v, sinks_log2, tq, name):
    bsz, t_q, width = q.shape
    t_c = kc.shape[1]
    latent = k is not None
    n_blocks = t_q // tq
    args = [sinks_log2, q, kc, vc]
    in_specs = [
        pl.BlockSpec(memory_space=pltpu.SMEM),
        pl.BlockSpec((1, tq, width), lambda b, i: (b, i, 0)),
        pl.BlockSpec((1, t_c, 2 * LANE), lambda b, i: (b, 0, 0)),
        pl.BlockSpec((1, t_c, 2 * LANE), lambda b, i: (b, 0, 0)),
    ]
    if latent:
        assert tq == WINDOW
        prev_map = lambda b, i: (b, jnp.maximum(i - 1, 0), 0)
        cur_map = lambda b, i: (b, i, 0)
        next_map = lambda b, i: (b, jnp.minimum(i + 1, n_blocks - 1), 0)
        for arr in (k, v):
            for mp in (prev_map, cur_map, next_map):
                args.append(arr)
                in_specs.append(pl.BlockSpec((1, tq, 2 * LANE), mp))
    return pl.pallas_call(
        functools.partial(_swa_kernel, latent=latent, tq=tq, n_blocks=n_blocks),
        grid=(bsz, n_blocks),
        in_specs=in_specs,
        out_specs=pl.BlockSpec((1, tq, width), lambda b, i: (b, i, 0)),
        out_shape=jax.ShapeDtypeStruct((bsz, t_q, width), BF16),
        compiler_params=_cparams(2),
        name=name,
    )(*args)


def _out_kernel(x_ref, o_ref, mod_ref, wg_ref, w_ref, lng_ref, lnb_ref, y_ref, *, alpha):
    tm, d = x_ref.shape[1], x_ref.shape[2]
    shift, scale, gate = mod_ref[0, :, 0:d], mod_ref[0, :, d:2 * d], mod_ref[0, :, 2 * d:3 * d]
    rs = min(tm, OUT_SUB_ROWS)
    n_sub = tm // rs

    def branch(r0):
        xs = x_ref[0, r0:r0 + rs, :]
        hb = (xs * (1.0 + scale) + shift).astype(BF16)
        g = _silu(_dot(hb, wg_ref[...]))
        u = (o_ref[0, r0:r0 + rs, :].astype(F32) * g).astype(BF16)
        return alpha * xs + gate * _dot(u, w_ref[...])

    def norm_store(r0, z):
        zc = z - jnp.mean(z, axis=-1, keepdims=True)
        var = jnp.mean(zc * zc, axis=-1, keepdims=True)
        y_ref[0, r0:r0 + rs, :] = zc * lax.rsqrt(var + NORM_EPS) * lng_ref[...] + lnb_ref[...]

    zs = [branch(i * rs) for i in range(n_sub)]
    for i in range(n_sub):
        norm_store(i * rs, zs[i])


def _out_call(x, o, w_gate, mod, w, ln_g, ln_b, alpha, tm, name):
    bsz, t_len, d = x.shape
    width = o.shape[-1]
    return pl.pallas_call(
        functools.partial(_out_kernel, alpha=alpha),
        grid=(bsz, t_len // tm),
        in_specs=[
            pl.BlockSpec((1, tm, d), lambda b, i: (b, i, 0)),
            pl.BlockSpec((1, tm, width), lambda b, i: (b, i, 0)),
            pl.BlockSpec((1, 1, mod.shape[-1]), lambda b, i: (b, 0, 0)),
            pl.BlockSpec(w_gate.shape, lambda b, i: (0, 0)),
            pl.BlockSpec(w.shape, lambda b, i: (0, 0)),
            pl.BlockSpec((1, d), lambda b, i: (0, 0)),
            pl.BlockSpec((1, d), lambda b, i: (0, 0)),
        ],
        out_specs=pl.BlockSpec((1, tm, d), lambda b, i: (b, i, 0)),
        out_shape=jax.ShapeDtypeStruct((bsz, t_len, d), F32),
        compiler_params=_cparams(2),
        name=name,
    )(x, o, mod, w_gate, w, ln_g.reshape(1, d), ln_b.reshape(1, d))


PROJ_ROWS = 512
OUT_ROWS = 1024
OUT_SUB_ROWS = 256
ATTN_ROWS = 1024
ATTN_TK = 1024
ATTN_PARTS = 4


def _tiles(t_len, n_row_groups=1):
    tm = min(PROJ_ROWS, t_len)
    tq = min(ATTN_ROWS // n_row_groups, t_len)
    tk = min(ATTN_TK, t_len)
    return tm, tq, tk


def _mla_layer(x, ctx, mod, mod_c, need_ctx, w_in, g_qa, w_qb, g_kva, w_kvb):
    t_len, t_c = x.shape[1], ctx.shape[1]
    tm, tq, tk = _tiles(t_len, 2)
    d = w_in.shape[0]
    o0, o1, o2 = A_Q_LORA, A_Q_LORA + A_KV_LORA, A_Q_LORA + A_KV_LORA + A_ROPE
    hr = A_ROPE // 2
    na = HALF - hr
    w_kpe = w_in[:, o1:o2]
    kpe_pad = _mla_slab(w_kpe[:, :hr], jnp.zeros((d, na), F32), w_kpe[:, hr:], jnp.zeros((d, A_NOPE - na), F32))
    w_in_r = jnp.concatenate([w_in[:, :o1], kpe_pad], axis=1).astype(BF16)
    w_gate = w_in[:, o2:].astype(BF16)
    qb = w_qb.reshape(A_Q_LORA, A_HEADS, A_NOPE + A_ROPE)
    w_qb_r = _mla_slab(qb[:, :, A_NOPE:A_NOPE + hr], qb[:, :, :na], qb[:, :, A_NOPE + hr:], qb[:, :, na:A_NOPE])
    w_qb_r = w_qb_r.reshape(A_Q_LORA, A_HEADS * LANE).astype(BF16)
    kvb = w_kvb.reshape(A_KV_LORA, A_HEADS, A_NOPE + A_V)
    zk = jnp.zeros((A_KV_LORA, A_HEADS, hr), F32)
    k_pad = _mla_slab(zk, kvb[:, :, :na], zk, kvb[:, :, na:A_NOPE]).reshape(A_KV_LORA, A_HEADS * LANE)
    w_kvb_r = jnp.concatenate([k_pad, kvb[:, :, A_NOPE:].reshape(A_KV_LORA, A_HEADS * A_V)], axis=1).astype(BF16)
    q_scale = (A_NOPE + A_ROPE) ** -0.5 * LOG2E
    gq, gk = g_qa.reshape(1, -1), g_kva.reshape(1, -1)
    tabs = _rope_tables(t_len, A_ROPE, "mla")
    q, k, v = _mla_proj_call(x, mod, tabs, w_in_r, gq, w_qb_r, gk, w_kvb_r, q_scale, tm, "mla_proj")
    q_c, k_c, v_c = _mla_proj_call(ctx, mod_c, None, w_in_r, gq, w_qb_r, gk, w_kvb_r, q_scale, t_c, "mla_proj_ctx")
    rows = ((0, None, 0), (1, None, 1))
    maps = dict(n_groups=A_HEADS // 2, q_map=lambda gi, j: 2 * gi + j, k_map=lambda gi, j: 2 * gi + j,
                v_map=lambda gi: gi, rows=rows, out_width=A_HEADS * A_V, lambda_init=0.0)
    o = _dense_attn_call("pair", q, k_c, v_c, k, v, (), tq=tq, tk=tk, name="mla_attn", **maps)
    o_c = _dense_attn_call("pair", q_c, k_c, v_c, None, None, (), tq=t_c, tk=tk, name="mla_attn_ctx", **maps) \
        if need_ctx else None
    return o, w_gate, o_c


def _diff_layer(x, ctx, mod, mod_c, need_ctx, layer_idx, w_in, lam, g_sub):
    t_len, t_c = x.shape[1], ctx.shape[1]
    tm, tq, tk = _tiles(t_len, 2)
    n = 2 * B_HEADS * B_HEAD
    q_scale = B_HEAD ** -0.5 * LOG2E
    segs = ((0, 0, n, "rope", q_scale, None), (1, n, n, "rope", 1.0, None), (2, 2 * n, n, "plain", 1.0, None))
    widths = (n, n, n)
    wb = w_in[:, :3 * n].astype(BF16)
    w_gate = w_in[:, 3 * n:].astype(BF16)
    tabs = _rope_tables(t_len, B_HEAD, "unit")
    q, k, v = _proj_call(x, mod, tabs, wb, None, segs, widths, B_HEAD // 2, tm, "diff_proj")
    q_c, k_c, v_c = _proj_call(ctx, mod_c, None, wb, None, segs, widths, B_HEAD // 2, t_c, "diff_proj_ctx")
    lambda_init = 0.8 - 0.6 * math.exp(-0.3 * layer_idx)
    rows = ((0, "lo", 0), (0, "hi", 0))
    extras = (lam.astype(F32), g_sub.reshape(1, -1).astype(F32))
    maps = dict(n_groups=B_HEADS, q_map=lambda gi, j: gi, k_map=lambda gi, j: gi, v_map=lambda gi: gi,
                rows=rows, out_width=n, lambda_init=lambda_init)
    o = _dense_attn_call("diff", q, k_c, v_c, k, v, extras, tq=tq, tk=tk, name="diff_attn", **maps)
    o_c = _dense_attn_call("diff", q_c, k_c, v_c, None, None, extras, tq=t_c, tk=tk, name="diff_attn_ctx", **maps) \
        if need_ctx else None
    return o, w_gate, o_c


def _gqa_layer(x, ctx, mod, mod_c, need_ctx, w_in, g_q, g_k):
    t_len, t_c = x.shape[1], ctx.shape[1]
    tm, tq, tk = _tiles(t_len, C_HEADS // C_KV_HEADS)
    nq, nkv = C_HEADS * C_HEAD, C_KV_HEADS * C_HEAD
    q_scale = C_HEAD ** -0.5 * LOG2E
    segs = ((0, 0, nq, "rope", q_scale, 0), (1, nq, nkv, "rope", 1.0, 1), (2, nq + nkv, nkv, "plain", 1.0, None))
    widths = (nq, nkv, nkv)
    wb = w_in[:, :nq + 2 * nkv].astype(BF16)
    w_gate = w_in[:, nq + 2 * nkv:].astype(BF16)
    gains = jnp.stack([g_q, g_k]).astype(F32)
    tabs = _rope_tables(t_len, C_HEAD, "unit")
    q, k, v = _proj_call(x, mod, tabs, wb, gains, segs, widths, C_HEAD // 2, tm, "gqa_proj")
    q_c, k_c, v_c = _proj_call(ctx, mod_c, None, wb, gains, segs, widths, C_HEAD // 2, t_c, "gqa_proj_ctx")
    per = C_HEADS // C_KV_HEADS
    rows = tuple((j, None, 0) for j in range(per))
    maps = dict(n_groups=C_KV_HEADS, q_map=lambda gi, j: per * gi + j, k_map=lambda gi, j: gi, v_map=lambda gi: gi,
                rows=rows, out_width=nq, lambda_init=0.0)
    o = _dense_attn_call("stack", q, k_c, v_c, k, v, (), tq=tq, tk=tk, name="gqa_attn", **maps)
    o_c = _dense_attn_call("stack", q_c, k_c, v_c, None, None, (), tq=t_c, tk=tk, name="gqa_attn_ctx", **maps) \
        if need_ctx else None
    return o, w_gate, o_c


def _swa_layer(x, ctx, mod, mod_c, need_ctx, w_in, sinks):
    t_len, t_c = x.shape[1], ctx.shape[1]
    tm, _, _ = _tiles(t_len)
    nq, nkv = D_HEADS * D_HEAD, D_KV_HEADS * D_HEAD
    wk = w_in[:, nq:nq + nkv]
    wv = w_in[:, nq + nkv:nq + 2 * nkv]
    swap = lambda w: jnp.concatenate([w, w[:, D_HEAD:], w[:, :D_HEAD]], axis=1)
    wb = jnp.concatenate([w_in[:, :nq], swap(wk), swap(wv)], axis=1).astype(BF16)
    w_gate = w_in[:, nq + 2 * nkv:].astype(BF16)
    q_scale = D_HEAD ** -0.5 * LOG2E
    segs = ((0, 0, nq, "rope", q_scale, None), (1, nq, 2 * nkv, "rope", 1.0, None),
            (2, nq + 2 * nkv, 2 * nkv, "plain", 1.0, None))
    widths = (nq, 2 * nkv, 2 * nkv)
    tabs = _rope_tables(t_len, D_HEAD, "unit")
    q, k, v = _proj_call(x, mod, tabs, wb, None, segs, widths, D_HEAD // 2, tm, "swa_proj")
    q_c, k_c, v_c = _proj_call(ctx, mod_c, None, wb, None, segs, widths, D_HEAD // 2, t_c, "swa_proj_ctx")
    sinks_log2 = sinks.astype(F32) * LOG2E
    o = _swa_call(q, k_c, v_c, k, v, sinks_log2, WINDOW, "swa_attn")
    o_c = _swa_call(q_c, k_c, v_c, None, None, sinks_log2, t_c, "swa_attn_ctx") if need_ctx else None
    return o, w_gate, o_c


def kernel(x, c, ctx, c_ctx, ada_w, ada_b, out_w, ln_g, ln_b, mla_w_in, mla_g_qa, mla_w_qb, mla_g_kva, mla_w_kvb,
           diff_w_in, diff_lambda, diff_g_sub, gqa_w_in, gqa_g_q, gqa_g_k, swa_w_in, swa_sink):
    depth = ada_w.shape[0]
    bsz, t_len, d = x.shape
    t_c = ctx.shape[1]
    alpha = (2 * depth) ** 0.25
    n_rows = 8
    assert bsz + 1 <= n_rows
    cvec = jnp.concatenate([c, c_ctx[None, :], jnp.zeros((n_rows - bsz - 1, d), F32)], axis=0)
    mods = _adaln_call(cvec, ada_w, ada_b)
    tm_out = min(OUT_ROWS, t_len)
    for i in range(depth):
        kind, j = i % 4, i // 4
        need_ctx = i < depth - 1
        mod = mods[i, 0:bsz][:, None, :]
        mod_c = jnp.broadcast_to(mods[i, bsz][None, None, :], (bsz, 1, 3 * d))
        if kind == 0:
            o, w_gate, o_c = _mla_layer(x, ctx, mod, mod_c, need_ctx, mla_w_in[j], mla_g_qa[j], mla_w_qb[j],
                                        mla_g_kva[j], mla_w_kvb[j])
        elif kind == 1:
            o, w_gate, o_c = _diff_layer(x, ctx, mod, mod_c, need_ctx, i, diff_w_in[j], diff_lambda[j], diff_g_sub[j])
        elif kind == 2:
            o, w_gate, o_c = _gqa_layer(x, ctx, mod, mod_c, need_ctx, gqa_w_in[j], gqa_g_q[j], gqa_g_k[j])
        else:
            o, w_gate, o_c = _swa_layer(x, ctx, mod, mod_c, need_ctx, swa_w_in[j], swa_sink[j])
        wo = out_w[i].astype(BF16)
        x = _out_call(x, o, w_gate, mod, wo, ln_g[i], ln_b[i], alpha, tm_out, "out_proj")
        if need_ctx:
            ctx = _out_call(ctx, o_c, w_gate, mod_c, wo, ln_g[i], ln_b[i], alpha, t_c, "out_proj_ctx")
    return x
```

```python
import functools
import math

import jax
import jax.numpy as jnp
from jax import lax
from jax.experimental import pallas as pl
from jax.experimental.pallas import tpu as pltpu

D_MODEL = 1024
GRID_W = 64
WINDOW = 128
ROPE_THETA = 10000.0
NORM_EPS = 1e-6
A_HEADS, A_Q_LORA, A_KV_LORA, A_NOPE, A_ROPE, A_V = 16, 256, 128, 64, 32, 64
B_HEADS, B_HEAD = 8, 64
C_HEADS, C_KV_HEADS, C_HEAD = 8, 2, 128
D_HEADS, D_KV_HEADS, D_HEAD = 16, 2, 64

LANE = 128
HALF = LANE // 2
MXU_WIDTH = 256
LOG2E = math.log2(math.e)
NEG_BIG = -1e30
VMEM_LIMIT = 56 * 1024 * 1024
BF16 = jnp.bfloat16
F32 = jnp.float32


def _cparams(n_grid):
    return pltpu.CompilerParams(dimension_semantics=("arbitrary",) * n_grid, vmem_limit_bytes=VMEM_LIMIT)


def _silu(x):
    return x / (1.0 + jnp.exp(-x))


def _dot(a, b):
    return jnp.dot(a, b, preferred_element_type=F32)


def _dot_nt(a, b):
    return lax.dot_general(a, b, (((1,), (1,)), ((), ())), preferred_element_type=F32)


def _adaln_kernel(c_ref, w_ref, b_ref, o_ref):
    sc = _silu(c_ref[...])
    o_ref[0] = jnp.dot(sc, w_ref[0], preferred_element_type=F32, precision=lax.Precision.HIGHEST) + b_ref[0]


def _adaln_call(cvec, ada_w, ada_b):
    depth, d, n = ada_w.shape
    rows = cvec.shape[0]
    tn = 1024
    return pl.pallas_call(
        _adaln_kernel,
        grid=(depth, n // tn),
        in_specs=[
            pl.BlockSpec((rows, d), lambda i, j: (0, 0)),
            pl.BlockSpec((1, d, tn), lambda i, j: (i, 0, j)),
            pl.BlockSpec((1, 1, tn), lambda i, j: (i, 0, j)),
        ],
        out_specs=pl.BlockSpec((1, rows, tn), lambda i, j: (i, 0, j)),
        out_shape=jax.ShapeDtypeStruct((depth, rows, n), F32),
        compiler_params=_cparams(2),
        name="adaln",
    )(cvec, ada_w, ada_b.reshape(depth, 1, n))


def _mla_slab(pe_lo, nope_a, pe_hi, nope_b):
    used = pe_lo.shape[-1] + nope_a.shape[-1] + pe_hi.shape[-1] + nope_b.shape[-1]
    pad = jnp.zeros(pe_lo.shape[:-1] + (LANE - used,), pe_lo.dtype)
    return jnp.concatenate([pe_lo, nope_a, pe_hi, nope_b, pad], axis=-1)


def _rope_tables(t_len, rot_dim, layout):
    n_freq = rot_dim // 4
    inv_freq = ROPE_THETA ** (-jnp.arange(n_freq, dtype=F32) / n_freq)

    def build(n_pos, is_row):
        ang = jnp.arange(n_pos, dtype=F32)[:, None] * inv_freq
        z = jnp.zeros((n_pos, n_freq), F32)
        cos = jnp.concatenate([jnp.cos(ang), z] if is_row else [z, jnp.cos(ang)], axis=-1)
        sin = jnp.concatenate([jnp.sin(ang), z] if is_row else [z, jnp.sin(ang)], axis=-1)
        one = 0.0 if is_row else 1.0
        if layout == "mla":
            na = HALF - A_ROPE // 2
            ones_a, ones_b = jnp.full((n_pos, na), one, F32), jnp.full((n_pos, A_NOPE - na), one, F32)
            return (_mla_slab(cos, ones_a, cos, ones_b), _mla_slab(-sin, 0 * ones_a, sin, 0 * ones_b))
        reps = LANE // rot_dim
        c = jnp.tile(jnp.concatenate([cos, cos], axis=-1), (1, reps))
        if rot_dim == LANE:
            return (c, jnp.concatenate([-sin, sin], axis=-1))
        zero = jnp.zeros_like(sin)
        return (c, jnp.tile(jnp.concatenate([-sin, zero], axis=-1), (1, reps)),
                jnp.tile(jnp.concatenate([zero, sin], axis=-1), (1, reps)))

    return build(t_len // GRID_W, True), build(GRID_W, False)


def _rope_block(row_refs, col_refs, g0, n_groups):
    tabs = []
    for rr, cr in zip(row_refs, col_refs):
        col_part = cr[...]
        tabs.append(jnp.concatenate(
            [jnp.broadcast_to(rr[g:g + 1, :], (GRID_W, LANE)) + col_part for g in range(g0, g0 + n_groups)], axis=0))
    return tabs


def _rope_slab(x, tabs, half):
    if half == HALF:
        c, sn = tabs
        return x * c + pltpu.roll(x, HALF, 1) * sn
    c, s_lo, s_hi = tabs
    return x * c + pltpu.roll(x, LANE - half, 1) * s_lo + pltpu.roll(x, half, 1) * s_hi


def _rms(x, g):
    return x * lax.rsqrt(jnp.mean(x * x, axis=-1, keepdims=True) + NORM_EPS) * g


PROJ_CHUNK = 512
PROJ_SUB_ROWS = 256


def _proj_kernel(*refs, segs, rope_half, has_rope, has_norm):
    it = iter(refs)
    x_ref, mod_ref = next(it), next(it)
    if has_rope:
        n_tabs = 2 if rope_half == HALF else 3
        row_refs = [next(it) for _ in range(n_tabs)]
        col_refs = [next(it) for _ in range(n_tabs)]
    w_ref = next(it)
    gn_ref = next(it) if has_norm else None
    out_refs = list(it)
    tm, d = x_ref.shape[1], x_ref.shape[2]
    shift, scale = mod_ref[0, :, 0:d], mod_ref[0, :, d:2 * d]
    rs = min(tm, PROJ_SUB_ROWS)
    hbs = [(x_ref[0, r0:r0 + rs, :] * (1.0 + scale) + shift).astype(BF16) for r0 in range(0, tm, rs)]
    if has_rope:
        tabs = [_rope_block(row_refs, col_refs, ri * (rs // GRID_W), rs // GRID_W) for ri in range(tm // rs)]
    items = [(ri, seg, cc, min(PROJ_CHUNK, seg[2] - cc))
             for ri in range(tm // rs) for seg in segs for cc in range(0, seg[2], PROJ_CHUNK)]

    def matmul(item):
        ri, (_, col0, _, _, _, _), cc, cw = item
        return _dot(hbs[ri], w_ref[:, col0 + cc:col0 + cc + cw])

    def finish(item, t):
        ri, (oi, _, _, kind, q_scale, norm_row), cc, cw = item
        rows = slice(ri * rs, (ri + 1) * rs)
        if kind == "plain":
            y = t
        else:
            slabs = []
            for s in range(cw // LANE):
                xs = t[:, s * LANE:(s + 1) * LANE]
                if norm_row is not None:
                    xs = _rms(xs, gn_ref[norm_row:norm_row + 1, :])
                if has_rope:
                    xs = _rope_slab(xs, tabs[ri], rope_half)
                if q_scale != 1.0:
                    xs = xs * q_scale
                slabs.append(xs)
            y = jnp.concatenate(slabs, axis=1) if len(slabs) > 1 else slabs[0]
        out_refs[oi][0, rows, cc:cc + cw] = y.astype(BF16)

    t_cur = matmul(items[0])
    for n, item in enumerate(items):
        t_next = matmul(items[n + 1]) if n + 1 < len(items) else None
        finish(item, t_cur)
        t_cur = t_next


def _proj_call(x, mod, tabs, w, gains, segs, out_widths, rope_half, tm, name):
    bsz, t_len, d = x.shape
    has_rope = tabs is not None
    has_norm = gains is not None
    args = [x, mod]
    in_specs = [
        pl.BlockSpec((1, tm, d), lambda b, i: (b, i, 0)),
        pl.BlockSpec((1, 1, mod.shape[-1]), lambda b, i: (b, 0, 0)),
    ]
    if has_rope:
        row_tabs, col_tabs = tabs
        for tab in row_tabs:
            args.append(tab)
            in_specs.append(pl.BlockSpec((tm // GRID_W, LANE), lambda b, i: (i, 0)))
        for tab in col_tabs:
            args.append(tab)
            in_specs.append(pl.BlockSpec((GRID_W, LANE), lambda b, i: (0, 0)))
    args.append(w)
    in_specs.append(pl.BlockSpec(w.shape, lambda b, i: (0, 0)))
    if has_norm:
        args.append(gains)
        in_specs.append(pl.BlockSpec(gains.shape, lambda b, i: (0, 0)))
    return pl.pallas_call(
        functools.partial(_proj_kernel, segs=segs, rope_half=rope_half, has_rope=has_rope, has_norm=has_norm),
        grid=(bsz, t_len // tm),
        in_specs=in_specs,
        out_specs=[pl.BlockSpec((1, tm, ow), lambda b, i: (b, i, 0)) for ow in out_widths],
        out_shape=[jax.ShapeDtypeStruct((bsz, t_len, ow), BF16) for ow in out_widths],
        compiler_params=_cparams(2),
        name=name,
    )(*args)


def _mla_proj_kernel(*refs, has_rope, q_scale):
    it = iter(refs)
    x_ref, mod_ref = next(it), next(it)
    if has_rope:
        row_refs = [next(it), next(it)]
        col_refs = [next(it), next(it)]
    w_in_ref, gqa_ref, w_qb_ref, gkva_ref, w_kvb_ref = next(it), next(it), next(it), next(it), next(it)
    q_ref, k_ref, v_ref = next(it), next(it), next(it)
    half = HALF
    n_lat =A_Q_LORA + A_KV_LORA + LANE
    n_qk = A_HEADS * LANE
    n_v = A_HEADS * A_V
    tm, d = x_ref.shape[1], x_ref.shape[2]
    shift, scale = mod_ref[0, :, 0:d], mod_ref[0, :, d:2 * d]
    rs = min(tm, PROJ_SUB_ROWS)
    for r0 in range(0, tm, rs):
        rows = slice(r0, r0 + rs)
        hb = (x_ref[0, rows, :] * (1.0 + scale) + shift).astype(BF16)
        t = _dot(hb, w_in_ref[...])
        qn = _rms(t[:, 0:A_Q_LORA], gqa_ref[...]).astype(BF16)
        kn = _rms(t[:, A_Q_LORA:A_Q_LORA + A_KV_LORA], gkva_ref[...]).astype(BF16)
        kpe = t[:, A_Q_LORA + A_KV_LORA:n_lat]
        if has_rope:
            tabs = _rope_block(row_refs, col_refs, r0 // GRID_W, rs // GRID_W)
            kpe = _rope_slab(kpe, tabs, half)
        for cc in range(0, n_qk, PROJ_CHUNK):
            tq = _dot(qn, w_qb_ref[:, cc:cc + PROJ_CHUNK])
            tk = _dot(kn, w_kvb_ref[:, cc:cc + PROJ_CHUNK])
            qs, ks = [], []
            for s in range(PROJ_CHUNK // LANE):
                xq = tq[:, s * LANE:(s + 1) * LANE]
                if has_rope:
                    xq = _rope_slab(xq, tabs, half)
                qs.append(xq * q_scale)
                ks.append(tk[:, s * LANE:(s + 1) * LANE] + kpe)
            q_ref[0, rows, cc:cc + PROJ_CHUNK] = jnp.concatenate(qs, axis=1).astype(BF16)
            k_ref[0, rows, cc:cc + PROJ_CHUNK] = jnp.concatenate(ks, axis=1).astype(BF16)
        for cc in range(0, n_v, PROJ_CHUNK):
            v_ref[0, rows, cc:cc + PROJ_CHUNK] = _dot(kn, w_kvb_ref[:, n_qk + cc:n_qk + cc + PROJ_CHUNK]).astype(BF16)


def _mla_proj_call(x, mod, tabs, w_in, g_qa, w_qb, g_kva, w_kvb, q_scale, tm, name):
    bsz, t_len, d = x.shape
    has_rope = tabs is not None
    args = [x, mod]
    in_specs = [
        pl.BlockSpec((1, tm, d), lambda b, i: (b, i, 0)),
        pl.BlockSpec((1, 1, mod.shape[-1]), lambda b, i: (b, 0, 0)),
    ]
    if has_rope:
        row_tabs, col_tabs = tabs
        for tab in row_tabs:
            args.append(tab)
            in_specs.append(pl.BlockSpec((tm // GRID_W, LANE), lambda b, i: (i, 0)))
        for tab in col_tabs:
            args.append(tab)
            in_specs.append(pl.BlockSpec((GRID_W, LANE), lambda b, i: (0, 0)))
    for a in (w_in, g_qa, w_qb, g_kva, w_kvb):
        args.append(a)
        in_specs.append(pl.BlockSpec(a.shape, lambda b, i: (0, 0)))
    out_widths = (A_HEADS * LANE, A_HEADS * LANE, A_HEADS * A_V)
    return pl.pallas_call(
        functools.partial(_mla_proj_kernel, has_rope=has_rope, q_scale=q_scale),
        grid=(bsz, t_len // tm),
        in_specs=in_specs,
        out_specs=[pl.BlockSpec((1, tm, ow), lambda b, i: (b, i, 0)) for ow in out_widths],
        out_shape=[jax.ShapeDtypeStruct((bsz, t_len, ow), BF16) for ow in out_widths],
        compiler_params=_cparams(2),
        name=name,
    )(*args)


def _lane_lo(shape):
    return lax.broadcasted_iota(jnp.int32, shape, len(shape) - 1) < HALF


def _dense_attn_kernel(*refs, mode, n_q, n_k, rows, tq, tk, n_chunks, n_parts, lambda_init):
    it = iter(refs)
    q_refs = [next(it) for _ in range(n_q)]
    kc_refs = [next(it) for _ in range(n_k)]
    vc_ref = next(it)
    if n_chunks:
        k_refs = [next(it) for _ in range(n_k)]
        v_ref = next(it)
    if mode == "diff":
        lam_ref, gsub_ref = next(it), next(it)
    o_ref = next(it)
    qs_scr, m_scr, acc_scr = next(it), next(it), next(it)
    buf_a, buf_b = next(it), next(it)

    n_rows = len(rows)
    staged = any(mask is not None for (_, mask, _) in rows)
    if staged:
        for r, (qi, mask, _) in enumerate(rows):
            q = q_refs[qi][0]
            lo = _lane_lo(q.shape)
            qs_scr[r * tq:(r + 1) * tq, :] = jnp.where(lo if mask == "lo" else jnp.logical_not(lo), q,
                                                        jnp.zeros_like(q))
    spans = []
    for ks in range(n_k):
        idx = [r for r, row in enumerate(rows) if row[2] == ks]
        spans.append((idx[0] * tq, (idx[-1] + 1) * tq))
    blocks_per_span = max(1, n_parts // n_k)
    row_blocks = []
    for (r0, r1) in spans:
        nb = (r1 - r0) // blocks_per_span
        row_blocks += [(r0 + i * nb, r0 + (i + 1) * nb) for i in range(blocks_per_span)]

    def q_rows(r0, r1):
        if staged:
            return qs_scr[r0:r1, :]
        parts = [q_refs[rows[r][0]][0] for r in range(r0 // tq, r1 // tq)]
        return jnp.concatenate(parts, axis=0) if len(parts) > 1 else parts[0]

    def qk_part(k_load, s_ref, c0, c1):
        for ks, (r0, r1) in enumerate(spans):
            s_ref[r0:r1, c0:c1] = _dot_nt(q_rows(r0, r1), k_load(ks, c0, c1))

    def sm_part(v_ext, s_ref, a, b, first):
        n_cols = v_ext.shape[0] // LANE
        mp = s_ref[a:b, 0:LANE]
        for j in range(1, n_cols):
            mp = jnp.maximum(mp, s_ref[a:b, j * LANE:(j + 1) * LANE])
        smax = jnp.max(mp, axis=1, keepdims=True)
        if first:
            m_new = jnp.broadcast_to(smax, (b - a, LANE))
        else:
            m_prev = m_scr[a:b, :]
            m_new = jnp.maximum(m_prev, smax)
        ps = [jnp.exp2(s_ref[a:b, j * LANE:(j + 1) * LANE] - m_new).astype(BF16) for j in range(n_cols)]
        p = jnp.concatenate(ps, axis=1) if n_cols > 1 else ps[0]
        pv = _dot(p, v_ext)
        if first:
            acc_scr[a:b, :] = pv
        else:
            alpha = jnp.exp2(m_prev - m_new)
            acc_scr[a:b, :] = jnp.concatenate([alpha, alpha], axis=1) * acc_scr[a:b, :] + pv
        m_scr[a:b, :] = m_new

    def ext(v):
        return jnp.concatenate([v, jnp.ones(v.shape, BF16)], axis=1)

    def k_loader(off):
        return lambda ks, c0, c1: k_refs[ks][0, off + c0:off + c1, :]

    def pipelined(v_ext, s_cur, first, k_next, s_next, next_width):
        kw = max(MXU_WIDTH, next_width // n_parts)
        for pi, (a, b) in enumerate(row_blocks):
            sm_part(v_ext, s_cur, a, b, first)
            if pi * kw < next_width:
                qk_part(k_next, s_next, pi * kw, (pi + 1) * kw)

    bufs = (buf_a, buf_b)
    t_c = vc_ref.shape[1]
    ctx_loader = lambda ks, c0, c1: kc_refs[ks][0, c0:c1, :]
    if n_chunks:
        qk_part(k_loader(0), bufs[0], 0, tk)
    else:
        qk_part(ctx_loader, bufs[0], 0, t_c)
    for c in range(n_chunks):
        v_ext = ext(v_ref[0, c * tk:(c + 1) * tk, :])
        if c == n_chunks - 1:
            pipelined(v_ext, bufs[c % 2], c == 0, ctx_loader, bufs[(c + 1) % 2], t_c)
        else:
            pipelined(v_ext, bufs[c % 2], c == 0, k_loader((c + 1) * tk), bufs[(c + 1) % 2], tk)
    vc_ext = ext(vc_ref[0])
    for (a, b) in row_blocks:
        sm_part(vc_ext, bufs[n_chunks % 2], a, b, n_chunks == 0)

    acc = acc_scr[...]
    o = acc[:, 0:LANE] / acc[:, LANE:2 * LANE]
    if mode == "pair":
        out = jnp.where(_lane_lo((tq, LANE)), o[0:tq], o[tq:2 * tq])
    elif mode == "diff":
        lam = lam_ref[...]
        a = jnp.sum(lam[0:1, :] * lam[1:2, :], axis=1, keepdims=True)
        b = jnp.sum(lam[2:3, :] * lam[3:4, :], axis=1, keepdims=True)
        lam_full = jnp.exp(a) - jnp.exp(b) + lambda_init
        dlt = o[0:tq] - lam_full * o[tq:2 * tq]
        out = _rms(dlt, gsub_ref[...]) * (1.0 - lambda_init)
    else:
        out = jnp.concatenate([o[r * tq:(r + 1) * tq] for r in range(n_rows)], axis=1)
    o_ref[0] = out.astype(o_ref.dtype)


def _dense_attn_call(mode, q, kc, vc, k, v, extras, n_groups, q_map, k_map, v_map, rows, out_width, tq, tk,
                     lambda_init, name):
    bsz, t_q, _ = q.shape
    t_c = kc.shape[1]
    n_q = 1 + max(r[0] for r in rows)
    n_k = 1 + max(r[2] for r in rows)
    n_rows = len(rows)
    has_latent = k is not None
    n_chunks = (k.shape[1] // tk) if has_latent else 0
    args, in_specs = [], []
    for j in range(n_q):
        args.append(q)
        in_specs.append(pl.BlockSpec((1, tq, LANE), lambda b, g, i, j=j: (b, i, q_map(g, j))))
    for j in range(n_k):
        args.append(kc)
        in_specs.append(pl.BlockSpec((1, t_c, LANE), lambda b, g, i, j=j: (b, 0, k_map(g, j))))
    args.append(vc)
    in_specs.append(pl.BlockSpec((1, t_c, LANE), lambda b, g, i: (b, 0, v_map(g))))
    if has_latent:
        t_k = k.shape[1]
        for j in range(n_k):
            args.append(k)
            in_specs.append(pl.BlockSpec((1, t_k, LANE), lambda b, g, i, j=j: (b, 0, k_map(g, j))))
        args.append(v)
        in_specs.append(pl.BlockSpec((1, t_k, LANE), lambda b, g, i: (b, 0, v_map(g))))
    for e in extras:
        args.append(e)
        in_specs.append(pl.BlockSpec(e.shape, lambda b, g, i: (0, 0)))
    ow = out_width // n_groups
    s_width = max(tk, t_c) if has_latent else t_c
    return pl.pallas_call(
        functools.partial(_dense_attn_kernel, mode=mode, n_q=n_q, n_k=n_k, rows=rows, tq=tq, tk=tk,
                          n_chunks=n_chunks, n_parts=ATTN_PARTS, lambda_init=lambda_init),
        grid=(bsz, n_groups, t_q // tq),
        in_specs=in_specs,
        out_specs=pl.BlockSpec((1, tq, ow), lambda b, g, i: (b, i, g)),
        out_shape=jax.ShapeDtypeStruct((bsz, t_q, out_width), BF16),
        scratch_shapes=[
            pltpu.VMEM((n_rows * tq, LANE), BF16),
            pltpu.VMEM((n_rows * tq, LANE), F32),
            pltpu.VMEM((n_rows * tq, 2 * LANE), F32),
            pltpu.VMEM((n_rows * tq, s_width), F32),
            pltpu.VMEM((n_rows * tq, s_width), F32),
        ],
        compiler_params=_cparams(3),
        name=name,
    )(*args)


def _swa_kernel(*refs, latent, tq, n_blocks):
    it = iter(refs)
    sink_ref = next(it)
    q_ref = next(it)
    kc_ref, vc_ref = next(it), next(it)
    if latent:
        kp_ref, k0_ref, kn_ref = next(it), next(it), next(it)
        vp_ref, v0_ref, vn_ref = next(it), next(it), next(it)
    o_ref = next(it)
    i = pl.program_id(1)
    per_g = D_HEADS // D_KV_HEADS
    n_pairs = per_g // 2
    lo = _lane_lo((tq, LANE))
    if latent:
        r = lax.broadcasted_iota(jnp.int32, (tq, tq), 0)
        cidx = lax.broadcasted_iota(jnp.int32, (tq, tq), 1)
        ok_prev = jnp.logical_and(cidx >= r, i > 0)
        ok_next = jnp.logical_and(cidx <= r, i < n_blocks - 1)
        ok_pc = jnp.concatenate([ok_prev, jnp.full((tq, tq), True)], axis=1)
        ok_pc = jnp.concatenate([ok_pc] * n_pairs, axis=0)
        ok_next = jnp.concatenate([ok_next] * n_pairs, axis=0)

    def scores(g, e):
        slab = g ^ e
        sl = slice(slab * LANE, (slab + 1) * LANE)
        qrows, sinks = [], []
        for pi in range(n_pairs):
            ps = g * n_pairs + pi
            qv = q_ref[0, :, ps * LANE:(ps + 1) * LANE]
            qrows.append(jnp.where(lo if e == 0 else jnp.logical_not(lo), qv, jnp.zeros_like(qv)))
            sinks.append(jnp.full((tq, LANE), sink_ref[g * per_g + 2 * pi + e], F32))
        qst = jnp.concatenate(qrows, axis=0)
        sink = jnp.concatenate(sinks, axis=0)
        s_list = [_dot_nt(qst, kc_ref[0, :, sl])]
        if latent:
            k_pc = jnp.concatenate([kp_ref[0, :, sl], k0_ref[0, :, sl]], axis=0)
            s_list.append(jnp.where(ok_pc, _dot_nt(qst, k_pc), NEG_BIG))
            s_list.append(jnp.where(ok_next, _dot_nt(qst, kn_ref[0, :, sl]), NEG_BIG))
        return jnp.concatenate(s_list, axis=1), sink, sl

    def softmax_pv(s, sink, sl):
        v_list = [vc_ref[0, :, sl]]
        if latent:
            v_list += [vp_ref[0, :, sl], v0_ref[0, :, sl], vn_ref[0, :, sl]]
        vv = jnp.concatenate(v_list, axis=0)
        vv = jnp.concatenate([vv, jnp.ones(vv.shape, BF16)], axis=1)
        m = jnp.maximum(jnp.max(s, axis=1, keepdims=True), sink)
        ps = [jnp.exp2(s[:, j * LANE:(j + 1) * LANE] - m).astype(BF16) for j in range(s.shape[1] // LANE)]
        pv = _dot(jnp.concatenate(ps, axis=1), vv)
        return pv[:, 0:LANE] / (pv[:, LANE:2 * LANE] + jnp.exp2(sink - m))

    groups = [(g, e) for g in range(D_KV_HEADS) for e in range(2)]
    outs = []
    pending = scores(*groups[0])
    for gi in range(len(groups)):
        nxt = scores(*groups[gi + 1]) if gi + 1 < len(groups) else None
        outs.append(softmax_pv(*pending))
        pending = nxt
    for g in range(D_KV_HEADS):
        for pi in range(n_pairs):
            ps = g * n_pairs + pi
            out = jnp.where(lo, outs[2 * g][pi * tq:(pi + 1) * tq], outs[2 * g + 1][pi * tq:(pi + 1) * tq])
            o_ref[0, :, ps * LANE:(ps + 1) * LANE] = out.astype(o_ref.dtype)


def _swa_call(q, kc, vc, k, v, sinks_log2, tq, name):
    bsz, t_q, width = q.shape
    t_c = kc.shape[1]
    latent = k is not None
    n_blocks = t_q // tq
    args = [sinks_log2, q, kc, vc]
    in_specs = [
        pl.BlockSpec(memory_space=pltpu.SMEM),
        pl.BlockSpec((1, tq, width), lambda b, i: (b, i, 0)),
        pl.BlockSpec((1, t_c, 2 * LANE), lambda b, i: (b, 0, 0)),
        pl.BlockSpec((1, t_c, 2 * LANE), lambda b, i: (b, 0, 0)),
    ]
    if latent:
        assert tq == WINDOW
        prev_map = lambda b, i: (b, jnp.maximum(i - 1, 0), 0)
        cur_map = lambda b, i: (b, i, 0)
        next_map = lambda b, i: (b, jnp.minimum(i + 1, n_blocks - 1), 0)
        for arr in (k, v):
            for mp in (prev_map, cur_map, next_map):
                args.append(arr)
                in_specs.append(pl.BlockSpec((1, tq, 2 * LANE), mp))
    return pl.pallas_call(
        functools.partial(_swa_kernel, latent=latent, tq=tq, n_blocks=n_blocks),
        grid=(bsz, n_blocks),
        in_specs=in_specs,
        out_specs=pl.BlockSpec((1, tq, width), lambda b, i: (b, i, 0)),
        out_shape=jax.ShapeDtypeStruct((bsz, t_q, width), BF16),
        compiler_params=_cparams(2),
        name=name,
    )(*args)


def _out_kernel(x_ref, o_ref, mod_ref, wg_ref, w_ref, lng_ref, lnb_ref, y_ref, *, alpha):
    tm, d = x_ref.shape[1], x_ref.shape[2]
    shift, scale, gate = mod_ref[0, :, 0:d], mod_ref[0, :, d:2 * d], mod_ref[0, :, 2 * d:3 * d]
    rs = min(tm, OUT_SUB_ROWS)
    n_sub = tm // rs

    def branch(r0):
        xs = x_ref[0, r0:r0 + rs, :]
        hb = (xs * (1.0 + scale) + shift).astype(BF16)
        g = _silu(_dot(hb, wg_ref[...]))
        u = (o_ref[0, r0:r0 + rs, :].astype(F32) * g).astype(BF16)
        return alpha * xs + gate * _dot(u, w_ref[...])

    def norm_store(r0, z):
        zc = z - jnp.mean(z, axis=-1, keepdims=True)
        var = jnp.mean(zc * zc, axis=-1, keepdims=True)
        y_ref[0, r0:r0 + rs, :] = zc * lax.rsqrt(var + NORM_EPS) * lng_ref[...] + lnb_ref[...]

    zs = [branch(i * rs) for i in range(n_sub)]
    for i in range(n_sub):
        norm_store(i * rs, zs[i])


def _out_call(x, o, w_gate, mod, w, ln_g, ln_b, alpha, tm, name):
    bsz, t_len, d = x.shape
    width = o.shape[-1]
    return pl.pallas_call(
        functools.partial(_out_kernel, alpha=alpha),
        grid=(bsz, t_len // tm),
        in_specs=[
            pl.BlockSpec((1, tm, d), lambda b, i: (b, i, 0)),
            pl.BlockSpec((1, tm, width), lambda b, i: (b, i, 0)),
            pl.BlockSpec((1, 1, mod.shape[-1]), lambda b, i: (b, 0, 0)),
            pl.BlockSpec(w_gate.shape, lambda b, i: (0, 0)),
            pl.BlockSpec(w.shape, lambda b, i: (0, 0)),
            pl.BlockSpec((1, d), lambda b, i: (0, 0)),
            pl.BlockSpec((1, d), lambda b, i: (0, 0)),
        ],
        out_specs=pl.BlockSpec((1, tm, d), lambda b, i: (b, i, 0)),
        out_shape=jax.ShapeDtypeStruct((bsz, t_len, d), F32),
        compiler_params=_cparams(2),
        name=name,
    )(x, o, mod, w_gate, w, ln_g.reshape(1, d), ln_b.reshape(1, d))


PROJ_ROWS = 512
OUT_ROWS = 1024
OUT_SUB_ROWS = 256
ATTN_ROWS = 1024
ATTN_TK = 2048
ATTN_PARTS = 4


def _tiles(t_len, n_row_groups=1):
    tm = min(PROJ_ROWS, t_len)
    tq = min(ATTN_ROWS // n_row_groups, t_len)
    tk = min(ATTN_TK, t_len)
    return tm, tq, tk


def _mla_layer(x, ctx, mod, mod_c, need_ctx, w_in, g_qa, w_qb, g_kva, w_kvb):
    t_len, t_c = x.shape[1], ctx.shape[1]
    tm, tq, tk = _tiles(t_len, 2)
    d = w_in.shape[0]
    o0, o1, o2 = A_Q_LORA, A_Q_LORA + A_KV_LORA, A_Q_LORA + A_KV_LORA + A_ROPE
    hr = A_ROPE // 2
    na = HALF - hr
    w_kpe = w_in[:, o1:o2]
    kpe_pad = _mla_slab(w_kpe[:, :hr], jnp.zeros((d, na), F32), w_kpe[:, hr:], jnp.zeros((d, A_NOPE - na), F32))
    w_in_r = jnp.concatenate([w_in[:, :o1], kpe_pad], axis=1).astype(BF16)
    w_gate = w_in[:, o2:].astype(BF16)
    qb = w_qb.reshape(A_Q_LORA, A_HEADS, A_NOPE + A_ROPE)
    w_qb_r = _mla_slab(qb[:, :, A_NOPE:A_NOPE + hr], qb[:, :, :na], qb[:, :, A_NOPE + hr:], qb[:, :, na:A_NOPE])
    w_qb_r = w_qb_r.reshape(A_Q_LORA, A_HEADS * LANE).astype(BF16)
    kvb = w_kvb.reshape(A_KV_LORA, A_HEADS, A_NOPE + A_V)
    zk = jnp.zeros((A_KV_LORA, A_HEADS, hr), F32)
    k_pad = _mla_slab(zk, kvb[:, :, :na], zk, kvb[:, :, na:A_NOPE]).reshape(A_KV_LORA, A_HEADS * LANE)
    w_kvb_r = jnp.concatenate([k_pad, kvb[:, :, A_NOPE:].reshape(A_KV_LORA, A_HEADS * A_V)], axis=1).astype(BF16)
    q_scale = (A_NOPE + A_ROPE) ** -0.5 * LOG2E
    gq, gk = g_qa.reshape(1, -1), g_kva.reshape(1, -1)
    tabs = _rope_tables(t_len, A_ROPE, "mla")
    q, k, v = _mla_proj_call(x, mod, tabs, w_in_r, gq, w_qb_r, gk, w_kvb_r, q_scale, tm, "mla_proj")
    q_c, k_c, v_c = _mla_proj_call(ctx, mod_c, None, w_in_r, gq, w_qb_r, gk, w_kvb_r, q_scale, t_c, "mla_proj_ctx")
    rows = ((0, None, 0), (1, None, 1))
    maps = dict(n_groups=A_HEADS // 2, q_map=lambda gi, j: 2 * gi + j, k_map=lambda gi, j: 2 * gi + j,
                v_map=lambda gi: gi, rows=rows, out_width=A_HEADS * A_V, lambda_init=0.0)
    o = _dense_attn_call("pair", q, k_c, v_c, k, v, (), tq=tq, tk=tk, name="mla_attn", **maps)
    o_c = _dense_attn_call("pair", q_c, k_c, v_c, None, None, (), tq=t_c, tk=tk, name="mla_attn_ctx", **maps) \
        if need_ctx else None
    return o, w_gate, o_c


def _diff_layer(x, ctx, mod, mod_c, need_ctx, layer_idx, w_in, lam, g_sub):
    t_len, t_c = x.shape[1], ctx.shape[1]
    tm, tq, tk = _tiles(t_len, 2)
    n = 2 * B_HEADS * B_HEAD
    q_scale = B_HEAD ** -0.5 * LOG2E
    segs = ((0, 0, n, "rope", q_scale, None), (1, n, n, "rope", 1.0, None), (2, 2 * n, n, "plain", 1.0, None))
    widths = (n, n, n)
    wb = w_in[:, :3 * n].astype(BF16)
    w_gate = w_in[:, 3 * n:].astype(BF16)
    tabs = _rope_tables(t_len, B_HEAD, "unit")
    q, k, v = _proj_call(x, mod, tabs, wb, None, segs, widths, B_HEAD // 2, tm, "diff_proj")
    q_c, k_c, v_c = _proj_call(ctx, mod_c, None, wb, None, segs, widths, B_HEAD // 2, t_c, "diff_proj_ctx")
    lambda_init = 0.8 - 0.6 * math.exp(-0.3 * layer_idx)
    rows = ((0, "lo", 0), (0, "hi", 0))
    extras = (lam.astype(F32), g_sub.reshape(1, -1).astype(F32))
    maps = dict(n_groups=B_HEADS, q_map=lambda gi, j: gi, k_map=lambda gi, j: gi, v_map=lambda gi: gi,
                rows=rows, out_width=n, lambda_init=lambda_init)
    o = _dense_attn_call("diff", q, k_c, v_c, k, v, extras, tq=tq, tk=tk, name="diff_attn", **maps)
    o_c = _dense_attn_call("diff", q_c, k_c, v_c, None, None, extras, tq=t_c, tk=tk, name="diff_attn_ctx", **maps) \
        if need_ctx else None
    return o, w_gate, o_c


def _gqa_layer(x, ctx, mod, mod_c, need_ctx, w_in, g_q, g_k):
    t_len, t_c = x.shape[1], ctx.shape[1]
    tm, tq, tk = _tiles(t_len, C_HEADS // C_KV_HEADS)
    nq, nkv = C_HEADS * C_HEAD, C_KV_HEADS * C_HEAD
    q_scale = C_HEAD ** -0.5 * LOG2E
    segs = ((0, 0, nq, "rope", q_scale, 0), (1, nq, nkv, "rope", 1.0, 1), (2, nq + nkv, nkv, "plain", 1.0, None))
    widths = (nq, nkv, nkv)
    wb = w_in[:, :nq + 2 * nkv].astype(BF16)
    w_gate = w_in[:, nq + 2 * nkv:].astype(BF16)
    gains = jnp.stack([g_q, g_k]).astype(F32)
    tabs = _rope_tables(t_len, C_HEAD, "unit")
    q, k, v = _proj_call(x, mod, tabs, wb, gains, segs, widths, C_HEAD // 2, tm, "gqa_proj")
    q_c, k_c, v_c = _proj_call(ctx, mod_c, None, wb, gains, segs, widths, C_HEAD // 2, t_c, "gqa_proj_ctx")
    per = C_HEADS // C_KV_HEADS
    rows = tuple((j, None, 0) for j in range(per))
    maps = dict(n_groups=C_KV_HEADS, q_map=lambda gi, j: per * gi + j, k_map=lambda gi, j: gi, v_map=lambda gi: gi,
                rows=rows, out_width=nq, lambda_init=0.0)
    o = _dense_attn_call("stack", q, k_c, v_c, k, v, (), tq=tq, tk=tk, name="gqa_attn", **maps)
    o_c = _dense_attn_call("stack", q_c, k_c, v_c, None, None, (), tq=t_c, tk=tk, name="gqa_attn_ctx", **maps) \
        if need_ctx else None
    return o, w_gate, o_c


def _swa_layer(x, ctx, mod, mod_c, need_ctx, w_in, sinks):
    t_len, t_c = x.shape[1], ctx.shape[1]
    tm, _, _ = _tiles(t_len)
    nq, nkv = D_HEADS * D_HEAD, D_KV_HEADS * D_HEAD
    wk = w_in[:, nq:nq + nkv]
    wv = w_in[:, nq + nkv:nq + 2 * nkv]
    swap = lambda w: jnp.concatenate([w, w[:, D_HEAD:], w[:, :D_HEAD]], axis=1)
    wb = jnp.concatenate([w_in[:, :nq], swap(wk), swap(wv)], axis=1).astype(BF16)
    w_gate = w_in[:, nq + 2 * nkv:].astype(BF16)
    q_scale = D_HEAD ** -0.5 * LOG2E
    segs = ((0, 0, nq, "rope", q_scale, None), (1, nq, 2 * nkv, "rope", 1.0, None),
            (2, nq + 2 * nkv, 2 * nkv, "plain", 1.0, None))
    widths = (nq, 2 * nkv, 2 * nkv)
    tabs = _rope_tables(t_len, D_HEAD, "unit")
    q, k, v = _proj_call(x, mod, tabs, wb, None, segs, widths, D_HEAD // 2, tm, "swa_proj")
    q_c, k_c, v_c = _proj_call(ctx, mod_c, None, wb, None, segs, widths, D_HEAD // 2, t_c, "swa_proj_ctx")
    sinks_log2 = sinks.astype(F32) * LOG2E
    o = _swa_call(q, k_c, v_c, k, v, sinks_log2, WINDOW, "swa_attn")
    o_c = _swa_call(q_c, k_c, v_c, None, None, sinks_log2, t_c, "swa_attn_ctx") if need_ctx else None
    return o, w_gate, o_c


def kernel(x, c, ctx, c_ctx, ada_w, ada_b, out_w, ln_g, ln_b, mla_w_in, mla_g_qa, mla_w_qb, mla_g_kva, mla_w_kvb,
           diff_w_in, diff_lambda, diff_g_sub, gqa_w_in, gqa_g_q, gqa_g_k, swa_w_in, swa_sink):
    depth = ada_w.shape[0]
    bsz, t_len, d = x.shape
    t_c = ctx.shape[1]
    alpha = (2 * depth) ** 0.25
    n_rows = 8
    assert bsz + 1 <= n_rows
    cvec = jnp.concatenate([c, c_ctx[None, :], jnp.zeros((n_rows - bsz - 1, d), F32)], axis=0)
    mods = _adaln_call(cvec, ada_w, ada_b)
    tm_out = min(OUT_ROWS, t_len)
    for i in range(depth):
        kind, j = i % 4, i // 4
        need_ctx = i < depth - 1
        mod = mods[i, 0:bsz][:, None, :]
        mod_c = jnp.broadcast_to(mods[i, bsz][None, None, :], (bsz, 1, 3 * d))
        if kind == 0:
            o, w_gate, o_c = _mla_layer(x, ctx, mod, mod_c, need_ctx, mla_w_in[j], mla_g_qa[j], mla_w_qb[j],
                                        mla_g_kva[j], mla_w_kvb[j])
        elif kind == 1:
            o, w_gate, o_c = _diff_layer(x, ctx, mod, mod_c, need_ctx, i, diff_w_in[j], diff_lambda[j], diff_g_sub[j])
        elif kind == 2:
            o, w_gate, o_c = _gqa_layer(x, ctx, mod, mod_c, need_ctx, gqa_w_in[j], gqa_g_q[j], gqa_g_k[j])
        else:
            o, w_gate, o_c = _swa_layer(x, ctx, mod, mod_c, need_ctx, swa_w_in[j], swa_sink[j])
        wo = out_w[i].astype(BF16)
        x = _out_call(x, o, w_gate, mod, wo, ln_g[i], ln_b[i], alpha, tm_out, "out_proj")
        if need_ctx:
            ctx = _out_call(ctx, o_c, w_gate, mod_c, wo, ln_g[i], ln_b[i], alpha, t_c, "out_proj_ctx")
    return x
```
